```python
import math
import jax
import jax.numpy as jnp
from jax import lax
import numpy as np

D_MODEL = 2048
BATCH = 4
SEQ = 4096
DEPTH = 2
DEC_BATCH = 16
DEC_SEQ = 64
PAST_LEN = 2048

CHUNK = 64
Q_BLOCK = 128
N_EVEN = (DEPTH + 1) // 2
N_ODD = DEPTH // 2
EPS = 1e-6
NEG_INF = -1e30

S5_WIDTH = D_MODEL // 2
S5_GROUP = 16
S5_GROUPS = S5_WIDTH // S5_GROUP
S5_STATE = 64
DIFF_WIDTH = D_MODEL - S5_WIDTH
DIFF_DK = 64
DIFF_DV = 2 * DIFF_DK
DIFF_HEADS = DIFF_WIDTH // DIFF_DV
EVEN_IN = S5_WIDTH + 3 * DIFF_WIDTH

MLA_HEADS = D_MODEL // 128
MLA_NOPE = 128
MLA_ROPE = 64
MLA_V = 128
MLA_Q_RANK = D_MODEL // 4
MLA_KV_RANK = D_MODEL // 4
ODD_IN = MLA_Q_RANK + MLA_KV_RANK + MLA_ROPE
ROPE_BASE = 10000.0

D_FF = 11 * D_MODEL // 4
CONV_W = 3

kernel_name = 'hybrid_s5_diffattn_mla_convffn_stream_step'


def rmsnorm(x, g):
    x32 = x.astype(jnp.float32)
    y = x32 * lax.rsqrt(jnp.mean(x32 * x32, axis=-1, keepdims=True) + EPS)
    return (y * g.astype(jnp.float32)).astype(x.dtype)


def chunk_mask(q_pos, k_pos):
    return (k_pos[None, :] // CHUNK) <= (q_pos[:, None] // CHUNK)


def masked_softmax(s, mask):
    return jax.nn.softmax(jnp.where(mask, s, NEG_INF), axis=-1)


def rope(x, pos):
    half = x.shape[-1] // 2
    inv = ROPE_BASE ** (-jnp.arange(half, dtype=jnp.float32) / half)
    ang = pos.astype(jnp.float32)[:, None] * inv[None, :]
    shape = (1, pos.shape[0]) + (1,) * (x.ndim - 3) + (half,)
    cos = jnp.cos(ang).reshape(shape)
    sin = jnp.sin(ang).reshape(shape)
    x32 = x.astype(jnp.float32)
    x1, x2 = x32[..., :half], x32[..., half:]
    return jnp.concatenate([x1 * cos - x2 * sin, x2 * cos + x1 * sin], axis=-1).astype(x.dtype)


def sweep_query_blocks(fn, qs, q_pos):
    t = q_pos.shape[0]
    if t <= Q_BLOCK:
        return fn(qs, q_pos)
    n_blk = t // Q_BLOCK

    def body(i):
        start = i * Q_BLOCK
        blk = tuple(lax.dynamic_slice_in_dim(q, start, Q_BLOCK, axis=1) for q in qs)
        return fn(blk, lax.dynamic_slice_in_dim(q_pos, start, Q_BLOCK))

    out = lax.map(body, jnp.arange(n_blk, dtype=jnp.int32))
    out = jnp.moveaxis(out, 0, 1)
    return out.reshape((out.shape[0], t) + out.shape[3:])


def diff_attention(q1, q2, k1, k2, v, q_pos, k_pos, lam):
    scale = DIFF_DK ** -0.5

    def block(qs, qp):
        b1, b2 = qs
        mask = chunk_mask(qp, k_pos)
        s1 = jnp.einsum('bqhd,bkhd->bhqk', b1, k1, preferred_element_type=jnp.float32) * scale
        s2 = jnp.einsum('bqhd,bkhd->bhqk', b2, k2, preferred_element_type=jnp.float32) * scale
        p = masked_softmax(s1, mask) - lam * masked_softmax(s2, mask)
        return jnp.einsum('bhqk,bkhd->bqhd', p.astype(v.dtype), v)

    return sweep_query_blocks(block, (q1, q2), q_pos)


def mla_attention(q_nope, q_pe, ckv, kpe, w_uk, w_uv, q_pos, k_pos):
    scale = (MLA_NOPE + MLA_ROPE) ** -0.5

    def block(qs, qp):
        qn, qr = qs
        q_lat = jnp.einsum('bqhn,rhn->bqhr', qn, w_uk)
        s = (jnp.einsum('bqhr,bkr->bhqk', q_lat, ckv, preferred_element_type=jnp.float32)
             + jnp.einsum('bqhp,bkp->bhqk', qr, kpe, preferred_element_type=jnp.float32)) * scale
        p = masked_softmax(s, chunk_mask(qp, k_pos)).astype(ckv.dtype)
        o_lat = jnp.einsum('bhqk,bkr->bqhr', p, ckv)
        return jnp.einsum('bqhr,rhv->bqhv', o_lat, w_uv)

    return sweep_query_blocks(block, (q_nope, q_pe), q_pos)


def s5_mix(u, h0_re, h0_im, w, i):
    f32 = jnp.float32
    b, t, _ = u.shape
    u32 = u.astype(f32).reshape(b, t, S5_GROUPS, S5_GROUP)
    lam_re = jnp.minimum(w['s5_a_re'][i].astype(f32), -1e-4)
    lam_im = w['s5_a_im'][i].astype(f32)
    dt = jnp.exp(w['s5_log_dt'][i].astype(f32))[:, None]
    mag = jnp.exp(lam_re * dt)
    ab_re = mag * jnp.cos(lam_im * dt)
    ab_im = mag * jnp.sin(lam_im * dt)
    den = lam_re * lam_re + lam_im * lam_im
    co_re = ((ab_re - 1.0) * lam_re + ab_im * lam_im) / den
    co_im = (ab_im * lam_re - (ab_re - 1.0) * lam_im) / den
    b_re = w['s5_b_re'][i].astype(f32)
    b_im = w['s5_b_im'][i].astype(f32)
    bb_re = co_re[..., None] * b_re - co_im[..., None] * b_im
    bb_im = co_re[..., None] * b_im + co_im[..., None] * b_re
    x_re = jnp.einsum('gnp,btgp->tbgn', bb_re, u32)
    x_im = jnp.einsum('gnp,btgp->tbgn', bb_im, u32)
    a_re_t = jnp.broadcast_to(ab_re, (t, 1) + ab_re.shape)
    a_im_t = jnp.broadcast_to(ab_im, (t, 1) + ab_im.shape)

    def combine(e, l):
        ar, ai, xr, xi = e
        br, bi, yr, yi = l
        return (br * ar - bi * ai, br * ai + bi * ar,
                br * xr - bi * xi + yr, br * xi + bi * xr + yi)

    p_re, p_im, h_re, h_im = lax.associative_scan(combine, (a_re_t, a_im_t, x_re, x_im), axis=0)
    h0r = h0_re.astype(f32)
    h0i = h0_im.astype(f32)
    h_re = h_re + p_re * h0r - p_im * h0i
    h_im = h_im + p_re * h0i + p_im * h0r
    c_re = w['s5_c_re'][i].astype(f32)
    c_im = w['s5_c_im'][i].astype(f32)
    y = (jnp.einsum('gpn,tbgn->btgp', c_re, h_re) - jnp.einsum('gpn,tbgn->btgp', c_im, h_im)
         + w['s5_d'][i].astype(f32) * u32)
    y = y.reshape(b, t, S5_WIDTH)
    z = jax.nn.gelu(y)
    out = z * jax.nn.sigmoid(z @ w['s5_w_glu'][i].astype(f32) + w['s5_b_glu'][i].astype(f32))
    return out.astype(u.dtype), h_re[-1], h_im[-1]


def even_mixer(h, h0_re, h0_im, k_past, v_past, q_pos, k_pos, w, i, lam_init):
    b, t, _ = h.shape
    proj = h @ w['w_in_even'][i]
    u, q, k, v = jnp.split(proj, [S5_WIDTH, S5_WIDTH + DIFF_WIDTH, S5_WIDTH + 2 * DIFF_WIDTH], axis=-1)
    q = q.reshape(b, t, DIFF_HEADS, 2 * DIFF_DK)
    k = k.reshape(b, t, DIFF_HEADS, 2 * DIFF_DK)
    v = v.reshape(b, t, DIFF_HEADS, DIFF_DV)
    y_s5, hT_re, hT_im = s5_mix(u, h0_re, h0_im, w, i)
    k_all = k if k_past is None else jnp.concatenate([k_past, k], axis=1)
    v_all = v if v_past is None else jnp.concatenate([v_past, v], axis=1)
    f32 = jnp.float32
    lam = (jnp.exp(jnp.sum(w['diff_lambda_q1'][i].astype(f32) * w['diff_lambda_k1'][i].astype(f32)))
           - jnp.exp(jnp.sum(w['diff_lambda_q2'][i].astype(f32) * w['diff_lambda_k2'][i].astype(f32)))
           + lam_init)
    o = diff_attention(q[..., :DIFF_DK], q[..., DIFF_DK:], k_all[..., :DIFF_DK], k_all[..., DIFF_DK:],
                       v_all, q_pos, k_pos, lam)
    o = rmsnorm(o, w['diff_subln'][i]) * (1.0 - lam_init)
    mixed = jnp.concatenate([y_s5, o.reshape(b, t, DIFF_WIDTH)], axis=-1)
    return mixed @ w['w_out_even'][i], hT_re, hT_im, k, v


def odd_mixer(h, ckv_past, kpe_past, q_pos, k_pos, w, i):
    b, t, _ = h.shape
    cq, ckv, kpe = jnp.split(h @ w['w_in_odd'][i], [MLA_Q_RANK, MLA_Q_RANK + MLA_KV_RANK], axis=-1)
    q = jnp.einsum('btr,rhd->bthd', rmsnorm(cq, w['mla_q_norm'][i]), w['mla_w_uq'][i])
    q_nope = q[..., :MLA_NOPE]
    q_pe = rope(q[..., MLA_NOPE:], q_pos)
    ckv = rmsnorm(ckv, w['mla_kv_norm'][i])
    kpe = rope(kpe, q_pos)
    ckv_all = ckv if ckv_past is None else jnp.concatenate([ckv_past, ckv], axis=1)
    kpe_all = kpe if kpe_past is None else jnp.concatenate([kpe_past, kpe], axis=1)
    w_ukv = w['mla_w_ukv'][i]
    o = mla_attention(q_nope, q_pe, ckv_all, kpe_all, w_ukv[..., :MLA_NOPE], w_ukv[..., MLA_NOPE:],
                      q_pos, k_pos)
    return o.reshape(b, t, MLA_HEADS * MLA_V) @ w['w_out_odd'][i], ckv, kpe


def conv_ffn(h, conv_state, w_in, conv_w, conv_b, w_down):
    val, gate = jnp.split(h @ w_in, 2, axis=-1)
    t = gate.shape[1]
    g = jnp.concatenate([conv_state.astype(gate.dtype), gate], axis=1)
    c = conv_b
    for j in range(CONV_W):
        c = c + conv_w[j] * g[:, j:j + t]
    out = (jax.nn.silu(c) * val) @ w_down
    return out, g[:, t:]


def trunk(x, s5_re0, s5_im0, k_past, v_past, ckv_past, kpe_past, conv0, q_pos, k_pos, w):
    s5_re, s5_im, ks, vs, ckvs, kpes, convs = [], [], [], [], [], [], []
    for layer in range(DEPTH):
        i = layer // 2
        h = rmsnorm(x, w['norm_mix'][layer])
        if layer % 2 == 0:
            lam_init = 0.8 - 0.6 * math.exp(-0.3 * layer)
            out, hr, hi, k, v = even_mixer(
                h, s5_re0[i], s5_im0[i],
                None if k_past is None else k_past[i],
                None if v_past is None else v_past[i],
                q_pos, k_pos, w, i, lam_init)
            s5_re.append(hr)
            s5_im.append(hi)
            ks.append(k)
            vs.append(v)
        else:
            out, ckv, kpe = odd_mixer(
                h,
                None if ckv_past is None else ckv_past[i],
                None if kpe_past is None else kpe_past[i],
                q_pos, k_pos, w, i)
            ckvs.append(ckv)
            kpes.append(kpe)
        x = x + out
        f, cs = conv_ffn(rmsnorm(x, w['norm_ffn'][layer]), conv0[layer], w['ffn_w_in'][layer],
                         w['ffn_conv_w'][layer], w['ffn_conv_b'][layer], w['ffn_w_down'][layer])
        x = x + f
        convs.append(cs)
    return (rmsnorm(x, w['norm_final']), jnp.stack(s5_re), jnp.stack(s5_im), jnp.stack(ks),
            jnp.stack(vs), jnp.stack(ckvs), jnp.stack(kpes), jnp.stack(convs))


def setup_inputs(seed: int = 0) -> dict:
    key = jax.random.key(seed)
    keys = iter(jax.random.split(key, 48))
    f32 = jnp.float32

    def normal(shape, scale):
        return scale * jax.random.normal(next(keys), shape, f32)

    def gain(shape):
        return 1.0 + 0.02 * jax.random.normal(next(keys), shape, f32)

    a_im0 = math.pi * jnp.arange(S5_STATE, dtype=f32)
    return {
        'x_prompt': normal((BATCH, SEQ, D_MODEL), 1.0),
        'x_sample': normal((DEC_BATCH, DEC_SEQ, D_MODEL), 1.0),
        'state_s5_re': normal((N_EVEN, DEC_BATCH, S5_GROUPS, S5_STATE), 0.5),
        'state_s5_im': normal((N_EVEN, DEC_BATCH, S5_GROUPS, S5_STATE), 0.5),
        'cache_diff_k': normal((N_EVEN, DEC_BATCH, PAST_LEN, DIFF_HEADS, 2 * DIFF_DK), 1.0),
        'cache_diff_v': normal((N_EVEN, DEC_BATCH, PAST_LEN, DIFF_HEADS, DIFF_DV), 1.0),
        'cache_mla_ckv': normal((N_ODD, DEC_BATCH, PAST_LEN, MLA_KV_RANK), 1.0),
        'cache_mla_kpe': normal((N_ODD, DEC_BATCH, PAST_LEN, MLA_ROPE), 1.0),
        'state_ffn_conv': normal((DEPTH, DEC_BATCH, CONV_W - 1, D_FF), 1.0),
        'norm_mix': gain((DEPTH, D_MODEL)),
        'norm_ffn': gain((DEPTH, D_MODEL)),
        'norm_final': gain((D_MODEL,)),
        'w_in_even': normal((N_EVEN, D_MODEL, EVEN_IN), D_MODEL ** -0.5),
        'w_out_even': normal((N_EVEN, D_MODEL, D_MODEL), D_MODEL ** -0.5),
        's5_a_re': -0.5 + normal((N_EVEN, S5_GROUPS, S5_STATE), 0.01),
        's5_a_im': a_im0 + normal((N_EVEN, S5_GROUPS, S5_STATE), 0.01),
        's5_b_re': normal((N_EVEN, S5_GROUPS, S5_STATE, S5_GROUP), (2 * S5_GROUP) ** -0.5),
        's5_b_im': normal((N_EVEN, S5_GROUPS, S5_STATE, S5_GROUP), (2 * S5_GROUP) ** -0.5),
        's5_c_re': normal((N_EVEN, S5_GROUPS, S5_GROUP, S5_STATE), (2 * S5_STATE) ** -0.5),
        's5_c_im': normal((N_EVEN, S5_GROUPS, S5_GROUP, S5_STATE), (2 * S5_STATE) ** -0.5),
        's5_d': normal((N_EVEN, S5_GROUPS, S5_GROUP), 1.0),
        's5_log_dt': jax.random.uniform(next(keys), (N_EVEN, S5_GROUPS), f32,
                                        math.log(1e-3), math.log(1e-1)),
        's5_w_glu': normal((N_EVEN, S5_WIDTH, S5_WIDTH), S5_WIDTH ** -0.5),
        's5_b_glu': normal((N_EVEN, S5_WIDTH), 0.01),
        'diff_lambda_q1': normal((N_EVEN, DIFF_DK), 0.1),
        'diff_lambda_k1': normal((N_EVEN, DIFF_DK), 0.1),
        'diff_lambda_q2': normal((N_EVEN, DIFF_DK), 0.1),
        'diff_lambda_k2': normal((N_EVEN, DIFF_DK), 0.1),
        'diff_subln': gain((N_EVEN, DIFF_DV)),
        'w_in_odd': normal((N_ODD, D_MODEL, ODD_IN), D_MODEL ** -0.5),
        'mla_q_norm': gain((N_ODD, MLA_Q_RANK)),
        'mla_kv_norm': gain((N_ODD, MLA_KV_RANK)),
        'mla_w_uq': normal((N_ODD, MLA_Q_RANK, MLA_HEADS, MLA_NOPE + MLA_ROPE), MLA_Q_RANK ** -0.5),
        'mla_w_ukv': normal((N_ODD, MLA_KV_RANK, MLA_HEADS, MLA_NOPE + MLA_V), MLA_KV_RANK ** -0.5),
        'w_out_odd': normal((N_ODD, MLA_HEADS * MLA_V, D_MODEL), (MLA_HEADS * MLA_V) ** -0.5),
        'ffn_w_in': normal((DEPTH, D_MODEL, 2 * D_FF), D_MODEL ** -0.5),
        'ffn_conv_w': normal((DEPTH, CONV_W, D_FF), CONV_W ** -0.5),
        'ffn_conv_b': normal((DEPTH, D_FF), 0.01),
        'ffn_w_down': normal((DEPTH, D_FF, D_MODEL), D_FF ** -0.5),
    }


def reference(x_prompt, x_sample, state_s5_re, state_s5_im, cache_diff_k, cache_diff_v,
              cache_mla_ckv, cache_mla_kpe, state_ffn_conv, norm_mix, norm_ffn, norm_final,
              w_in_even, w_out_even, s5_a_re, s5_a_im, s5_b_re, s5_b_im, s5_c_re, s5_c_im,
              s5_d, s5_log_dt, s5_w_glu, s5_b_glu, diff_lambda_q1, diff_lambda_k1,
              diff_lambda_q2, diff_lambda_k2, diff_subln, w_in_odd, mla_q_norm, mla_kv_norm,
              mla_w_uq, mla_w_ukv, w_out_odd, ffn_w_in, ffn_conv_w, ffn_conv_b, ffn_w_down):
    w = {
        'norm_mix': norm_mix, 'norm_ffn': norm_ffn, 'norm_final': norm_final,
        'w_in_even': w_in_even, 'w_out_even': w_out_even,
        's5_a_re': s5_a_re, 's5_a_im': s5_a_im, 's5_b_re': s5_b_re, 's5_b_im': s5_b_im,
        's5_c_re': s5_c_re, 's5_c_im': s5_c_im, 's5_d': s5_d, 's5_log_dt': s5_log_dt,
        's5_w_glu': s5_w_glu, 's5_b_glu': s5_b_glu,
        'diff_lambda_q1': diff_lambda_q1, 'diff_lambda_k1': diff_lambda_k1,
        'diff_lambda_q2': diff_lambda_q2, 'diff_lambda_k2': diff_lambda_k2,
        'diff_subln': diff_subln,
        'w_in_odd': w_in_odd, 'mla_q_norm': mla_q_norm, 'mla_kv_norm': mla_kv_norm,
        'mla_w_uq': mla_w_uq, 'mla_w_ukv': mla_w_ukv, 'w_out_odd': w_out_odd,
        'ffn_w_in': ffn_w_in, 'ffn_conv_w': ffn_conv_w, 'ffn_conv_b': ffn_conv_b,
        'ffn_w_down': ffn_w_down,
    }
    b_p, t_p = x_prompt.shape[0], x_prompt.shape[1]
    q_pos_p = jnp.arange(t_p, dtype=jnp.int32)
    s5_zero = jnp.zeros((N_EVEN, b_p, S5_GROUPS, S5_STATE), jnp.float32)
    conv_zero = jnp.zeros((DEPTH, b_p, CONV_W - 1, D_FF), x_prompt.dtype)
    (y_prompt, s5_re_prompt, s5_im_prompt, diff_k_prompt, diff_v_prompt,
     mla_ckv_prompt, mla_kpe_prompt, ffn_conv_prompt) = trunk(
        x_prompt, s5_zero, s5_zero, None, None, None, None, conv_zero, q_pos_p, q_pos_p, w)
    past_len = cache_diff_k.shape[2]
    t_s = x_sample.shape[1]
    q_pos_s = past_len + jnp.arange(t_s, dtype=jnp.int32)
    k_pos_s = jnp.arange(past_len + t_s, dtype=jnp.int32)
    (y_sample, s5_re_sample, s5_im_sample, diff_k_sample, diff_v_sample,
     mla_ckv_sample, mla_kpe_sample, ffn_conv_sample) = trunk(
        x_sample, state_s5_re, state_s5_im, cache_diff_k, cache_diff_v, cache_mla_ckv,
        cache_mla_kpe, state_ffn_conv, q_pos_s, k_pos_s, w)
    return (y_prompt, y_sample, s5_re_prompt, s5_im_prompt, s5_re_sample, s5_im_sample,
            diff_k_prompt, diff_v_prompt, diff_k_sample, diff_v_sample,
            mla_ckv_prompt, mla_kpe_prompt, mla_ckv_sample, mla_kpe_sample,
            ffn_conv_prompt, ffn_conv_sample)
```

```python
import functools
import math

import jax
import jax.numpy as jnp
from jax import lax
from jax.experimental import pallas as pl
from jax.experimental.pallas import tpu as pltpu

F32 = jnp.float32
BF16 = jnp.bfloat16

CHUNK = 64
EPS = 1e-6
NEG_INF = -1e30
ROPE_BASE = 10000.0
S5_GROUP = 16
DIFF_DK = 64
MLA_NOPE = 128
MLA_ROPE = 64
MLA_V = 128
CONV_W = 3

LANES = 128
SUBLANES = 8
MXU_DIM = 256
VMEM_LIMIT_BYTES = 56 * 1024 * 1024

S5_L = 16
S5_GB = MXU_DIM // S5_GROUP


def _cparams(n_axes):
    return pltpu.CompilerParams(
        dimension_semantics=("arbitrary",) * n_axes,
        vmem_limit_bytes=VMEM_LIMIT_BYTES)


def _resident(shape):
    nd = len(shape)
    return pl.BlockSpec(shape, lambda *_: (0,) * nd, pipeline_mode=pl.Buffered(1))


def _tile(n, pref):
    t = min(n, pref)
    while n % t:
        t //= 2
    return t


def _dot(a, b):
    return jnp.dot(a, b, preferred_element_type=F32)


def _dot_t(a, b):
    return lax.dot_general(a, b, (((1,), (1,)), ((), ())), preferred_element_type=F32)


def _rms(x, g):
    ms = jnp.mean(x * x, axis=-1, keepdims=True)
    return x * lax.rsqrt(ms + EPS) * g


def _even_in_kernel(x_ref, g_ref, w_ref, u_ref, q_ref, k_ref, v_ref, kb_ref, vb_ref, *, s5w, dw, qscale):
    xn = _rms(x_ref[...], g_ref[...]).astype(BF16)
    u_ref[...] = _dot(xn, w_ref[:, 0:s5w])
    q_ref[...] = (_dot(xn, w_ref[:, s5w:s5w + dw]) * qscale).astype(BF16)
    k = _dot(xn, w_ref[:, s5w + dw:s5w + 2 * dw])
    k_ref[...] = k
    kb_ref[...] = k.astype(BF16)
    v = _dot(xn, w_ref[:, s5w + 2 * dw:s5w + 3 * dw])
    v_ref[...] = v
    vb_ref[...] = v.astype(BF16)


def _even_in(x, g, w, s5w, dw):
    n, d = x.shape
    tm = _tile(n, 512)
    row = lambda i: (i, 0)
    kern = functools.partial(_even_in_kernel, s5w=s5w, dw=dw, qscale=DIFF_DK ** -0.5)
    return pl.pallas_call(
        kern,
        grid=(n // tm,),
        in_specs=[pl.BlockSpec((tm, d), row), _resident(g.shape), _resident(w.shape)],
        out_specs=[pl.BlockSpec((tm, s5w), row), pl.BlockSpec((tm, dw), row), pl.BlockSpec((tm, dw), row),
                   pl.BlockSpec((tm, dw), row), pl.BlockSpec((tm, dw), row), pl.BlockSpec((tm, dw), row)],
        out_shape=[jax.ShapeDtypeStruct((n, s5w), F32), jax.ShapeDtypeStruct((n, dw), BF16),
                   jax.ShapeDtypeStruct((n, dw), F32), jax.ShapeDtypeStruct((n, dw), F32),
                   jax.ShapeDtypeStruct((n, dw), BF16), jax.ShapeDtypeStruct((n, dw), BF16)],
        compiler_params=_cparams(1),
        name="even_in",
    )(x, g, w)


def _s5_kernel(u2_ref, h0_ref, lre_ref, lim_ref, dt_ref, bre_ref, bim_ref, cbd_ref, d_ref, wglu_ref, bglu_ref,
               y2_ref, ht_ref,
               a_ref, a16_ref, bbd_ref, xs_ref, xe_ref, hs_ref, carry_ref, ys_ref,
               *, rows, width, cps, ngb):
    t = pl.program_id(0)
    nblk = ngb * 16
    half = 8
    gw = MXU_DIM

    @pl.when(t == 0)
    def _prepare():
        carry_ref[...] = jnp.zeros_like(carry_ref)
        for gb in range(ngb):
            for k in range(half):
                lre = jnp.minimum(lre_ref[gb * half + k], -1e-4)
                lim = lim_ref[gb * half + k]
                dt = jnp.exp(dt_ref[gb * half + k])
                mag = jnp.exp(lre * dt)
                are = mag * jnp.cos(lim * dt)
                aim = mag * jnp.sin(lim * dt)
                den = lre * lre + lim * lim
                cre = ((are - 1.0) * lre + aim * lim) / den
                cim = (aim * lre - (are - 1.0) * lim) / den
                ire, iim = gb * 16 + k, gb * 16 + half + k
                a_ref[ire] = are
                a_ref[iim] = aim
                pre, pim = are, aim
                for _ in range(4):
                    pre, pim = pre * pre - pim * pim, 2.0 * pre * pim
                a16_ref[ire] = pre
                a16_ref[iim] = pim
                br = bre_ref[gb, :, k * LANES:(k + 1) * LANES]
                bi = bim_ref[gb, :, k * LANES:(k + 1) * LANES]
                bbd_ref[gb, :, k * LANES:(k + 1) * LANES] = (cre * br - cim * bi).astype(BF16)
                bbd_ref[gb, :, (half + k) * LANES:(half + k + 1) * LANES] = (cre * bi + cim * br).astype(BF16)

    for gb in range(ngb):
        lhs = jnp.concatenate(
            [u2_ref[:, l * width + gb * gw:l * width + (gb + 1) * gw] for l in range(S5_L)], axis=0).astype(BF16)
        x = _dot(lhs, bbd_ref[gb])
        for j in range(16):
            xs_ref[gb * 16 + j] = x[:, j * LANES:(j + 1) * LANES]

    ncplx = ngb * half

    def _blocks(c):
        gb = c // half
        k = c % half
        return gb * 16 + k, gb * 16 + half + k

    def pass_a(c, _):
        ire, iim = _blocks(c)
        are, aim = a_ref[ire], a_ref[iim]
        hre = xs_ref[ire, 0:rows, :]
        him = xs_ref[iim, 0:rows, :]
        for l in range(1, S5_L):
            xre = xs_ref[ire, l * rows:(l + 1) * rows, :]
            xim = xs_ref[iim, l * rows:(l + 1) * rows, :]
            hre, him = are * hre - aim * him + xre, are * him + aim * hre + xim
        xe_ref[ire] = hre
        xe_ref[iim] = him
        return 0

    lax.fori_loop(0, ncplx, pass_a, 0)

    base_row = t * rows
    for grp in range(ncplx // 8):
        ire0 = (grp // (half // 8)) * 16 + (grp % (half // 8)) * 8
        iim0 = ire0 + half
        a16re = a16_ref[ire0:ire0 + 8]
        a16im = a16_ref[iim0:iim0 + 8]

        def scan_body(r, carry, ire0=ire0, iim0=iim0, a16re=a16re, a16im=a16im):
            cre, cim = carry
            gr = base_row + r
            seq = gr // cps - (base_row // cps)
            is_start = (gr % cps) == 0
            cre = jnp.where(is_start, h0_ref[seq, ire0:ire0 + 8], cre)
            cim = jnp.where(is_start, h0_ref[seq, iim0:iim0 + 8], cim)
            hs_ref[ire0:ire0 + 8, pl.ds(r, 1), :] = cre
            hs_ref[iim0:iim0 + 8, pl.ds(r, 1), :] = cim
            xre = xe_ref[ire0:ire0 + 8, pl.ds(r, 1), :]
            xim = xe_ref[iim0:iim0 + 8, pl.ds(r, 1), :]
            nre = a16re * cre - a16im * cim + xre
            nim = a16re * cim + a16im * cre + xim

            @pl.when((gr % cps) == cps - 1)
            def _():
                ht_ref[seq, ire0:ire0 + 8] = nre
                ht_ref[seq, iim0:iim0 + 8] = nim

            return nre, nim

        cre, cim = lax.fori_loop(0, rows, scan_body, (carry_ref[ire0:ire0 + 8], carry_ref[iim0:iim0 + 8]))
        carry_ref[ire0:ire0 + 8] = cre
        carry_ref[iim0:iim0 + 8] = cim

    def pass_b(c, _):
        ire, iim = _blocks(c)
        are, aim = a_ref[ire], a_ref[iim]
        hre = hs_ref[ire]
        him = hs_ref[iim]
        for l in range(S5_L):
            xre = xs_ref[ire, l * rows:(l + 1) * rows, :]
            xim = xs_ref[iim, l * rows:(l + 1) * rows, :]
            hre, him = are * hre - aim * him + xre, are * him + aim * hre + xim
            xs_ref[ire, l * rows:(l + 1) * rows, :] = hre
            xs_ref[iim, l * rows:(l + 1) * rows, :] = him
        return 0

    lax.fori_loop(0, ncplx, pass_b, 0)

    for gb in range(ngb):
        h = jnp.concatenate([xs_ref[gb * 16 + j].astype(BF16) for j in range(16)], axis=1)
        ys_ref[:, gb * gw:(gb + 1) * gw] = _dot(h, cbd_ref[gb])
    ust = jnp.concatenate([u2_ref[:, l * width:(l + 1) * width] for l in range(S5_L)], axis=0)
    y = ys_ref[...] + d_ref[...] * ust
    z = jax.nn.gelu(y)
    gate = jax.nn.sigmoid(_dot(z.astype(BF16), wglu_ref[...]) + bglu_ref[...])
    out = (z * gate).astype(BF16)
    for l in range(S5_L):
        y2_ref[:, l * width:(l + 1) * width] = out[l * rows:(l + 1) * rows]


def _s5_layout(p, ngb, n_state):
    return p.reshape(ngb * (S5_GB * n_state // LANES), 1, LANES)


def _s5_state_to_blocks(s, ngb):
    b = s.shape[0]
    return s.reshape(b, ngb, 8, 1, LANES)


def _s5(u, h0_re, h0_im, seq_len, w):
    n, width = u.shape
    groups, n_state = w['a_re'].shape
    assert n_state * S5_GB == 8 * LANES and width == groups * S5_GROUP
    ngb = groups // S5_GB
    nseq = n // seq_len
    cps = seq_len // S5_L
    nrow = n // S5_L
    rows = _tile(nrow, 32)
    assert rows % SUBLANES == 0 and (cps % rows == 0 or rows % cps == 0)
    spt = max(1, rows // cps)
    tps = max(1, cps // rows)
    nblk = ngb * 16
    sw = S5_GB * n_state

    u2 = u.reshape(nrow, S5_L * width)
    h0 = jnp.concatenate([_s5_state_to_blocks(h0_re, ngb), _s5_state_to_blocks(h0_im, ngb)], axis=2)
    h0 = h0.reshape(nseq, nblk, 1, LANES)
    lre = _s5_layout(w['a_re'], ngb, n_state)
    lim = _s5_layout(w['a_im'], ngb, n_state)
    dt = _s5_layout(jnp.broadcast_to(w['log_dt'][:, None], (groups, n_state)), ngb, n_state)
    eye = jnp.eye(S5_GB, dtype=F32)

    def bdiag_b(b):
        bb = b.reshape(ngb, S5_GB, n_state, S5_GROUP)
        return jnp.einsum('agnp,gh->agphn', bb, eye).reshape(ngb, S5_GB * S5_GROUP, sw)

    def bdiag_c(c):
        cc = c.reshape(ngb, S5_GB, S5_GROUP, n_state)
        return jnp.einsum('agpn,gh->ahngp', cc, eye).reshape(ngb, sw, S5_GB * S5_GROUP)

    bre = bdiag_b(w['b_re'])
    bim = bdiag_b(w['b_im'])
    cbd = jnp.concatenate([bdiag_c(w['c_re']), -bdiag_c(w['c_im'])], axis=1).astype(BF16)
    d = w['d'].reshape(1, width)
    wglu = w['w_glu'].astype(BF16)
    bglu = w['b_glu'].reshape(1, width)

    seq_idx = (lambda t: (t // tps, 0, 0, 0)) if tps > 1 else (lambda t: (t, 0, 0, 0))
    kern = functools.partial(_s5_kernel, rows=rows, width=width, cps=cps, ngb=ngb)
    y2, ht = pl.pallas_call(
        kern,
        grid=(nrow // rows,),
        in_specs=[pl.BlockSpec((rows, S5_L * width), lambda t: (t, 0)),
                  pl.BlockSpec((spt, nblk, 1, LANES), seq_idx),
                  _resident(lre.shape), _resident(lim.shape), _resident(dt.shape),
                  _resident(bre.shape), _resident(bim.shape), _resident(cbd.shape),
                  _resident(d.shape), _resident(wglu.shape), _resident(bglu.shape)],
        out_specs=[pl.BlockSpec((rows, S5_L * width), lambda t: (t, 0)),
                   pl.BlockSpec((spt, nblk, 1, LANES), seq_idx)],
        out_shape=[jax.ShapeDtypeStruct((nrow, S5_L * width), BF16),
                   jax.ShapeDtypeStruct((nseq, nblk, 1, LANES), F32)],
        scratch_shapes=[pltpu.VMEM((nblk, 1, LANES), F32),
                        pltpu.VMEM((nblk, 1, LANES), F32),
                        pltpu.VMEM((ngb, MXU_DIM, 2 * sw), BF16),
                        pltpu.VMEM((nblk, S5_L * rows, LANES), F32),
                        pltpu.VMEM((nblk, rows, LANES), F32),
                        pltpu.VMEM((nblk, rows, LANES), F32),
                        pltpu.VMEM((nblk, 1, LANES), F32),
                        pltpu.VMEM((S5_L * rows, width), F32)],
        compiler_params=_cparams(1),
        name="s5_mix",
    )(u2, h0, lre, lim, dt, bre, bim, cbd, d, wglu, bglu)
    ht = ht.reshape(nseq, ngb, 2, groups // ngb, n_state)
    ht_re = ht[:, :, 0].reshape(nseq, groups, n_state)
    ht_im = ht[:, :, 1].reshape(nseq, groups, n_state)
    return y2.reshape(n, width), ht_re, ht_im


def _diff_lambda(lq1_ref, lk1_ref, lq2_ref, lk2_ref, lam_init):
    s1 = jnp.sum(lq1_ref[...] * lk1_ref[...], axis=-1, keepdims=True)
    s2 = jnp.sum(lq2_ref[...] * lk2_ref[...], axis=-1, keepdims=True)
    return jnp.exp(s1) - jnp.exp(s2) + lam_init


def _split_maps(q):
    lane = lax.broadcasted_iota(jnp.int32, q.shape, 1)
    zero = jnp.zeros_like(q)
    return jnp.where(lane < DIFF_DK, q, zero), jnp.where(lane >= DIFF_DK, q, zero)


def _chunk_mask(q0, k0, tq, tk):
    qc = (q0 + lax.broadcasted_iota(jnp.int32, (tq, tk), 0)) // CHUNK
    kc = (k0 + lax.broadcasted_iota(jnp.int32, (tq, tk), 1)) // CHUNK
    return kc <= qc


def _subln(o, g, lam_init):
    return (_rms(o, g) * (1.0 - lam_init)).astype(BF16)


def _diff_prompt_kernel(q_ref, k_ref, v_ref, lq1_ref, lk1_ref, lq2_ref, lk2_ref, g_ref, o_ref,
                        m1_ref, l1_ref, a1_ref, m2_ref, l2_ref, a2_ref, *, tq, lam_init):
    i = pl.program_id(2)
    q1, q2 = _split_maps(q_ref[...])
    stats = ((q1, m1_ref, l1_ref, a1_ref), (q2, m2_ref, l2_ref, a2_ref))

    kd = k_ref[pl.ds(pl.multiple_of(i * tq, tq), tq), :]
    vd = v_ref[pl.ds(pl.multiple_of(i * tq, tq), tq), :]
    mask = _chunk_mask(0, 0, tq, tq)
    for qm, m_ref, l_ref, a_ref in stats:
        s = jnp.where(mask, _dot_t(qm, kd), NEG_INF)
        m = jnp.max(s, axis=-1, keepdims=True)
        p = jnp.exp(s - m)
        m_ref[...] = m
        l_ref[...] = jnp.sum(p, axis=-1, keepdims=True)
        a_ref[...] = _dot(p.astype(BF16), vd)

    def body(kt, _):
        kb = k_ref[pl.ds(pl.multiple_of(kt * tq, tq), tq), :]
        vb = v_ref[pl.ds(pl.multiple_of(kt * tq, tq), tq), :]
        for qm, m_ref, l_ref, a_ref in stats:
            s = _dot_t(qm, kb)
            m_prev = m_ref[...]
            m = jnp.maximum(m_prev, jnp.max(s, axis=-1, keepdims=True))
            alpha = jnp.exp(m_prev - m)
            p = jnp.exp(s - m)
            m_ref[...] = m
            l_ref[...] = alpha * l_ref[...] + jnp.sum(p, axis=-1, keepdims=True)
            a_ref[...] = alpha * a_ref[...] + _dot(p.astype(BF16), vb)
        return 0

    lax.fori_loop(0, i, body, 0)

    lam = _diff_lambda(lq1_ref, lk1_ref, lq2_ref, lk2_ref, lam_init)
    o = a1_ref[...] / l1_ref[...] - lam * (a2_ref[...] / l2_ref[...])
    o_ref[...] = _subln(o, g_ref[...], lam_init)


def _diff_prompt(q, kb, vb, lam_w, g, nbatch, seq_len, lam_init):
    n, dw = q.shape
    heads = dw // LANES
    tq = _tile(seq_len, 256)
    assert tq % CHUNK == 0
    nq = seq_len // tq
    kern = functools.partial(_diff_prompt_kernel, tq=tq, lam_init=lam_init)
    vec = lambda a: _resident(a.shape)
    return pl.pallas_call(
        kern,
        grid=(nbatch, heads, nq),
        in_specs=[pl.BlockSpec((tq, LANES), lambda b, h, i: (b * nq + i, h)),
                  pl.BlockSpec((seq_len, LANES), lambda b, h, i: (b, h)),
                  pl.BlockSpec((seq_len, LANES), lambda b, h, i: (b, h)),
                  vec(lam_w[0]), vec(lam_w[1]), vec(lam_w[2]), vec(lam_w[3]), vec(g)],
        out_specs=pl.BlockSpec((tq, LANES), lambda b, h, i: (b * nq + i, h)),
        out_shape=jax.ShapeDtypeStruct((n, dw), BF16),
        scratch_shapes=[pltpu.VMEM((tq, 1), F32), pltpu.VMEM((tq, 1), F32), pltpu.VMEM((tq, LANES), F32),
                        pltpu.VMEM((tq, 1), F32), pltpu.VMEM((tq, 1), F32), pltpu.VMEM((tq, LANES), F32)],
        compiler_params=_cparams(3),
        name="diff_attn_prompt",
    )(q, kb, vb, *lam_w, g)


def _diff_sample_kernel(q_ref, kc_ref, vc_ref, kn_ref, vn_ref, lq1_ref, lk1_ref, lq2_ref, lk2_ref, g_ref, o_ref,
                        *, past, lam_init):
    tq = q_ref.shape[0]
    q1, q2 = _split_maps(q_ref[...])
    kc = kc_ref[...].astype(BF16)
    vc = vc_ref[...].astype(BF16)
    kn = kn_ref[...]
    vn = vn_ref[...]
    mask_n = _chunk_mask(past, past, tq, tq)
    outs = []
    for qm in (q1, q2):
        sc = _dot_t(qm, kc)
        sn = jnp.where(mask_n, _dot_t(qm, kn), NEG_INF)
        m = jnp.maximum(jnp.max(sc, axis=-1, keepdims=True), jnp.max(sn, axis=-1, keepdims=True))
        pc = jnp.exp(sc - m)
        pn = jnp.exp(sn - m)
        l = jnp.sum(pc, axis=-1, keepdims=True) + jnp.sum(pn, axis=-1, keepdims=True)
        outs.append((_dot(pc.astype(BF16), vc) + _dot(pn.astype(BF16), vn)) / l)
    lam = _diff_lambda(lq1_ref, lk1_ref, lq2_ref, lk2_ref, lam_init)
    o_ref[...] = _subln(outs[0] - lam * outs[1], g_ref[...], lam_init)


def _diff_sample(q, kb, vb, cache_k, cache_v, lam_w, g, nbatch, seq_len, lam_init):
    n, dw = q.shape
    heads = dw // LANES
    past = cache_k.shape[1]
    assert cache_k.shape[0] == nbatch and (past // CHUNK) * CHUNK == past
    kc = cache_k.reshape(nbatch * past, dw)
    vc = cache_v.reshape(nbatch * past, dw)
    kern = functools.partial(_diff_sample_kernel, past=past, lam_init=lam_init)
    vec = lambda a: _resident(a.shape)
    blk = lambda rows: pl.BlockSpec((rows, LANES), lambda b, h: (b, h))
    return pl.pallas_call(
        kern,
        grid=(nbatch, heads),
        in_specs=[blk(seq_len), blk(past), blk(past), blk(seq_len), blk(seq_len),
                  vec(lam_w[0]), vec(lam_w[1]), vec(lam_w[2]), vec(lam_w[3]), vec(g)],
        out_specs=blk(seq_len),
        out_shape=jax.ShapeDtypeStruct((n, dw), BF16),
        compiler_params=_cparams(2),
        name="diff_attn_sample",
    )(q, kc, vc, kb, vb, *lam_w, g)


def _out_proj_kernel(*refs, n_lhs, final):
    lhs = refs[:n_lhs]
    w_ref, x_ref, g_ref = refs[n_lhs:n_lhs + 3]
    outs = refs[n_lhs + 3:]
    acc = x_ref[...]
    off = 0
    for a in lhs:
        kdim = a.shape[1]
        acc = acc + _dot(a[...], w_ref[off:off + kdim, :])
        off += kdim
    if final:
        outs[0][...] = _rms(acc, g_ref[...])
    else:
        outs[0][...] = acc
        outs[1][...] = _rms(acc, g_ref[...]).astype(BF16)


def _out_proj(lhs, w, x, g, final=False, tm_pref=512):
    n, d = x.shape
    tm = _tile(n, tm_pref)
    row = lambda i: (i, 0)
    kern = functools.partial(_out_proj_kernel, n_lhs=len(lhs), final=final)
    in_specs = [pl.BlockSpec((tm, a.shape[1]), row) for a in lhs]
    in_specs += [_resident(w.shape), pl.BlockSpec((tm, d), row), _resident(g.shape)]
    if final:
        out_specs = [pl.BlockSpec((tm, d), row)]
        out_shape = [jax.ShapeDtypeStruct((n, d), F32)]
    else:
        out_specs = [pl.BlockSpec((tm, d), row), pl.BlockSpec((tm, d), row)]
        out_shape = [jax.ShapeDtypeStruct((n, d), F32), jax.ShapeDtypeStruct((n, d), BF16)]
    return pl.pallas_call(
        kern, grid=(n // tm,), in_specs=in_specs, out_specs=out_specs, out_shape=out_shape,
        compiler_params=_cparams(1), name="out_proj",
    )(*lhs, w, x, g)


def _ffn_up_kernel(hn_ref, wv_ref, wg_ref, cw_ref, cb_ref, st_ref, act_ref, stout_ref, ext_ref,
                   *, seg, tiles_per_seq):
    i = pl.program_id(1)
    hn = hn_ref[...]
    tm = hn.shape[0]
    val = _dot(hn, wv_ref[...])
    gate = _dot(hn, wg_ref[...])
    cw = cw_ref[...]
    cb = cb_ref[...]
    for s in range(tm // seg):
        g0 = gate[s * seg:(s + 1) * seg]
        if tiles_per_seq > 1:
            first = (i % tiles_per_seq) == 0

            @pl.when(first)
            def _():
                ext_ref[0:SUBLANES] = st_ref[0]

            @pl.when(jnp.logical_not(first))
            def _():
                ext_ref[0:SUBLANES] = ext_ref[seg:seg + SUBLANES]
        else:
            ext_ref[0:SUBLANES] = st_ref[s]
        ext_ref[SUBLANES:SUBLANES + seg] = g0
        g1 = ext_ref[SUBLANES - 1:SUBLANES - 1 + seg]
        g2 = ext_ref[SUBLANES - 2:SUBLANES - 2 + seg]
        c = cb + cw[0:1] * g2 + cw[1:2] * g1 + cw[2:3] * g0
        act_ref[s * seg:(s + 1) * seg] = (jax.nn.silu(c) * val[s * seg:(s + 1) * seg]).astype(BF16)
        stout_ref[s] = g0[seg - SUBLANES:seg]


def _ffn_up(hn, w_in, conv_w, conv_b, conv_state, seq_len):
    n, d = hn.shape
    f = w_in.shape[1] // 2
    nseq = n // seq_len
    tf = _tile(f, 512)
    tm = _tile(n, 1024)
    nf = f // tf
    if tm >= seq_len:
        seg, tps, spt = seq_len, 1, tm // seq_len
        st_idx = lambda j, i: (i, 0, j)
    else:
        seg, tps, spt = tm, seq_len // tm, 1
        st_idx = lambda j, i: (i // tps, 0, j)
    assert seg % SUBLANES == 0 and seg >= SUBLANES
    st = jnp.pad(conv_state, ((0, 0), (SUBLANES - (CONV_W - 1), 0), (0, 0)))
    cw = jnp.pad(conv_w, ((0, SUBLANES - CONV_W), (0, 0)))
    cb = conv_b.reshape(1, f)
    kern = functools.partial(_ffn_up_kernel, seg=seg, tiles_per_seq=tps)
    act, st_out = pl.pallas_call(
        kern,
        grid=(nf, n // tm),
        in_specs=[pl.BlockSpec((tm, d), lambda j, i: (i, 0)),
                  pl.BlockSpec((d, tf), lambda j, i: (0, j)),
                  pl.BlockSpec((d, tf), lambda j, i: (0, nf + j)),
                  pl.BlockSpec((SUBLANES, tf), lambda j, i: (0, j)),
                  pl.BlockSpec((1, tf), lambda j, i: (0, j)),
                  pl.BlockSpec((spt, SUBLANES, tf), st_idx)],
        out_specs=[pl.BlockSpec((tm, tf), lambda j, i: (i, j)),
                   pl.BlockSpec((spt, SUBLANES, tf), st_idx)],
        out_shape=[jax.ShapeDtypeStruct((n, f), BF16), jax.ShapeDtypeStruct((nseq, SUBLANES, f), F32)],
        scratch_shapes=[pltpu.VMEM((seg + SUBLANES, tf), F32)],
        compiler_params=_cparams(2),
        name="ffn_up",
    )(hn, w_in, w_in, cw, cb, st)
    return act, st_out[:, SUBLANES - (CONV_W - 1):, :]


def _rope_pair(x, cos, sin):
    return x * cos + pltpu.roll(x, MLA_ROPE, 1) * sin


def _odd_in_kernel(hn_ref, win_ref, gq_ref, gkv_ref, wuq_ref, cos_ref, sin_ref, *rest,
                   qr, kvr, heads, qscale, expand):
    if expand:
        wkv_ref, q_ref, ckv_ref, ckvb_ref, kpe_ref, kpeb_ref, kn_ref, v_ref = rest
    else:
        q_ref, ckv_ref, ckvb_ref, kpe_ref, kpeb_ref = rest
    hn = hn_ref[...]
    cos = cos_ref[...]
    sin = sin_ref[...]
    cq = _rms(_dot(hn, win_ref[:, 0:qr]), gq_ref[...]).astype(BF16)
    ckv = _rms(_dot(hn, win_ref[:, qr:qr + kvr]), gkv_ref[...])
    kpe = _rope_pair(_dot(hn, win_ref[:, qr + kvr:qr + kvr + LANES]), cos, sin)
    ckv_ref[...] = ckv
    ckvb = ckv.astype(BF16)
    ckvb_ref[...] = ckvb
    kpe_ref[...] = kpe[:, 0:MLA_ROPE]
    kpeb_ref[...] = kpe.astype(BF16)
    hw = MXU_DIM
    for h in range(heads):
        qh = _dot(cq, wuq_ref[:, h * hw:(h + 1) * hw]) * qscale
        q_ref[:, h * hw:h * hw + LANES] = qh[:, 0:LANES].astype(BF16)
        q_ref[:, h * hw + LANES:(h + 1) * hw] = _rope_pair(qh[:, LANES:hw], cos, sin).astype(BF16)
    if expand:
        nk = heads * MLA_NOPE
        kn_ref[...] = _dot(ckvb, wkv_ref[:, 0:nk]).astype(BF16)
        v_ref[...] = _dot(ckvb, wkv_ref[:, nk:]).astype(BF16)


def _odd_in(hn, win, gq, gkv, wuq, cos, sin, wkv, seq_len, heads):
    n, d = hn.shape
    qr, kvr = gq.shape[1], gkv.shape[1]
    tm = _tile(n, 256)
    row = lambda i: (i, 0)
    if tm <= seq_len:
        tps = seq_len // tm
        pos = lambda i: (i % tps, 0)
    else:
        cos = jnp.tile(cos, (tm // seq_len, 1))
        sin = jnp.tile(sin, (tm // seq_len, 1))
        pos = lambda i: (0, 0)
    expand = wkv is not None
    kern = functools.partial(_odd_in_kernel, qr=qr, kvr=kvr, heads=heads,
                             qscale=(MLA_NOPE + MLA_ROPE) ** -0.5, expand=expand)
    in_specs = [pl.BlockSpec((tm, d), row), _resident(win.shape), _resident(gq.shape), _resident(gkv.shape),
                _resident(wuq.shape), pl.BlockSpec((tm, LANES), pos), pl.BlockSpec((tm, LANES), pos)]
    args = [hn, win, gq, gkv, wuq, cos, sin]
    out_specs = [pl.BlockSpec((tm, heads * MXU_DIM), row), pl.BlockSpec((tm, kvr), row),
                 pl.BlockSpec((tm, kvr), row), pl.BlockSpec((tm, MLA_ROPE), row), pl.BlockSpec((tm, LANES), row)]
    out_shape = [jax.ShapeDtypeStruct((n, heads * MXU_DIM), BF16), jax.ShapeDtypeStruct((n, kvr), F32),
                 jax.ShapeDtypeStruct((n, kvr), BF16), jax.ShapeDtypeStruct((n, MLA_ROPE), F32),
                 jax.ShapeDtypeStruct((n, LANES), BF16)]
    if expand:
        in_specs.append(_resident(wkv.shape))
        args.append(wkv)
        out_specs += [pl.BlockSpec((tm, heads * MLA_NOPE), row), pl.BlockSpec((tm, heads * MLA_V), row)]
        out_shape += [jax.ShapeDtypeStruct((n, heads * MLA_NOPE), BF16),
                      jax.ShapeDtypeStruct((n, heads * MLA_V), BF16)]
    return pl.pallas_call(
        kern, grid=(n // tm,), in_specs=in_specs, out_specs=out_specs, out_shape=out_shape,
        compiler_params=_cparams(1), name="odd_in",
    )(*args)


def _mla_prompt_kernel(q_ref, kn_ref, kpe_ref, v_ref, o_ref, m_ref, l_ref, a_ref, *, tq):
    i = pl.program_id(2)
    q = q_ref[...]

    def keys(kt):
        sl = pl.ds(pl.multiple_of(kt * tq, tq), tq)
        return jnp.concatenate([kn_ref[sl, :], kpe_ref[sl, :]], axis=1), v_ref[sl, :]

    kd, vd = keys(i)
    s = jnp.where(_chunk_mask(0, 0, tq, tq), _dot_t(q, kd), NEG_INF)
    m = jnp.max(s, axis=-1, keepdims=True)
    p = jnp.exp(s - m)
    m_ref[...] = m
    l_ref[...] = jnp.sum(p, axis=-1, keepdims=True)
    a_ref[...] = _dot(p.astype(BF16), vd)

    def body(kt, _):
        kb, vb = keys(kt)
        s = _dot_t(q, kb)
        m_prev = m_ref[...]
        m = jnp.maximum(m_prev, jnp.max(s, axis=-1, keepdims=True))
        alpha = jnp.exp(m_prev - m)
        p = jnp.exp(s - m)
        m_ref[...] = m
        l_ref[...] = alpha * l_ref[...] + jnp.sum(p, axis=-1, keepdims=True)
        a_ref[...] = alpha * a_ref[...] + _dot(p.astype(BF16), vb)
        return 0

    lax.fori_loop(0, i, body, 0)
    o_ref[...] = (a_ref[...] / l_ref[...]).astype(BF16)


def _mla_prompt(q, kn, kpeb, v, nbatch, seq_len, heads):
    n = q.shape[0]
    tq = _tile(seq_len, 256)
    assert tq % CHUNK == 0
    nq = seq_len // tq
    kern = functools.partial(_mla_prompt_kernel, tq=tq)
    return pl.pallas_call(
        kern,
        grid=(nbatch, heads, nq),
        in_specs=[pl.BlockSpec((tq, MXU_DIM), lambda b, h, i: (b * nq + i, h)),
                  pl.BlockSpec((seq_len, MLA_NOPE), lambda b, h, i: (b, h)),
                  pl.BlockSpec((seq_len, LANES), lambda b, h, i: (b, 0)),
                  pl.BlockSpec((seq_len, MLA_V), lambda b, h, i: (b, h))],
        out_specs=pl.BlockSpec((tq, MLA_V), lambda b, h, i: (b * nq + i, h)),
        out_shape=jax.ShapeDtypeStruct((n, heads * MLA_V), BF16),
        scratch_shapes=[pltpu.VMEM((tq, 1), F32), pltpu.VMEM((tq, 1), F32), pltpu.VMEM((tq, MLA_V), F32)],
        compiler_params=_cparams(3),
        name="mla_attn_prompt",
    )(q, kn, kpeb, v)


def _mla_sample_kernel(q_ref, cc_ref, pc_ref, cn_ref, pn_ref, wk_ref, wv_ref, o_ref, ql_ref, qp_ref,
                       *, heads, past):
    tq = q_ref.shape[0]
    hw = MXU_DIM
    for h in range(heads):
        qn = q_ref[:, h * hw:h * hw + MLA_NOPE]
        ql_ref[h * tq:(h + 1) * tq, :] = _dot(qn, wk_ref[h]).astype(BF16)
        qp_ref[h * tq:(h + 1) * tq, :] = q_ref[:, h * hw + MLA_NOPE:(h + 1) * hw]
    ql = ql_ref[...]
    qp = qp_ref[...]
    cc = cc_ref[...].astype(BF16)
    pc = pc_ref[...].astype(BF16)
    cn = cn_ref[...]
    mask_n = jnp.concatenate([_chunk_mask(past, past, tq, tq)] * heads, axis=0)
    sc = _dot_t(ql, cc) + _dot_t(qp[:, 0:MLA_ROPE], pc)
    sn = jnp.where(mask_n, _dot_t(ql, cn) + _dot_t(qp, pn_ref[...]), NEG_INF)
    m = jnp.maximum(jnp.max(sc, axis=-1, keepdims=True), jnp.max(sn, axis=-1, keepdims=True))
    ec = jnp.exp(sc - m)
    en = jnp.exp(sn - m)
    l = jnp.sum(ec, axis=-1, keepdims=True) + jnp.sum(en, axis=-1, keepdims=True)
    ol = ((_dot(ec.astype(BF16), cc) + _dot(en.astype(BF16), cn)) / l).astype(BF16)
    for h in range(heads):
        o_ref[:, h * MLA_V:(h + 1) * MLA_V] = _dot(ol[h * tq:(h + 1) * tq], wv_ref[h]).astype(BF16)


def _mla_sample(q, ckvb, kpeb, cache_ckv, cache_kpe, wk_t, wv, nbatch, seq_len, heads):
    n = q.shape[0]
    past, kvr = cache_ckv.shape[1], cache_ckv.shape[2]
    assert (past // CHUNK) * CHUNK == past
    cc = cache_ckv.reshape(nbatch * past, kvr)
    pc = cache_kpe.reshape(nbatch * past, MLA_ROPE)
    kern = functools.partial(_mla_sample_kernel, heads=heads, past=past)
    row = lambda b: (b, 0)
    return pl.pallas_call(
        kern,
        grid=(nbatch,),
        in_specs=[pl.BlockSpec((seq_len, heads * MXU_DIM), row),
                  pl.BlockSpec((past, kvr), row), pl.BlockSpec((past, MLA_ROPE), row),
                  pl.BlockSpec((seq_len, kvr), row), pl.BlockSpec((seq_len, LANES), row),
                  _resident(wk_t.shape), _resident(wv.shape)],
        out_specs=pl.BlockSpec((seq_len, heads * MLA_V), row),
        out_shape=jax.ShapeDtypeStruct((n, heads * MLA_V), BF16),
        scratch_shapes=[pltpu.VMEM((heads * seq_len, kvr), BF16), pltpu.VMEM((heads * seq_len, LANES), BF16)],
        compiler_params=_cparams(1),
        name="mla_attn_sample",
    )(q, cc, pc, ckvb, kpeb, wk_t, wv)


def _rope_tables(pos):
    half = MLA_ROPE // 2
    inv = ROPE_BASE ** (-jnp.arange(half, dtype=F32) / half)
    ang = pos.astype(F32)[:, None] * inv[None, :]
    cos, sin = jnp.cos(ang), jnp.sin(ang)
    zero = jnp.zeros_like(cos)
    return (jnp.concatenate([cos, cos, zero, zero], axis=1),
            jnp.concatenate([-sin, sin, zero, zero], axis=1))


def _swap_halves(w):
    half = MLA_ROPE // 2
    return jnp.concatenate([w[..., half:], w[..., :half]], axis=-1)


def _prepare_weights(p):
    w = {}
    heads = p['mla_w_uq'].shape[2]
    w['heads'] = heads
    w['norm_mix'] = p['norm_mix'][:, None, :]
    w['norm_ffn'] = p['norm_ffn'][:, None, :]
    w['norm_final'] = p['norm_final'][None, :]
    w['w_in_even'] = p['w_in_even'][0].astype(BF16)
    w['w_out_even'] = p['w_out_even'][0].astype(BF16)
    w['s5'] = dict(a_re=p['s5_a_re'][0], a_im=p['s5_a_im'][0], b_re=p['s5_b_re'][0], b_im=p['s5_b_im'][0],
                   c_re=p['s5_c_re'][0], c_im=p['s5_c_im'][0], d=p['s5_d'][0], log_dt=p['s5_log_dt'][0],
                   w_glu=p['s5_w_glu'][0], b_glu=p['s5_b_glu'][0])
    w['lam'] = [p[k][0][None, :] for k in ('diff_lambda_q1', 'diff_lambda_k1', 'diff_lambda_q2', 'diff_lambda_k2')]
    w['subln'] = p['diff_subln'][0][None, :]
    wi = p['w_in_odd'][0]
    qr = p['mla_q_norm'].shape[1]
    kvr = p['mla_kv_norm'].shape[1]
    wpe = wi[:, qr + kvr:]
    w['w_in_odd'] = jnp.concatenate([wi[:, :qr + kvr], wpe, _swap_halves(wpe)], axis=1).astype(BF16)
    w['gq'] = p['mla_q_norm'][0][None, :]
    w['gkv'] = p['mla_kv_norm'][0][None, :]
    wuq = p['mla_w_uq'][0]
    wuq = jnp.concatenate([wuq, _swap_halves(wuq[..., MLA_NOPE:])], axis=-1)
    w['w_uq'] = wuq.reshape(qr, heads * MXU_DIM).astype(BF16)
    wukv = p['mla_w_ukv'][0]
    w['w_kv'] = jnp.concatenate([wukv[..., :MLA_NOPE].reshape(kvr, heads * MLA_NOPE),
                                 wukv[..., MLA_NOPE:].reshape(kvr, heads * MLA_V)], axis=1).astype(BF16)
    w['w_uk_t'] = jnp.transpose(wukv[..., :MLA_NOPE], (1, 2, 0)).astype(BF16)
    w['w_uv'] = jnp.transpose(wukv[..., MLA_NOPE:], (1, 0, 2)).astype(BF16)
    w['w_out_odd'] = p['w_out_odd'][0].astype(BF16)
    w['ffn_w_in'] = p['ffn_w_in'].astype(BF16)
    w['ffn_w_down'] = p['ffn_w_down'].astype(BF16)
    w['ffn_conv_w'] = p['ffn_conv_w']
    w['ffn_conv_b'] = p['ffn_conv_b']
    return w


def _trunk(x3, s5_re0, s5_im0, k_past, v_past, ckv_past, kpe_past, conv0, pos0, w):
    nb, t, d = x3.shape
    n = nb * t
    x = x3.reshape(n, d)
    heads = w['heads']
    s5w = w['s5']['d'].size
    dw = (w['w_in_even'].shape[1] - s5w) // 3
    dheads = dw // LANES

    lam_init = 0.8 - 0.6 * math.exp(-0.3 * 0)
    u, q, k, v, kb, vb = _even_in(x, w['norm_mix'][0], w['w_in_even'], s5w, dw)
    y_s5, ht_re, ht_im = _s5(u, s5_re0, s5_im0, t, w['s5'])
    if k_past is None:
        o = _diff_prompt(q, kb, vb, w['lam'], w['subln'], nb, t, lam_init)
    else:
        o = _diff_sample(q, kb, vb, k_past, v_past, w['lam'], w['subln'], nb, t, lam_init)
    x, hn = _out_proj([y_s5, o], w['w_out_even'], x, w['norm_ffn'][0])
    act, conv_a = _ffn_up(hn, w['ffn_w_in'][0], w['ffn_conv_w'][0], w['ffn_conv_b'][0], conv0[0], t)
    x, hn = _out_proj([act], w['ffn_w_down'][0], x, w['norm_mix'][1], tm_pref=256)

    cos, sin = _rope_tables(pos0 + jnp.arange(t, dtype=jnp.int32))
    if ckv_past is None:
        qm, ckv, ckvb, kpe, kpeb, kn, vm = _odd_in(hn, w['w_in_odd'], w['gq'], w['gkv'], w['w_uq'], cos, sin,
                                                   w['w_kv'], t, heads)
        om = _mla_prompt(qm, kn, kpeb, vm, nb, t, heads)
    else:
        qm, ckv, ckvb, kpe, kpeb = _odd_in(hn, w['w_in_odd'], w['gq'], w['gkv'], w['w_uq'], cos, sin,
                                           None, t, heads)
        om = _mla_sample(qm, ckvb, kpeb, ckv_past, kpe_past, w['w_uk_t'], w['w_uv'], nb, t, heads)
    x, hn = _out_proj([om], w['w_out_odd'], x, w['norm_ffn'][1])
    act, conv_b = _ffn_up(hn, w['ffn_w_in'][1], w['ffn_conv_w'][1], w['ffn_conv_b'][1], conv0[1], t)
    (y,) = _out_proj([act], w['ffn_w_down'][1], x, w['norm_final'], final=True, tm_pref=256)

    groups, n_state = w['s5']['a_re'].shape
    return (y.reshape(nb, t, d), ht_re[None], ht_im[None],
            k.reshape(1, nb, t, dheads, LANES), v.reshape(1, nb, t, dheads, LANES),
            ckv.reshape(1, nb, t, -1), kpe.reshape(1, nb, t, MLA_ROPE), jnp.stack([conv_a, conv_b]))


def kernel(x_prompt, x_sample, state_s5_re, state_s5_im, cache_diff_k, cache_diff_v, cache_mla_ckv, cache_mla_kpe, state_ffn_conv, norm_mix, norm_ffn, norm_final, w_in_even, w_out_even, s5_a_re, s5_a_im, s5_b_re, s5_b_im, s5_c_re, s5_c_im, s5_d, s5_log_dt, s5_w_glu, s5_b_glu, diff_lambda_q1, diff_lambda_k1, diff_lambda_q2, diff_lambda_k2, diff_subln, w_in_odd, mla_q_norm, mla_kv_norm, mla_w_uq, mla_w_ukv, w_out_odd, ffn_w_in, ffn_conv_w, ffn_conv_b, ffn_w_down):
    w = _prepare_weights(dict(
        norm_mix=norm_mix, norm_ffn=norm_ffn, norm_final=norm_final, w_in_even=w_in_even, w_out_even=w_out_even,
        s5_a_re=s5_a_re, s5_a_im=s5_a_im, s5_b_re=s5_b_re, s5_b_im=s5_b_im, s5_c_re=s5_c_re, s5_c_im=s5_c_im,
        s5_d=s5_d, s5_log_dt=s5_log_dt, s5_w_glu=s5_w_glu, s5_b_glu=s5_b_glu,
        diff_lambda_q1=diff_lambda_q1, diff_lambda_k1=diff_lambda_k1, diff_lambda_q2=diff_lambda_q2,
        diff_lambda_k2=diff_lambda_k2, diff_subln=diff_subln, w_in_odd=w_in_odd, mla_q_norm=mla_q_norm,
        mla_kv_norm=mla_kv_norm, mla_w_uq=mla_w_uq, mla_w_ukv=mla_w_ukv, w_out_odd=w_out_odd,
        ffn_w_in=ffn_w_in, ffn_conv_w=ffn_conv_w, ffn_conv_b=ffn_conv_b, ffn_w_down=ffn_w_down))
    nb_p = x_prompt.shape[0]
    groups, n_state = s5_a_re.shape[1:]
    d_ff = ffn_conv_b.shape[1]
    depth = ffn_conv_b.shape[0]
    s5_zero = jnp.zeros((nb_p, groups, n_state), F32)
    conv_zero = jnp.zeros((depth, nb_p, CONV_W - 1, d_ff), F32)
    (y_p, re_p, im_p, k_p, v_p, ckv_p, kpe_p, conv_p) = _trunk(
        x_prompt, s5_zero, s5_zero, None, None, None, None, conv_zero, 0, w)
    past = cache_diff_k.shape[2]
    (y_s, re_s, im_s, k_s, v_s, ckv_s, kpe_s, conv_s) = _trunk(
        x_sample, state_s5_re[0], state_s5_im[0], cache_diff_k[0], cache_diff_v[0], cache_mla_ckv[0],
        cache_mla_kpe[0], state_ffn_conv, past, w)
    return (y_p, y_s, re_p, im_p, re_s, im_s, k_p, v_p, k_s, v_s, ckv_p, kpe_p, ckv_s, kpe_s, conv_p, conv_s)
```

```python
import functools
import math

import jax
import jax.numpy as jnp
from jax import lax
from jax.experimental import pallas as pl
from jax.experimental.pallas import tpu as pltpu

F32 = jnp.float32
BF16 = jnp.bfloat16

CHUNK = 64
EPS = 1e-6
NEG_INF = -1e30
ROPE_BASE = 10000.0
S5_GROUP = 16
DIFF_DK = 64
MLA_NOPE = 128
MLA_ROPE = 64
MLA_V = 128
CONV_W = 3
LOG2E = 1.4426950408889634

LANES = 128
SUBLANES = 8
MXU_DIM = 256
VMEM_LIMIT_BYTES = 56 * 1024 * 1024

FFN_WEIGHT_BYTES = 24 * 1024 * 1024
FLASH_ROWS = 256
S5_L = 16
S5_GB = MXU_DIM // S5_GROUP


def _cparams(n_axes):
    return pltpu.CompilerParams(
        dimension_semantics=("arbitrary",) * n_axes,
        vmem_limit_bytes=VMEM_LIMIT_BYTES)


def _resident(shape):
    nd = len(shape)
    return pl.BlockSpec(shape, lambda *_: (0,) * nd, pipeline_mode=pl.Buffered(1))


def _tile(n, pref):
    t = min(n, pref)
    while n % t:
        t //= 2
    return t


def _dot(a, b):
    return jnp.dot(a, b, preferred_element_type=F32)


def _dot_t(a, b):
    return lax.dot_general(a, b, (((1,), (1,)), ((), ())), preferred_element_type=F32)


def _rms(x, g):
    ms = jnp.mean(x * x, axis=-1, keepdims=True)
    return x * lax.rsqrt(ms + EPS) * g


def _even_in_kernel(x_ref, g_ref, w_ref, u_ref, q_ref, k_ref, v_ref, kb_ref, vb_ref, *, s5w, dw, qscale):
    xn = _rms(x_ref[...], g_ref[...]).astype(BF16)
    u_ref[...] = _dot(xn, w_ref[:, 0:s5w])
    q_ref[...] = (_dot(xn, w_ref[:, s5w:s5w + dw]) * qscale).astype(BF16)
    k = _dot(xn, w_ref[:, s5w + dw:s5w + 2 * dw])
    k_ref[...] = k
    kb_ref[...] = k.astype(BF16)
    v = _dot(xn, w_ref[:, s5w + 2 * dw:s5w + 3 * dw])
    v_ref[...] = v
    vb_ref[...] = v.astype(BF16)


def _even_in(x, g, w, s5w, dw):
    n, d = x.shape
    tm = _tile(n, 512)
    row = lambda i: (i, 0)
    kern = functools.partial(_even_in_kernel, s5w=s5w, dw=dw, qscale=DIFF_DK ** -0.5 * LOG2E)
    return pl.pallas_call(
        kern,
        grid=(n // tm,),
        in_specs=[pl.BlockSpec((tm, d), row), _resident(g.shape), _resident(w.shape)],
        out_specs=[pl.BlockSpec((tm, s5w), row), pl.BlockSpec((tm, dw), row), pl.BlockSpec((tm, dw), row),
                   pl.BlockSpec((tm, dw), row), pl.BlockSpec((tm, dw), row), pl.BlockSpec((tm, dw), row)],
        out_shape=[jax.ShapeDtypeStruct((n, s5w), F32), jax.ShapeDtypeStruct((n, dw), BF16),
                   jax.ShapeDtypeStruct((n, dw), F32), jax.ShapeDtypeStruct((n, dw), F32),
                   jax.ShapeDtypeStruct((n, dw), BF16), jax.ShapeDtypeStruct((n, dw), BF16)],
        compiler_params=_cparams(1),
        name="even_in",
    )(x, g, w)


def _s5_kernel(u2_ref, h0_ref, lre_ref, lim_ref, dt_ref, bre_ref, bim_ref, cbd_ref, d_ref, wglu_ref, bglu_ref,
               y2_ref, ht_ref,
               a_ref, a16_ref, bbd_ref, xs_ref, xe_ref, hs_ref, carry_ref, ys_ref,
               *, rows, width, cps, ngb):
    t = pl.program_id(0)
    nblk = ngb * 16
    half = 8
    gw = MXU_DIM

    @pl.when(t == 0)
    def _prepare():
        carry_ref[...] = jnp.zeros_like(carry_ref)
        for gb in range(ngb):
            for k in range(half):
                lre = jnp.minimum(lre_ref[gb * half + k], -1e-4)
                lim = lim_ref[gb * half + k]
                dt = jnp.exp(dt_ref[gb * half + k])
                mag = jnp.exp(lre * dt)
                are = mag * jnp.cos(lim * dt)
                aim = mag * jnp.sin(lim * dt)
                den = lre * lre + lim * lim
                cre = ((are - 1.0) * lre + aim * lim) / den
                cim = (aim * lre - (are - 1.0) * lim) / den
                ire, iim = gb * 16 + k, gb * 16 + half + k
                a_ref[ire] = are
                a_ref[iim] = aim
                pre, pim = are, aim
                for _ in range(4):
                    pre, pim = pre * pre - pim * pim, 2.0 * pre * pim
                a16_ref[ire] = pre
                a16_ref[iim] = pim
                br = bre_ref[gb, :, k * LANES:(k + 1) * LANES]
                bi = bim_ref[gb, :, k * LANES:(k + 1) * LANES]
                bbd_ref[gb, :, k * LANES:(k + 1) * LANES] = (cre * br - cim * bi).astype(BF16)
                bbd_ref[gb, :, (half + k) * LANES:(half + k + 1) * LANES] = (cre * bi + cim * br).astype(BF16)

    for gb in range(ngb):
        lhs = jnp.concatenate(
            [u2_ref[:, l * width + gb * gw:l * width + (gb + 1) * gw] for l in range(S5_L)], axis=0).astype(BF16)
        x = _dot(lhs, bbd_ref[gb])
        for j in range(16):
            xs_ref[gb * 16 + j] = x[:, j * LANES:(j + 1) * LANES]

    ncplx = ngb * half

    def _blocks(c):
        gb = c // half
        k = c % half
        return gb * 16 + k, gb * 16 + half + k

    def pass_a(c, _):
        ire, iim = _blocks(c)
        are, aim = a_ref[ire], a_ref[iim]
        hre = xs_ref[ire, 0:rows, :]
        him = xs_ref[iim, 0:rows, :]
        for l in range(1, S5_L):
            xre = xs_ref[ire, l * rows:(l + 1) * rows, :]
            xim = xs_ref[iim, l * rows:(l + 1) * rows, :]
            hre, him = are * hre - aim * him + xre, are * him + aim * hre + xim
        xe_ref[ire] = hre
        xe_ref[iim] = him
        return 0

    lax.fori_loop(0, ncplx, pass_a, 0)

    base_row = t * rows
    for grp in range(ncplx // 8):
        ire0 = (grp // (half // 8)) * 16 + (grp % (half // 8)) * 8
        iim0 = ire0 + half
        a16re = a16_ref[ire0:ire0 + 8]
        a16im = a16_ref[iim0:iim0 + 8]

        def scan_body(r, carry, ire0=ire0, iim0=iim0, a16re=a16re, a16im=a16im):
            cre, cim = carry
            gr = base_row + r
            seq = gr // cps - (base_row // cps)
            is_start = (gr % cps) == 0
            cre = jnp.where(is_start, h0_ref[seq, ire0:ire0 + 8], cre)
            cim = jnp.where(is_start, h0_ref[seq, iim0:iim0 + 8], cim)
            hs_ref[ire0:ire0 + 8, pl.ds(r, 1), :] = cre
            hs_ref[iim0:iim0 + 8, pl.ds(r, 1), :] = cim
            xre = xe_ref[ire0:ire0 + 8, pl.ds(r, 1), :]
            xim = xe_ref[iim0:iim0 + 8, pl.ds(r, 1), :]
            nre = a16re * cre - a16im * cim + xre
            nim = a16re * cim + a16im * cre + xim

            @pl.when((gr % cps) == cps - 1)
            def _():
                ht_ref[seq, ire0:ire0 + 8] = nre
                ht_ref[seq, iim0:iim0 + 8] = nim

            return nre, nim

        cre, cim = lax.fori_loop(0, rows, scan_body, (carry_ref[ire0:ire0 + 8], carry_ref[iim0:iim0 + 8]))
        carry_ref[ire0:ire0 + 8] = cre
        carry_ref[iim0:iim0 + 8] = cim

    def pass_b(c, _):
        ire, iim = _blocks(c)
        are, aim = a_ref[ire], a_ref[iim]
        hre = hs_ref[ire]
        him = hs_ref[iim]
        for l in range(S5_L):
            xre = xs_ref[ire, l * rows:(l + 1) * rows, :]
            xim = xs_ref[iim, l * rows:(l + 1) * rows, :]
            hre, him = are * hre - aim * him + xre, are * him + aim * hre + xim
            xs_ref[ire, l * rows:(l + 1) * rows, :] = hre
            xs_ref[iim, l * rows:(l + 1) * rows, :] = him
        return 0

    lax.fori_loop(0, ncplx, pass_b, 0)

    for gb in range(ngb):
        h = jnp.concatenate([xs_ref[gb * 16 + j].astype(BF16) for j in range(16)], axis=1)
        ys_ref[:, gb * gw:(gb + 1) * gw] = _dot(h, cbd_ref[gb])
    ust = jnp.concatenate([u2_ref[:, l * width:(l + 1) * width] for l in range(S5_L)], axis=0)
    y = ys_ref[...] + d_ref[...] * ust
    z = jax.nn.gelu(y)
    gate = jax.nn.sigmoid(_dot(z.astype(BF16), wglu_ref[...]) + bglu_ref[...])
    out = (z * gate).astype(BF16)
    for l in range(S5_L):
        y2_ref[:, l * width:(l + 1) * width] = out[l * rows:(l + 1) * rows]


def _s5_layout(p, ngb, n_state):
    return p.reshape(ngb * (S5_GB * n_state // LANES), 1, LANES)


def _s5_state_to_blocks(s, ngb):
    b = s.shape[0]
    return s.reshape(b, ngb, 8, 1, LANES)


def _s5(u, h0_re, h0_im, seq_len, w):
    n, width = u.shape
    groups, n_state = w['a_re'].shape
    assert n_state * S5_GB == 8 * LANES and width == groups * S5_GROUP
    ngb = groups // S5_GB
    nseq = n // seq_len
    cps = seq_len // S5_L
    nrow = n // S5_L
    rows = _tile(nrow, 32)
    assert rows % SUBLANES == 0 and (cps % rows == 0 or rows % cps == 0)
    spt = max(1, rows // cps)
    tps = max(1, cps // rows)
    nblk = ngb * 16
    sw = S5_GB * n_state

    u2 = u.reshape(nrow, S5_L * width)
    h0 = jnp.concatenate([_s5_state_to_blocks(h0_re, ngb), _s5_state_to_blocks(h0_im, ngb)], axis=2)
    h0 = h0.reshape(nseq, nblk, 1, LANES)
    lre = _s5_layout(w['a_re'], ngb, n_state)
    lim = _s5_layout(w['a_im'], ngb, n_state)
    dt = _s5_layout(jnp.broadcast_to(w['log_dt'][:, None], (groups, n_state)), ngb, n_state)
    eye = jnp.eye(S5_GB, dtype=F32)

    def bdiag_b(b):
        bb = b.reshape(ngb, S5_GB, n_state, S5_GROUP)
        return jnp.einsum('agnp,gh->agphn', bb, eye).reshape(ngb, S5_GB * S5_GROUP, sw)

    def bdiag_c(c):
        cc = c.reshape(ngb, S5_GB, S5_GROUP, n_state)
        return jnp.einsum('agpn,gh->ahngp', cc, eye).reshape(ngb, sw, S5_GB * S5_GROUP)

    bre = bdiag_b(w['b_re'])
    bim = bdiag_b(w['b_im'])
    cbd = jnp.concatenate([bdiag_c(w['c_re']), -bdiag_c(w['c_im'])], axis=1).astype(BF16)
    d = w['d'].reshape(1, width)
    wglu = w['w_glu'].astype(BF16)
    bglu = w['b_glu'].reshape(1, width)

    seq_idx = (lambda t: (t // tps, 0, 0, 0)) if tps > 1 else (lambda t: (t, 0, 0, 0))
    kern = functools.partial(_s5_kernel, rows=rows, width=width, cps=cps, ngb=ngb)
    y2, ht = pl.pallas_call(
        kern,
        grid=(nrow // rows,),
        in_specs=[pl.BlockSpec((rows, S5_L * width), lambda t: (t, 0)),
                  pl.BlockSpec((spt, nblk, 1, LANES), seq_idx),
                  _resident(lre.shape), _resident(lim.shape), _resident(dt.shape),
                  _resident(bre.shape), _resident(bim.shape), _resident(cbd.shape),
                  _resident(d.shape), _resident(wglu.shape), _resident(bglu.shape)],
        out_specs=[pl.BlockSpec((rows, S5_L * width), lambda t: (t, 0)),
                   pl.BlockSpec((spt, nblk, 1, LANES), seq_idx)],
        out_shape=[jax.ShapeDtypeStruct((nrow, S5_L * width), BF16),
                   jax.ShapeDtypeStruct((nseq, nblk, 1, LANES), F32)],
        scratch_shapes=[pltpu.VMEM((nblk, 1, LANES), F32),
                        pltpu.VMEM((nblk, 1, LANES), F32),
                        pltpu.VMEM((ngb, MXU_DIM, 2 * sw), BF16),
                        pltpu.VMEM((nblk, S5_L * rows, LANES), F32),
                        pltpu.VMEM((nblk, rows, LANES), F32),
                        pltpu.VMEM((nblk, rows, LANES), F32),
                        pltpu.VMEM((nblk, 1, LANES), F32),
                        pltpu.VMEM((S5_L * rows, width), F32)],
        compiler_params=_cparams(1),
        name="s5_mix",
    )(u2, h0, lre, lim, dt, bre, bim, cbd, d, wglu, bglu)
    ht = ht.reshape(nseq, ngb, 2, groups // ngb, n_state)
    ht_re = ht[:, :, 0].reshape(nseq, groups, n_state)
    ht_im = ht[:, :, 1].reshape(nseq, groups, n_state)
    return y2.reshape(n, width), ht_re, ht_im


def _diff_lambda(lq1_ref, lk1_ref, lq2_ref, lk2_ref, lam_init):
    s1 = jnp.sum(lq1_ref[...] * lk1_ref[...], axis=-1, keepdims=True)
    s2 = jnp.sum(lq2_ref[...] * lk2_ref[...], axis=-1, keepdims=True)
    return jnp.exp(s1) - jnp.exp(s2) + lam_init


def _split_maps(q):
    lane = lax.broadcasted_iota(jnp.int32, q.shape, 1)
    zero = jnp.zeros_like(q)
    return jnp.where(lane < DIFF_DK, q, zero), jnp.where(lane >= DIFF_DK, q, zero)


def _chunk_mask(q0, k0, tq, tk):
    qc = (q0 + lax.broadcasted_iota(jnp.int32, (tq, tk), 0)) // CHUNK
    kc = (k0 + lax.broadcasted_iota(jnp.int32, (tq, tk), 1)) // CHUNK
    return kc <= qc


def _subln(o, g, lam_init):
    return (_rms(o, g) * (1.0 - lam_init)).astype(BF16)


def _flash_update(q, k, vx, m_ref, a_ref, mask):
    rows = q.shape[0]
    sub = min(rows, FLASH_ROWS)
    for r0 in range(0, rows, sub):
        rs = slice(r0, r0 + sub)
        s = _dot_t(q[rs], k)
        if mask is not None:
            s = jnp.where(mask[rs], s, NEG_INF)
        m_prev = m_ref[rs]
        m_new = jnp.maximum(m_prev, jnp.max(s, axis=-1, keepdims=True))
        alpha = jnp.exp2(m_prev - m_new)
        p = jnp.exp2(s - m_new)
        m_ref[rs] = m_new
        a_ref[rs] = alpha * a_ref[rs] + _dot(p.astype(BF16), vx)


def _unit_column(rows):
    return (lax.broadcasted_iota(jnp.int32, (rows, LANES), 1) == 0).astype(BF16)


def _diff_prompt_kernel(q_ref, k_ref, v_ref, lq1_ref, lk1_ref, lq2_ref, lk2_ref, g_ref, o_ref,
                        vx_ref, m_ref, a_ref, *, tq, lam_init):
    i = pl.program_id(2)

    @pl.when(i == 0)
    def _():
        vx_ref[:, 0:LANES] = v_ref[...]
        vx_ref[:, LANES:2 * LANES] = _unit_column(v_ref.shape[0])

    qs = _split_maps(q_ref[...])
    m_ref[...] = jnp.full(m_ref.shape, NEG_INF, F32)
    a_ref[...] = jnp.zeros(a_ref.shape, F32)

    def step(kt, mask):
        sl = pl.ds(pl.multiple_of(kt * tq, tq), tq)
        kb = k_ref[sl, :]
        vb = vx_ref[sl, :]
        for j in range(2):
            _flash_update(qs[j], kb, vb, m_ref.at[j], a_ref.at[j], mask)

    def body(kt, carry):
        step(kt, None)
        return carry

    lax.fori_loop(0, i, body, 0)
    step(i, _chunk_mask(0, 0, tq, tq))

    lam = _diff_lambda(lq1_ref, lk1_ref, lq2_ref, lk2_ref, lam_init)
    a1 = a_ref[0]
    a2 = a_ref[1]
    o = a1[:, 0:LANES] / a1[:, LANES:LANES + 1] - lam * (a2[:, 0:LANES] / a2[:, LANES:LANES + 1])
    o_ref[...] = _subln(o, g_ref[...], lam_init)


def _diff_prompt(q, kb, vb, lam_w, g, nbatch, seq_len, lam_init):
    n, dw = q.shape
    heads = dw // LANES
    tq = _tile(seq_len, 512)
    assert tq % CHUNK == 0
    nq = seq_len // tq
    kern = functools.partial(_diff_prompt_kernel, tq=tq, lam_init=lam_init)
    vec = lambda a: _resident(a.shape)
    return pl.pallas_call(
        kern,
        grid=(nbatch, heads, nq),
        in_specs=[pl.BlockSpec((tq, LANES), lambda b, h, i: (b * nq + i, h)),
                  pl.BlockSpec((seq_len, LANES), lambda b, h, i: (b, h)),
                  pl.BlockSpec((seq_len, LANES), lambda b, h, i: (b, h)),
                  vec(lam_w[0]), vec(lam_w[1]), vec(lam_w[2]), vec(lam_w[3]), vec(g)],
        out_specs=pl.BlockSpec((tq, LANES), lambda b, h, i: (b * nq + i, h)),
        out_shape=jax.ShapeDtypeStruct((n, dw), BF16),
        scratch_shapes=[pltpu.VMEM((seq_len, 2 * LANES), BF16),
                        pltpu.VMEM((2, tq, 1), F32), pltpu.VMEM((2, tq, 2 * LANES), F32)],
        compiler_params=_cparams(3),
        name="diff_attn_prompt",
    )(q, kb, vb, *lam_w, g)


def _diff_sample_kernel(q_ref, kc_ref, vc_ref, kn_ref, vn_ref, lq1_ref, lk1_ref, lq2_ref, lk2_ref, g_ref, o_ref,
                        *, past, lam_init):
    tq = q_ref.shape[0]
    q1, q2 = _split_maps(q_ref[...])
    kc = kc_ref[...].astype(BF16)
    vc = vc_ref[...].astype(BF16)
    kn = kn_ref[...]
    vn = vn_ref[...]
    mask_n = _chunk_mask(past, past, tq, tq)
    outs = []
    for qm in (q1, q2):
        sc = _dot_t(qm, kc)
        sn = jnp.where(mask_n, _dot_t(qm, kn), NEG_INF)
        m = jnp.maximum(jnp.max(sc, axis=-1, keepdims=True), jnp.max(sn, axis=-1, keepdims=True))
        pc = jnp.exp2(sc - m)
        pn = jnp.exp2(sn - m)
        l = jnp.sum(pc, axis=-1, keepdims=True) + jnp.sum(pn, axis=-1, keepdims=True)
        outs.append((_dot(pc.astype(BF16), vc) + _dot(pn.astype(BF16), vn)) / l)
    lam = _diff_lambda(lq1_ref, lk1_ref, lq2_ref, lk2_ref, lam_init)
    o_ref[...] = _subln(outs[0] - lam * outs[1], g_ref[...], lam_init)


def _diff_sample(q, kb, vb, cache_k, cache_v, lam_w, g, nbatch, seq_len, lam_init):
    n, dw = q.shape
    heads = dw // LANES
    past = cache_k.shape[1]
    assert cache_k.shape[0] == nbatch and (past // CHUNK) * CHUNK == past
    kc = cache_k.reshape(nbatch * past, dw)
    vc = cache_v.reshape(nbatch * past, dw)
    kern = functools.partial(_diff_sample_kernel, past=past, lam_init=lam_init)
    vec = lambda a: _resident(a.shape)
    blk = lambda rows: pl.BlockSpec((rows, LANES), lambda b, h: (b, h))
    return pl.pallas_call(
        kern,
        grid=(nbatch, heads),
        in_specs=[blk(seq_len), blk(past), blk(past), blk(seq_len), blk(seq_len),
                  vec(lam_w[0]), vec(lam_w[1]), vec(lam_w[2]), vec(lam_w[3]), vec(g)],
        out_specs=blk(seq_len),
        out_shape=jax.ShapeDtypeStruct((n, dw), BF16),
        compiler_params=_cparams(2),
        name="diff_attn_sample",
    )(q, kc, vc, kb, vb, *lam_w, g)


def _out_proj_kernel(*refs, n_lhs, final):
    lhs = refs[:n_lhs]
    w_ref, x_ref, g_ref = refs[n_lhs:n_lhs + 3]
    outs = refs[n_lhs + 3:]
    acc = x_ref[...]
    off = 0
    for a in lhs:
        kdim = a.shape[1]
        acc = acc + _dot(a[...], w_ref[off:off + kdim, :])
        off += kdim
    if final:
        outs[0][...] = _rms(acc, g_ref[...])
    else:
        outs[0][...] = acc
        outs[1][...] = _rms(acc, g_ref[...]).astype(BF16)


def _out_proj(lhs, w, x, g, final=False, tm_pref=512):
    n, d = x.shape
    tm = _tile(n, tm_pref)
    row = lambda i: (i, 0)
    kern = functools.partial(_out_proj_kernel, n_lhs=len(lhs), final=final)
    in_specs = [pl.BlockSpec((tm, a.shape[1]), row) for a in lhs]
    in_specs += [_resident(w.shape), pl.BlockSpec((tm, d), row), _resident(g.shape)]
    if final:
        out_specs = [pl.BlockSpec((tm, d), row)]
        out_shape = [jax.ShapeDtypeStruct((n, d), F32)]
    else:
        out_specs = [pl.BlockSpec((tm, d), row), pl.BlockSpec((tm, d), row)]
        out_shape = [jax.ShapeDtypeStruct((n, d), F32), jax.ShapeDtypeStruct((n, d), BF16)]
    return pl.pallas_call(
        kern, grid=(n // tm,), in_specs=in_specs, out_specs=out_specs, out_shape=out_shape,
        compiler_params=_cparams(1), name="out_proj",
    )(*lhs, w, x, g)


def _ffn_up_kernel(hn_ref, wv_ref, wg_ref, cw_ref, cb_ref, st_ref, act_ref, stout_ref, prev_ref,
                   *, seg, tiles_per_seq):
    i = pl.program_id(1)
    hn = hn_ref[...]
    tm = hn.shape[0]
    tf = act_ref.shape[1]
    if tiles_per_seq > 1:
        @pl.when((i % tiles_per_seq) == 0)
        def _():
            prev_ref[...] = st_ref[0]

    cwid = min(tf, MXU_DIM)
    for c0 in range(0, tf, cwid):
        cs = slice(c0, c0 + cwid)
        val = _dot(hn, wv_ref[:, cs])
        gate = _dot(hn, wg_ref[:, cs])
        cw = cw_ref[:, cs]
        cb = cb_ref[:, cs]
        for s in range(tm // seg):
            g0 = gate[s * seg:(s + 1) * seg]
            prev = st_ref[s, :, cs] if tiles_per_seq == 1 else prev_ref[:, cs]
            ext = jnp.concatenate([prev, g0], axis=0)
            g1 = pltpu.roll(ext, 1, 0)[SUBLANES:]
            g2 = pltpu.roll(ext, 2, 0)[SUBLANES:]
            c = cb + cw[0:1] * g2 + cw[1:2] * g1 + cw[2:3] * g0
            act_ref[s * seg:(s + 1) * seg, cs] = (jax.nn.silu(c) * val[s * seg:(s + 1) * seg]).astype(BF16)
            last = g0[seg - SUBLANES:seg]
            stout_ref[s, :, cs] = last
            if tiles_per_seq > 1:
                prev_ref[:, cs] = last


def _ffn_up(hn, w_in, conv_w, conv_b, conv_state, seq_len):
    n, d = hn.shape
    f = w_in.shape[1] // 2
    nseq = n // seq_len
    tf = max(t for t in range(MXU_DIM, f + 1, MXU_DIM) if f % t == 0 and 4 * d * t <= FFN_WEIGHT_BYTES)
    tm = _tile(n, 512)
    nf = f // tf
    wspec = lambda idx: pl.BlockSpec((d, tf), idx, pipeline_mode=pl.Buffered(1))
    if tm >= seq_len:
        seg, tps, spt = seq_len, 1, tm // seq_len
        st_idx = lambda j, i: (i, 0, j)
    else:
        seg, tps, spt = tm, seq_len // tm, 1
        st_idx = lambda j, i: (i // tps, 0, j)
    assert seg % SUBLANES == 0 and seg >= SUBLANES
    st = jnp.pad(conv_state, ((0, 0), (SUBLANES - (CONV_W - 1), 0), (0, 0)))
    cw = jnp.pad(conv_w, ((0, SUBLANES - CONV_W), (0, 0)))
    cb = conv_b.reshape(1, f)
    kern = functools.partial(_ffn_up_kernel, seg=seg, tiles_per_seq=tps)
    act, st_out = pl.pallas_call(
        kern,
        grid=(nf, n // tm),
        in_specs=[pl.BlockSpec((tm, d), lambda j, i: (i, 0)),
                  wspec(lambda j, i: (0, j)), wspec(lambda j, i: (0, nf + j)),
                  pl.BlockSpec((SUBLANES, tf), lambda j, i: (0, j)),
                  pl.BlockSpec((1, tf), lambda j, i: (0, j)),
                  pl.BlockSpec((spt, SUBLANES, tf), st_idx)],
        out_specs=[pl.BlockSpec((tm, tf), lambda j, i: (i, j)),
                   pl.BlockSpec((spt, SUBLANES, tf), st_idx)],
        out_shape=[jax.ShapeDtypeStruct((n, f), BF16), jax.ShapeDtypeStruct((nseq, SUBLANES, f), F32)],
        scratch_shapes=[pltpu.VMEM((SUBLANES, tf), F32)],
        compiler_params=_cparams(2),
        name="ffn_up",
    )(hn, w_in, w_in, cw, cb, st)
    return act, st_out[:, SUBLANES - (CONV_W - 1):, :]


def _rope_pair(x, cos, sin):
    return x * cos + pltpu.roll(x, MLA_ROPE, 1) * sin


def _odd_in_kernel(hn_ref, win_ref, gq_ref, gkv_ref, wuq_ref, cos_ref, sin_ref, *rest,
                   qr, kvr, heads, qscale, expand):
    if expand:
        wkv_ref, q_ref, ckv_ref, ckvb_ref, kpe_ref, kpeb_ref, kn_ref, v_ref = rest
    else:
        q_ref, ckv_ref, ckvb_ref, kpe_ref, kpeb_ref = rest
    hn = hn_ref[...]
    cos = cos_ref[...]
    sin = sin_ref[...]
    cq = _rms(_dot(hn, win_ref[:, 0:qr]), gq_ref[...]).astype(BF16)
    ckv = _rms(_dot(hn, win_ref[:, qr:qr + kvr]), gkv_ref[...])
    kpe = _rope_pair(_dot(hn, win_ref[:, qr + kvr:qr + kvr + LANES]), cos, sin)
    ckv_ref[...] = ckv
    ckvb = ckv.astype(BF16)
    ckvb_ref[...] = ckvb
    kpe_ref[...] = kpe[:, 0:MLA_ROPE]
    kpeb_ref[...] = kpe.astype(BF16)
    hw = MXU_DIM
    for h in range(heads):
        qh = _dot(cq, wuq_ref[:, h * hw:(h + 1) * hw]) * qscale
        q_ref[:, h * hw:h * hw + LANES] = qh[:, 0:LANES].astype(BF16)
        q_ref[:, h * hw + LANES:(h + 1) * hw] = _rope_pair(qh[:, LANES:hw], cos, sin).astype(BF16)
    if expand:
        nk = heads * MLA_NOPE
        kn_ref[...] = _dot(ckvb, wkv_ref[:, 0:nk]).astype(BF16)
        v_ref[...] = _dot(ckvb, wkv_ref[:, nk:]).astype(BF16)


def _odd_in(hn, win, gq, gkv, wuq, cos, sin, wkv, seq_len, heads):
    n, d = hn.shape
    qr, kvr = gq.shape[1], gkv.shape[1]
    tm = _tile(n, 256)
    row = lambda i: (i, 0)
    if tm <= seq_len:
        tps = seq_len // tm
        pos = lambda i: (i % tps, 0)
    else:
        cos = jnp.tile(cos, (tm // seq_len, 1))
        sin = jnp.tile(sin, (tm // seq_len, 1))
        pos = lambda i: (0, 0)
    expand = wkv is not None
    kern = functools.partial(_odd_in_kernel, qr=qr, kvr=kvr, heads=heads,
                             qscale=(MLA_NOPE + MLA_ROPE) ** -0.5 * LOG2E, expand=expand)
    in_specs = [pl.BlockSpec((tm, d), row), _resident(win.shape), _resident(gq.shape), _resident(gkv.shape),
                _resident(wuq.shape), pl.BlockSpec((tm, LANES), pos), pl.BlockSpec((tm, LANES), pos)]
    args = [hn, win, gq, gkv, wuq, cos, sin]
    out_specs = [pl.BlockSpec((tm, heads * MXU_DIM), row), pl.BlockSpec((tm, kvr), row),
                 pl.BlockSpec((tm, kvr), row), pl.BlockSpec((tm, MLA_ROPE), row), pl.BlockSpec((tm, LANES), row)]
    out_shape = [jax.ShapeDtypeStruct((n, heads * MXU_DIM), BF16), jax.ShapeDtypeStruct((n, kvr), F32),
                 jax.ShapeDtypeStruct((n, kvr), BF16), jax.ShapeDtypeStruct((n, MLA_ROPE), F32),
                 jax.ShapeDtypeStruct((n, LANES), BF16)]
    if expand:
        in_specs.append(_resident(wkv.shape))
        args.append(wkv)
        out_specs += [pl.BlockSpec((tm, heads * MLA_NOPE), row), pl.BlockSpec((tm, heads * MLA_V), row)]
        out_shape += [jax.ShapeDtypeStruct((n, heads * MLA_NOPE), BF16),
                      jax.ShapeDtypeStruct((n, heads * MLA_V), BF16)]
    return pl.pallas_call(
        kern, grid=(n // tm,), in_specs=in_specs, out_specs=out_specs, out_shape=out_shape,
        compiler_params=_cparams(1), name="odd_in",
    )(*args)


def _mla_prompt_kernel(q_ref, kn_ref, kpe_ref, v_ref, o_ref, kx_ref, vx_ref, m_ref, a_ref, *, tq, hp):
    i = pl.program_id(2)
    hw = MXU_DIM

    @pl.when(i == 0)
    def _():
        for j in range(hp):
            kx_ref[j, :, 0:MLA_NOPE] = kn_ref[:, j * MLA_NOPE:(j + 1) * MLA_NOPE]
            kx_ref[j, :, MLA_NOPE:hw] = kpe_ref[...]
            vx_ref[j, :, 0:MLA_V] = v_ref[:, j * MLA_V:(j + 1) * MLA_V]
            vx_ref[j, :, MLA_V:MLA_V + LANES] = _unit_column(v_ref.shape[0])

    m_ref[...] = jnp.full(m_ref.shape, NEG_INF, F32)
    a_ref[...] = jnp.zeros(a_ref.shape, F32)

    def step(kt, mask):
        sl = pl.ds(pl.multiple_of(kt * tq, tq), tq)
        for j in range(hp):
            _flash_update(q_ref[:, j * hw:(j + 1) * hw], kx_ref[j, sl, :], vx_ref[j, sl, :],
                          m_ref.at[j], a_ref.at[j], mask)

    def body(kt, carry):
        step(kt, None)
        return carry

    lax.fori_loop(0, i, body, 0)
    step(i, _chunk_mask(0, 0, tq, tq))
    for j in range(hp):
        a = a_ref[j]
        o_ref[:, j * MLA_V:(j + 1) * MLA_V] = (a[:, 0:MLA_V] / a[:, MLA_V:MLA_V + 1]).astype(BF16)


def _mla_prompt(q, kn, kpeb, v, nbatch, seq_len, heads):
    n = q.shape[0]
    tq = _tile(seq_len, 512)
    hp = 4
    assert tq % CHUNK == 0 and heads % hp == 0
    nq = seq_len // tq
    kern = functools.partial(_mla_prompt_kernel, tq=tq, hp=hp)
    return pl.pallas_call(
        kern,
        grid=(nbatch, heads // hp, nq),
        in_specs=[pl.BlockSpec((tq, hp * MXU_DIM), lambda b, h, i: (b * nq + i, h)),
                  pl.BlockSpec((seq_len, hp * MLA_NOPE), lambda b, h, i: (b, h)),
                  pl.BlockSpec((seq_len, LANES), lambda b, h, i: (b, 0)),
                  pl.BlockSpec((seq_len, hp * MLA_V), lambda b, h, i: (b, h))],
        out_specs=pl.BlockSpec((tq, hp * MLA_V), lambda b, h, i: (b * nq + i, h)),
        out_shape=jax.ShapeDtypeStruct((n, heads * MLA_V), BF16),
        scratch_shapes=[pltpu.VMEM((hp, seq_len, MXU_DIM), BF16),
                        pltpu.VMEM((hp, seq_len, MLA_V + LANES), BF16),
                        pltpu.VMEM((hp, tq, 1), F32), pltpu.VMEM((hp, tq, MLA_V + LANES), F32)],
        compiler_params=_cparams(3),
        name="mla_attn_prompt",
    )(q, kn, kpeb, v)


def _mla_sample_kernel(q_ref, cc_ref, pc_ref, cn_ref, pn_ref, wk_ref, wv_ref, o_ref, ql_ref, qp_ref,
                       *, heads, past):
    tq = q_ref.shape[0]
    hw = MXU_DIM
    for h in range(heads):
        qn = q_ref[:, h * hw:h * hw + MLA_NOPE]
        ql_ref[h * tq:(h + 1) * tq, :] = _dot(qn, wk_ref[h]).astype(BF16)
        qp_ref[h * tq:(h + 1) * tq, :] = q_ref[:, h * hw + MLA_NOPE:(h + 1) * hw]
    ql = ql_ref[...]
    qp = qp_ref[...]
    cc = cc_ref[...].astype(BF16)
    pc = pc_ref[...].astype(BF16)
    cn = cn_ref[...]
    mask_n = jnp.concatenate([_chunk_mask(past, past, tq, tq)] * heads, axis=0)
    sc = _dot_t(ql, cc) + _dot_t(qp[:, 0:MLA_ROPE], pc)
    sn = jnp.where(mask_n, _dot_t(ql, cn) + _dot_t(qp, pn_ref[...]), NEG_INF)
    m = jnp.maximum(jnp.max(sc, axis=-1, keepdims=True), jnp.max(sn, axis=-1, keepdims=True))
    ec = jnp.exp2(sc - m)
    en = jnp.exp2(sn - m)
    l = jnp.sum(ec, axis=-1, keepdims=True) + jnp.sum(en, axis=-1, keepdims=True)
    ol = ((_dot(ec.astype(BF16), cc) + _dot(en.astype(BF16), cn)) / l).astype(BF16)
    for h in range(heads):
        o_ref[:, h * MLA_V:(h + 1) * MLA_V] = _dot(ol[h * tq:(h + 1) * tq], wv_ref[h]).astype(BF16)


def _mla_sample(q, ckvb, kpeb, cache_ckv, cache_kpe, wk_t, wv, nbatch, seq_len, heads):
    n = q.shape[0]
    past, kvr = cache_ckv.shape[1], cache_ckv.shape[2]
    assert (past // CHUNK) * CHUNK == past
    cc = cache_ckv.reshape(nbatch * past, kvr)
    pc = cache_kpe.reshape(nbatch * past, MLA_ROPE)
    kern = functools.partial(_mla_sample_kernel, heads=heads, past=past)
    row = lambda b: (b, 0)
    return pl.pallas_call(
        kern,
        grid=(nbatch,),
        in_specs=[pl.BlockSpec((seq_len, heads * MXU_DIM), row),
                  pl.BlockSpec((past, kvr), row), pl.BlockSpec((past, MLA_ROPE), row),
                  pl.BlockSpec((seq_len, kvr), row), pl.BlockSpec((seq_len, LANES), row),
                  _resident(wk_t.shape), _resident(wv.shape)],
        out_specs=pl.BlockSpec((seq_len, heads * MLA_V), row),
        out_shape=jax.ShapeDtypeStruct((n, heads * MLA_V), BF16),
        scratch_shapes=[pltpu.VMEM((heads * seq_len, kvr), BF16), pltpu.VMEM((heads * seq_len, LANES), BF16)],
        compiler_params=_cparams(1),
        name="mla_attn_sample",
    )(q, cc, pc, ckvb, kpeb, wk_t, wv)


def _rope_tables(pos):
    half = MLA_ROPE // 2
    inv = ROPE_BASE ** (-jnp.arange(half, dtype=F32) / half)
    ang = pos.astype(F32)[:, None] * inv[None, :]
    cos, sin = jnp.cos(ang), jnp.sin(ang)
    zero = jnp.zeros_like(cos)
    return (jnp.concatenate([cos, cos, zero, zero], axis=1),
            jnp.concatenate([-sin, sin, zero, zero], axis=1))


def _swap_halves(w):
    half = MLA_ROPE // 2
    return jnp.concatenate([w[..., half:], w[..., :half]], axis=-1)


def _prepare_weights(p):
    w = {}
    heads = p['mla_w_uq'].shape[2]
    w['heads'] = heads
    w['norm_mix'] = p['norm_mix'][:, None, :]
    w['norm_ffn'] = p['norm_ffn'][:, None, :]
    w['norm_final'] = p['norm_final'][None, :]
    w['w_in_even'] = p['w_in_even'][0].astype(BF16)
    w['w_out_even'] = p['w_out_even'][0].astype(BF16)
    w['s5'] = dict(a_re=p['s5_a_re'][0], a_im=p['s5_a_im'][0], b_re=p['s5_b_re'][0], b_im=p['s5_b_im'][0],
                   c_re=p['s5_c_re'][0], c_im=p['s5_c_im'][0], d=p['s5_d'][0], log_dt=p['s5_log_dt'][0],
                   w_glu=p['s5_w_glu'][0], b_glu=p['s5_b_glu'][0])
    w['lam'] = [p[k][0][None, :] for k in ('diff_lambda_q1', 'diff_lambda_k1', 'diff_lambda_q2', 'diff_lambda_k2')]
    w['subln'] = p['diff_subln'][0][None, :]
    wi = p['w_in_odd'][0]
    qr = p['mla_q_norm'].shape[1]
    kvr = p['mla_kv_norm'].shape[1]
    wpe = wi[:, qr + kvr:]
    w['w_in_odd'] = jnp.concatenate([wi[:, :qr + kvr], wpe, _swap_halves(wpe)], axis=1).astype(BF16)
    w['gq'] = p['mla_q_norm'][0][None, :]
    w['gkv'] = p['mla_kv_norm'][0][None, :]
    wuq = p['mla_w_uq'][0]
    wuq = jnp.concatenate([wuq, _swap_halves(wuq[..., MLA_NOPE:])], axis=-1)
    w['w_uq'] = wuq.reshape(qr, heads * MXU_DIM).astype(BF16)
    wukv = p['mla_w_ukv'][0]
    w['w_kv'] = jnp.concatenate([wukv[..., :MLA_NOPE].reshape(kvr, heads * MLA_NOPE),
                                 wukv[..., MLA_NOPE:].reshape(kvr, heads * MLA_V)], axis=1).astype(BF16)
    w['w_uk_t'] = jnp.transpose(wukv[..., :MLA_NOPE], (1, 2, 0)).astype(BF16)
    w['w_uv'] = jnp.transpose(wukv[..., MLA_NOPE:], (1, 0, 2)).astype(BF16)
    w['w_out_odd'] = p['w_out_odd'][0].astype(BF16)
    w['ffn_w_in'] = p['ffn_w_in'].astype(BF16)
    w['ffn_w_down'] = p['ffn_w_down'].astype(BF16)
    w['ffn_conv_w'] = p['ffn_conv_w']
    w['ffn_conv_b'] = p['ffn_conv_b']
    return w


def _trunk(x3, s5_re0, s5_im0, k_past, v_past, ckv_past, kpe_past, conv0, pos0, w):
    nb, t, d = x3.shape
    n = nb * t
    x = x3.reshape(n, d)
    heads = w['heads']
    s5w = w['s5']['d'].size
    dw = (w['w_in_even'].shape[1] - s5w) // 3
    dheads = dw // LANES

    lam_init = 0.8 - 0.6 * math.exp(-0.3 * 0)
    u, q, k, v, kb, vb = _even_in(x, w['norm_mix'][0], w['w_in_even'], s5w, dw)
    y_s5, ht_re, ht_im = _s5(u, s5_re0, s5_im0, t, w['s5'])
    if k_past is None:
        o = _diff_prompt(q, kb, vb, w['lam'], w['subln'], nb, t, lam_init)
    else:
        o = _diff_sample(q, kb, vb, k_past, v_past, w['lam'], w['subln'], nb, t, lam_init)
    x, hn = _out_proj([y_s5, o], w['w_out_even'], x, w['norm_ffn'][0])
    act, conv_a = _ffn_up(hn, w['ffn_w_in'][0], w['ffn_conv_w'][0], w['ffn_conv_b'][0], conv0[0], t)
    x, hn = _out_proj([act], w['ffn_w_down'][0], x, w['norm_mix'][1], tm_pref=256)

    cos, sin = _rope_tables(pos0 + jnp.arange(t, dtype=jnp.int32))
    if ckv_past is None:
        qm, ckv, ckvb, kpe, kpeb, kn, vm = _odd_in(hn, w['w_in_odd'], w['gq'], w['gkv'], w['w_uq'], cos, sin,
                                                   w['w_kv'], t, heads)
        om = _mla_prompt(qm, kn, kpeb, vm, nb, t, heads)
    else:
        qm, ckv, ckvb, kpe, kpeb = _odd_in(hn, w['w_in_odd'], w['gq'], w['gkv'], w['w_uq'], cos, sin,
                                           None, t, heads)
        om = _mla_sample(qm, ckvb, kpeb, ckv_past, kpe_past, w['w_uk_t'], w['w_uv'], nb, t, heads)
    x, hn = _out_proj([om], w['w_out_odd'], x, w['norm_ffn'][1])
    act, conv_b = _ffn_up(hn, w['ffn_w_in'][1], w['ffn_conv_w'][1], w['ffn_conv_b'][1], conv0[1], t)
    (y,) = _out_proj([act], w['ffn_w_down'][1], x, w['norm_final'], final=True, tm_pref=256)

    groups, n_state = w['s5']['a_re'].shape
    return (y.reshape(nb, t, d), ht_re[None], ht_im[None],
            k.reshape(1, nb, t, dheads, LANES), v.reshape(1, nb, t, dheads, LANES),
            ckv.reshape(1, nb, t, -1), kpe.reshape(1, nb, t, MLA_ROPE), jnp.stack([conv_a, conv_b]))


def kernel(x_prompt, x_sample, state_s5_re, state_s5_im, cache_diff_k, cache_diff_v, cache_mla_ckv, cache_mla_kpe, state_ffn_conv, norm_mix, norm_ffn, norm_final, w_in_even, w_out_even, s5_a_re, s5_a_im, s5_b_re, s5_b_im, s5_c_re, s5_c_im, s5_d, s5_log_dt, s5_w_glu, s5_b_glu, diff_lambda_q1, diff_lambda_k1, diff_lambda_q2, diff_lambda_k2, diff_subln, w_in_odd, mla_q_norm, mla_kv_norm, mla_w_uq, mla_w_ukv, w_out_odd, ffn_w_in, ffn_conv_w, ffn_conv_b, ffn_w_down):
    w = _prepare_weights(dict(
        norm_mix=norm_mix, norm_ffn=norm_ffn, norm_final=norm_final, w_in_even=w_in_even, w_out_even=w_out_even,
        s5_a_re=s5_a_re, s5_a_im=s5_a_im, s5_b_re=s5_b_re, s5_b_im=s5_b_im, s5_c_re=s5_c_re, s5_c_im=s5_c_im,
        s5_d=s5_d, s5_log_dt=s5_log_dt, s5_w_glu=s5_w_glu, s5_b_glu=s5_b_glu,
        diff_lambda_q1=diff_lambda_q1, diff_lambda_k1=diff_lambda_k1, diff_lambda_q2=diff_lambda_q2,
        diff_lambda_k2=diff_lambda_k2, diff_subln=diff_subln, w_in_odd=w_in_odd, mla_q_norm=mla_q_norm,
        mla_kv_norm=mla_kv_norm, mla_w_uq=mla_w_uq, mla_w_ukv=mla_w_ukv, w_out_odd=w_out_odd,
        ffn_w_in=ffn_w_in, ffn_conv_w=ffn_conv_w, ffn_conv_b=ffn_conv_b, ffn_w_down=ffn_w_down))
    nb_p = x_prompt.shape[0]
    groups, n_state = s5_a_re.shape[1:]
    d_ff = ffn_conv_b.shape[1]
    depth = ffn_conv_b.shape[0]
    s5_zero = jnp.zeros((nb_p, groups, n_state), F32)
    conv_zero = jnp.zeros((depth, nb_p, CONV_W - 1, d_ff), F32)
    (y_p, re_p, im_p, k_p, v_p, ckv_p, kpe_p, conv_p) = _trunk(
        x_prompt, s5_zero, s5_zero, None, None, None, None, conv_zero, 0, w)
    past = cache_diff_k.shape[2]
    (y_s, re_s, im_s, k_s, v_s, ckv_s, kpe_s, conv_s) = _trunk(
        x_sample, state_s5_re[0], state_s5_im[0], cache_diff_k[0], cache_diff_v[0], cache_mla_ckv[0],
        cache_mla_kpe[0], state_ffn_conv, past, w)
    return (y_p, y_s, re_p, im_p, re_s, im_s, k_p, v_p, k_s, v_s, ckv_p, kpe_p, ckv_s, kpe_s, conv_p, conv_s)
```

```python
import functools
import math

import jax
import jax.numpy as jnp
from jax import lax
from jax.experimental import pallas as pl
from jax.experimental.pallas import tpu as pltpu

F32 = jnp.float32
BF16 = jnp.bfloat16

CHUNK = 64
EPS = 1e-6
NEG_INF = -1e30
ROPE_BASE = 10000.0
S5_GROUP = 16
DIFF_DK = 64
MLA_NOPE = 128
MLA_ROPE = 64
MLA_V = 128
CONV_W = 3
LOG2E = 1.4426950408889634

LANES = 128
SUBLANES = 8
MXU_DIM = 256
VMEM_LIMIT_BYTES = 56 * 1024 * 1024

FFN_WEIGHT_BYTES = 24 * 1024 * 1024
FLASH_COLS = 256
VX_ROWS = LANES + 16
S5_L = 16
S5_GB = MXU_DIM // S5_GROUP


def _cparams(n_axes):
    return pltpu.CompilerParams(
        dimension_semantics=("arbitrary",) * n_axes,
        vmem_limit_bytes=VMEM_LIMIT_BYTES)


def _resident(shape):
    nd = len(shape)
    return pl.BlockSpec(shape, lambda *_: (0,) * nd, pipeline_mode=pl.Buffered(1))


def _tile(n, pref):
    t = min(n, pref)
    while n % t:
        t //= 2
    return t


def _dot(a, b):
    return jnp.dot(a, b, preferred_element_type=F32)


def _dot_t(a, b):
    return lax.dot_general(a, b, (((1,), (1,)), ((), ())), preferred_element_type=F32)


def _rms(x, g):
    ms = jnp.mean(x * x, axis=-1, keepdims=True)
    return x * lax.rsqrt(ms + EPS) * g


def _even_in_kernel(x_ref, g_ref, w_ref, u_ref, q_ref, k_ref, v_ref, kb_ref, vb_ref, vbt_ref,
                    *, s5w, dw, qscale):
    xn = _rms(x_ref[...], g_ref[...]).astype(BF16)
    u_ref[...] = _dot(xn, w_ref[:, 0:s5w])
    q_ref[...] = (_dot(xn, w_ref[:, s5w:s5w + dw]) * qscale).astype(BF16)
    k = _dot(xn, w_ref[:, s5w + dw:s5w + 2 * dw])
    k_ref[...] = k
    kb_ref[...] = k.astype(BF16)
    v = _dot(xn, w_ref[:, s5w + 2 * dw:s5w + 3 * dw])
    v_ref[...] = v
    vb_ref[...] = v.astype(BF16)
    vbt_ref[...] = v.T.astype(BF16)


def _even_in(x, g, w, s5w, dw):
    n, d = x.shape
    tm = _tile(n, 512)
    row = lambda i: (i, 0)
    kern = functools.partial(_even_in_kernel, s5w=s5w, dw=dw, qscale=DIFF_DK ** -0.5 * LOG2E)
    return pl.pallas_call(
        kern,
        grid=(n // tm,),
        in_specs=[pl.BlockSpec((tm, d), row), _resident(g.shape), _resident(w.shape)],
        out_specs=[pl.BlockSpec((tm, s5w), row), pl.BlockSpec((tm, dw), row), pl.BlockSpec((tm, dw), row),
                   pl.BlockSpec((tm, dw), row), pl.BlockSpec((tm, dw), row), pl.BlockSpec((tm, dw), row),
                   pl.BlockSpec((dw, tm), lambda i: (0, i))],
        out_shape=[jax.ShapeDtypeStruct((n, s5w), F32), jax.ShapeDtypeStruct((n, dw), BF16),
                   jax.ShapeDtypeStruct((n, dw), F32), jax.ShapeDtypeStruct((n, dw), F32),
                   jax.ShapeDtypeStruct((n, dw), BF16), jax.ShapeDtypeStruct((n, dw), BF16),
                   jax.ShapeDtypeStruct((dw, n), BF16)],
        compiler_params=_cparams(1),
        name="even_in",
    )(x, g, w)


def _s5_kernel(u2_ref, h0_ref, lre_ref, lim_ref, dt_ref, bre_ref, bim_ref, cbd_ref, d_ref, wglu_ref, bglu_ref,
               y2_ref, ht_ref,
               a_ref, a16_ref, bbd_ref, xs_ref, xe_ref, hs_ref, carry_ref, ys_ref,
               *, rows, width, cps, ngb):
    t = pl.program_id(0)
    nblk = ngb * 16
    half = 8
    gw = MXU_DIM

    @pl.when(t == 0)
    def _prepare():
        carry_ref[...] = jnp.zeros_like(carry_ref)
        for gb in range(ngb):
            for k in range(half):
                lre = jnp.minimum(lre_ref[gb * half + k], -1e-4)
                lim = lim_ref[gb * half + k]
                dt = jnp.exp(dt_ref[gb * half + k])
                mag = jnp.exp(lre * dt)
                are = mag * jnp.cos(lim * dt)
                aim = mag * jnp.sin(lim * dt)
                den = lre * lre + lim * lim
                cre = ((are - 1.0) * lre + aim * lim) / den
                cim = (aim * lre - (are - 1.0) * lim) / den
                ire, iim = gb * 16 + k, gb * 16 + half + k
                a_ref[ire] = are
                a_ref[iim] = aim
                pre, pim = are, aim
                for _ in range(4):
                    pre, pim = pre * pre - pim * pim, 2.0 * pre * pim
                a16_ref[ire] = pre
                a16_ref[iim] = pim
                br = bre_ref[gb, :, k * LANES:(k + 1) * LANES]
                bi = bim_ref[gb, :, k * LANES:(k + 1) * LANES]
                bbd_ref[gb, :, k * LANES:(k + 1) * LANES] = (cre * br - cim * bi).astype(BF16)
                bbd_ref[gb, :, (half + k) * LANES:(half + k + 1) * LANES] = (cre * bi + cim * br).astype(BF16)

    for gb in range(ngb):
        lhs = jnp.concatenate(
            [u2_ref[:, l * width + gb * gw:l * width + (gb + 1) * gw] for l in range(S5_L)], axis=0).astype(BF16)
        x = _dot(lhs, bbd_ref[gb])
        for j in range(16):
            xs_ref[gb * 16 + j] = x[:, j * LANES:(j + 1) * LANES]

    ncplx = ngb * half

    def _blocks(c):
        gb = c // half
        k = c % half
        return gb * 16 + k, gb * 16 + half + k

    def step(l):
        return pl.ds(l * rows, rows)

    def pass_a(c, _):
        ire, iim = _blocks(c)
        are, aim = a_ref[ire], a_ref[iim]
        hre = xs_ref[ire, step(0), :]
        him = xs_ref[iim, step(0), :]
        for l in range(1, S5_L):
            xre = xs_ref[ire, step(l), :]
            xim = xs_ref[iim, step(l), :]
            hre, him = are * hre - aim * him + xre, are * him + aim * hre + xim
        xe_ref[ire] = hre
        xe_ref[iim] = him
        return 0

    lax.fori_loop(0, ncplx, pass_a, 0)

    base_row = t * rows
    for grp in range(ncplx // 8):
        ire0 = (grp // (half // 8)) * 16 + (grp % (half // 8)) * 8
        iim0 = ire0 + half
        a16re = a16_ref[ire0:ire0 + 8]
        a16im = a16_ref[iim0:iim0 + 8]

        def scan_body(r, carry, ire0=ire0, iim0=iim0, a16re=a16re, a16im=a16im):
            cre, cim = carry
            gr = base_row + r
            seq = gr // cps - (base_row // cps)
            is_start = (gr % cps) == 0
            cre = jnp.where(is_start, h0_ref[seq, ire0:ire0 + 8], cre)
            cim = jnp.where(is_start, h0_ref[seq, iim0:iim0 + 8], cim)
            hs_ref[ire0:ire0 + 8, pl.ds(r, 1), :] = cre
            hs_ref[iim0:iim0 + 8, pl.ds(r, 1), :] = cim
            xre = xe_ref[ire0:ire0 + 8, pl.ds(r, 1), :]
            xim = xe_ref[iim0:iim0 + 8, pl.ds(r, 1), :]
            nre = a16re * cre - a16im * cim + xre
            nim = a16re * cim + a16im * cre + xim

            @pl.when((gr % cps) == cps - 1)
            def _():
                ht_ref[seq, ire0:ire0 + 8] = nre
                ht_ref[seq, iim0:iim0 + 8] = nim

            return nre, nim

        cre, cim = lax.fori_loop(0, rows, scan_body, (carry_ref[ire0:ire0 + 8], carry_ref[iim0:iim0 + 8]))
        carry_ref[ire0:ire0 + 8] = cre
        carry_ref[iim0:iim0 + 8] = cim

    def pass_b(c, _):
        ire, iim = _blocks(c)
        are, aim = a_ref[ire], a_ref[iim]
        hre = hs_ref[ire]
        him = hs_ref[iim]
        for l in range(S5_L):
            xre = xs_ref[ire, step(l), :]
            xim = xs_ref[iim, step(l), :]
            hre, him = are * hre - aim * him + xre, are * him + aim * hre + xim
            xs_ref[ire, step(l), :] = hre
            xs_ref[iim, step(l), :] = him
        return 0

    lax.fori_loop(0, ncplx, pass_b, 0)

    for gb in range(ngb):
        h = jnp.concatenate([xs_ref[gb * 16 + j].astype(BF16) for j in range(16)], axis=1)
        ys_ref[:, gb * gw:(gb + 1) * gw] = _dot(h, cbd_ref[gb])
    ust = jnp.concatenate([u2_ref[:, l * width:(l + 1) * width] for l in range(S5_L)], axis=0)
    y = ys_ref[...] + d_ref[...] * ust
    z = jax.nn.gelu(y)
    gate = jax.nn.sigmoid(_dot(z.astype(BF16), wglu_ref[...]) + bglu_ref[...])
    out = (z * gate).astype(BF16)
    for l in range(S5_L):
        y2_ref[:, l * width:(l + 1) * width] = out[l * rows:(l + 1) * rows]


def _s5_layout(p, ngb, n_state):
    return p.reshape(ngb * (S5_GB * n_state // LANES), 1, LANES)


def _s5_state_to_blocks(s, ngb):
    b = s.shape[0]
    return s.reshape(b, ngb, 8, 1, LANES)


def _s5(u, h0_re, h0_im, seq_len, w):
    n, width = u.shape
    groups, n_state = w['a_re'].shape
    assert n_state * S5_GB == 8 * LANES and width == groups * S5_GROUP
    ngb = groups // S5_GB
    nseq = n // seq_len
    cps = seq_len // S5_L
    nrow = n // S5_L
    rows = _tile(nrow, 32)
    assert rows % SUBLANES == 0 and (cps % rows == 0 or rows % cps == 0)
    spt = max(1, rows // cps)
    tps = max(1, cps // rows)
    nblk = ngb * 16
    sw = S5_GB * n_state

    h0 = jnp.concatenate([_s5_state_to_blocks(h0_re, ngb), _s5_state_to_blocks(h0_im, ngb)], axis=2)
    h0 = h0.reshape(nseq, nblk, 1, LANES)
    lre = _s5_layout(w['a_re'], ngb, n_state)
    lim = _s5_layout(w['a_im'], ngb, n_state)
    dt = _s5_layout(jnp.broadcast_to(w['log_dt'][:, None], (groups, n_state)), ngb, n_state)
    eye = jnp.eye(S5_GB, dtype=F32)

    def bdiag_b(b):
        bb = b.reshape(ngb, S5_GB, n_state, S5_GROUP)
        return jnp.einsum('agnp,gh->agphn', bb, eye).reshape(ngb, S5_GB * S5_GROUP, sw)

    def bdiag_c(c):
        cc = c.reshape(ngb, S5_GB, S5_GROUP, n_state)
        return jnp.einsum('agpn,gh->ahngp', cc, eye).reshape(ngb, sw, S5_GB * S5_GROUP)

    bre = bdiag_b(w['b_re'])
    bim = bdiag_b(w['b_im'])
    cbd = jnp.concatenate([bdiag_c(w['c_re']), -bdiag_c(w['c_im'])], axis=1).astype(BF16)
    d = w['d'].reshape(1, width)
    wglu = w['w_glu'].astype(BF16)
    bglu = w['b_glu'].reshape(1, width)

    seq_idx = (lambda t: (t // tps, 0, 0, 0)) if tps > 1 else (lambda t: (t, 0, 0, 0))
    kern = functools.partial(_s5_kernel, rows=rows, width=width, cps=cps, ngb=ngb)
    u2 = u.reshape(nrow, S5_L * width)
    y2, ht = pl.pallas_call(
        kern,
        grid=(nrow // rows,),
        in_specs=[pl.BlockSpec((rows, S5_L * width), lambda t: (t, 0)),
                  pl.BlockSpec((spt, nblk, 1, LANES), seq_idx),
                  _resident(lre.shape), _resident(lim.shape), _resident(dt.shape),
                  _resident(bre.shape), _resident(bim.shape), _resident(cbd.shape),
                  _resident(d.shape), _resident(wglu.shape), _resident(bglu.shape)],
        out_specs=[pl.BlockSpec((rows, S5_L * width), lambda t: (t, 0)),
                   pl.BlockSpec((spt, nblk, 1, LANES), seq_idx)],
        out_shape=[jax.ShapeDtypeStruct((nrow, S5_L * width), BF16),
                   jax.ShapeDtypeStruct((nseq, nblk, 1, LANES), F32)],
        scratch_shapes=[pltpu.VMEM((nblk, 1, LANES), F32),
                        pltpu.VMEM((nblk, 1, LANES), F32),
                        pltpu.VMEM((ngb, MXU_DIM, 2 * sw), BF16),
                        pltpu.VMEM((nblk, S5_L * rows, LANES), F32),
                        pltpu.VMEM((nblk, rows, LANES), F32),
                        pltpu.VMEM((nblk, rows, LANES), F32),
                        pltpu.VMEM((nblk, 1, LANES), F32),
                        pltpu.VMEM((S5_L * rows, width), F32)],
        compiler_params=_cparams(1),
        name="s5_mix",
    )(u2, h0, lre, lim, dt, bre, bim, cbd, d, wglu, bglu)
    ht = ht.reshape(nseq, ngb, 2, groups // ngb, n_state)
    ht_re = ht[:, :, 0].reshape(nseq, groups, n_state)
    ht_im = ht[:, :, 1].reshape(nseq, groups, n_state)
    return y2.reshape(n, width), ht_re, ht_im


def _diff_lambda(lq1_ref, lk1_ref, lq2_ref, lk2_ref, lam_init):
    s1 = jnp.sum(lq1_ref[...] * lk1_ref[...], axis=-1, keepdims=True)
    s2 = jnp.sum(lq2_ref[...] * lk2_ref[...], axis=-1, keepdims=True)
    return jnp.exp(s1) - jnp.exp(s2) + lam_init


def _split_maps(q):
    lane = lax.broadcasted_iota(jnp.int32, q.shape, 1)
    zero = jnp.zeros_like(q)
    return jnp.where(lane < DIFF_DK, q, zero), jnp.where(lane >= DIFF_DK, q, zero)


def _chunk_mask(q0, k0, tq, tk):
    qc = (q0 + lax.broadcasted_iota(jnp.int32, (tq, tk), 0)) // CHUNK
    kc = (k0 + lax.broadcasted_iota(jnp.int32, (tq, tk), 1)) // CHUNK
    return kc <= qc


def _subln(o, g, lam_init):
    return (_rms(o, g) * (1.0 - lam_init)).astype(BF16)


def _flash_update(streams, mask):
    chains = []
    for k, q, vxt, m_ref, a_ref in streams:
        tq = q.shape[0]
        sub = min(tq, FLASH_COLS)
        for c0 in range(0, tq, sub):
            chains.append((k, q, vxt, m_ref, a_ref, slice(c0, c0 + sub)))
    scores = [_dot_t(k, q[cs]) for k, q, _, _, _, cs in chains]
    probs = []
    for (_, _, _, m_ref, _, cs), s in zip(chains, scores):
        if mask is not None:
            s = jnp.where(mask[:, cs], s, NEG_INF)
        m_prev = m_ref[:, cs]
        m_new = jnp.maximum(m_prev, jnp.max(s, axis=0, keepdims=True))
        m_ref[:, cs] = m_new
        probs.append((jnp.exp2(m_prev - m_new), jnp.exp2(s - m_new).astype(BF16)))
    for (_, _, vxt, _, a_ref, cs), (alpha, p) in zip(chains, probs):
        a_ref[:, cs] = alpha * a_ref[:, cs] + _dot(vxt, p)


def _chunk_mask_t(tk, tq):
    kc = lax.broadcasted_iota(jnp.int32, (tk, tq), 0) // CHUNK
    qc = lax.broadcasted_iota(jnp.int32, (tk, tq), 1) // CHUNK
    return kc <= qc


def _unit_rows(cols):
    return (lax.broadcasted_iota(jnp.int32, (VX_ROWS - LANES, cols), 0) == 0).astype(BF16)


def _diff_prompt_kernel(q_ref, k_ref, vt_ref, lq1_ref, lk1_ref, lq2_ref, lk2_ref, g_ref, o_ref,
                        vxt_ref, m_ref, a_ref, *, tq, hq, lam_init):
    i = pl.program_id(2)

    @pl.when(i == 0)
    def _():
        for j in range(hq):
            for kt in range(vxt_ref.shape[1]):
                vxt_ref[j, kt, 0:LANES, :] = vt_ref[j * LANES:(j + 1) * LANES, kt * tq:(kt + 1) * tq]
                vxt_ref[j, kt, LANES:VX_ROWS, :] = _unit_rows(tq)

    qs = [_split_maps(q_ref[:, j * LANES:(j + 1) * LANES]) for j in range(hq)]
    m_ref[...] = jnp.full(m_ref.shape, NEG_INF, F32)
    a_ref[...] = jnp.zeros(a_ref.shape, F32)

    def step(kt, mask):
        sl = pl.ds(pl.multiple_of(kt * tq, tq), tq)
        streams = []
        for j in range(hq):
            kb = k_ref[sl, j * LANES:(j + 1) * LANES]
            for mp in range(2):
                streams.append((kb, qs[j][mp], vxt_ref[j, kt], m_ref.at[2 * j + mp], a_ref.at[2 * j + mp]))
        _flash_update(streams, mask)

    def body(kt, carry):
        step(kt, None)
        return carry

    lax.fori_loop(0, i, body, 0)
    step(i, _chunk_mask_t(tq, tq))

    lam = _diff_lambda(lq1_ref, lk1_ref, lq2_ref, lk2_ref, lam_init)
    for j in range(hq):
        a1 = a_ref[2 * j]
        a2 = a_ref[2 * j + 1]
        ot = a1[0:LANES] / a1[LANES:LANES + 1] - lam * (a2[0:LANES] / a2[LANES:LANES + 1])
        ms = jnp.mean(ot * ot, axis=0, keepdims=True)
        ot = ot * lax.rsqrt(ms + EPS) * g_ref[...] * (1.0 - lam_init)
        o_ref[:, j * LANES:(j + 1) * LANES] = ot.T.astype(BF16)


def _diff_prompt(q, kb, vbt, lam_w, g, nbatch, seq_len, lam_init):
    n, dw = q.shape
    gcol = g.reshape(LANES, 1)
    heads = dw // LANES
    tq = _tile(seq_len, 512)
    hq = 2
    assert tq % CHUNK == 0 and heads % hq == 0
    nq = seq_len // tq
    kern = functools.partial(_diff_prompt_kernel, tq=tq, hq=hq, lam_init=lam_init)
    vec = lambda a: _resident(a.shape)
    return pl.pallas_call(
        kern,
        grid=(nbatch, heads // hq, nq),
        in_specs=[pl.BlockSpec((tq, hq * LANES), lambda b, h, i: (b * nq + i, h)),
                  pl.BlockSpec((seq_len, hq * LANES), lambda b, h, i: (b, h)),
                  pl.BlockSpec((hq * LANES, seq_len), lambda b, h, i: (h, b)),
                  vec(lam_w[0]), vec(lam_w[1]), vec(lam_w[2]), vec(lam_w[3]), vec(gcol)],
        out_specs=pl.BlockSpec((tq, hq * LANES), lambda b, h, i: (b * nq + i, h)),
        out_shape=jax.ShapeDtypeStruct((n, dw), BF16),
        scratch_shapes=[pltpu.VMEM((hq, nq, VX_ROWS, tq), BF16),
                        pltpu.VMEM((2 * hq, 1, tq), F32), pltpu.VMEM((2 * hq, VX_ROWS, tq), F32)],
        compiler_params=_cparams(3),
        name="diff_attn_prompt",
    )(q, kb, vbt, *lam_w, gcol)


def _diff_sample_kernel(q_ref, kc_ref, vc_ref, kn_ref, vn_ref, lq1_ref, lk1_ref, lq2_ref, lk2_ref, g_ref, o_ref,
                        *, past, heads, lam_init):
    tq = q_ref.shape[0]
    mask_n = jnp.concatenate([_chunk_mask(past, past, tq, tq)] * 2, axis=0)
    lam = _diff_lambda(lq1_ref, lk1_ref, lq2_ref, lk2_ref, lam_init)
    for h in range(heads):
        hs = slice(h * LANES, (h + 1) * LANES)
        qm = jnp.concatenate(_split_maps(q_ref[:, hs]), axis=0)
        kc = kc_ref[pl.ds(h, past, stride=heads), :].astype(BF16)
        vc = vc_ref[pl.ds(h, past, stride=heads), :].astype(BF16)
        sc = _dot_t(qm, kc)
        sn = jnp.where(mask_n, _dot_t(qm, kn_ref[:, hs]), NEG_INF)
        m = jnp.maximum(jnp.max(sc, axis=-1, keepdims=True), jnp.max(sn, axis=-1, keepdims=True))
        pc = jnp.exp2(sc - m)
        pn = jnp.exp2(sn - m)
        l = jnp.sum(pc, axis=-1, keepdims=True) + jnp.sum(pn, axis=-1, keepdims=True)
        o = (_dot(pc.astype(BF16), vc) + _dot(pn.astype(BF16), vn_ref[:, hs])) / l
        o_ref[:, hs] = _subln(o[0:tq] - lam * o[tq:2 * tq], g_ref[...], lam_init)


def _diff_sample(q, kb, vb, cache_k, cache_v, lam_w, g, nbatch, seq_len, lam_init):
    n, dw = q.shape
    heads = dw // LANES
    past = cache_k.shape[1]
    assert cache_k.shape[0] == nbatch and (past // CHUNK) * CHUNK == past and heads == SUBLANES
    kc = cache_k.reshape(nbatch * past * heads, LANES)
    vc = cache_v.reshape(nbatch * past * heads, LANES)
    kern = functools.partial(_diff_sample_kernel, past=past, heads=heads, lam_init=lam_init)
    vec = lambda a: _resident(a.shape)
    row = lambda b: (b, 0)
    return pl.pallas_call(
        kern,
        grid=(nbatch,),
        in_specs=[pl.BlockSpec((seq_len, dw), row),
                  pl.BlockSpec((past * heads, LANES), row), pl.BlockSpec((past * heads, LANES), row),
                  pl.BlockSpec((seq_len, dw), row), pl.BlockSpec((seq_len, dw), row),
                  vec(lam_w[0]), vec(lam_w[1]), vec(lam_w[2]), vec(lam_w[3]), vec(g)],
        out_specs=pl.BlockSpec((seq_len, dw), row),
        out_shape=jax.ShapeDtypeStruct((n, dw), BF16),
        compiler_params=_cparams(1),
        name="diff_attn_sample",
    )(q, kc, vc, kb, vb, *lam_w, g)


def _out_proj_kernel(*refs, n_lhs, final):
    lhs = refs[:n_lhs]
    w_ref, x_ref, g_ref = refs[n_lhs:n_lhs + 3]
    outs = refs[n_lhs + 3:]
    acc = x_ref[...]
    off = 0
    for a in lhs:
        kdim = a.shape[1]
        acc = acc + _dot(a[...], w_ref[off:off + kdim, :])
        off += kdim
    if final:
        outs[0][...] = _rms(acc, g_ref[...])
    else:
        outs[0][...] = acc
        outs[1][...] = _rms(acc, g_ref[...]).astype(BF16)


def _out_proj(lhs, w, x, g, final=False, tm_pref=512):
    n, d = x.shape
    tm = _tile(n, tm_pref)
    row = lambda i: (i, 0)
    kern = functools.partial(_out_proj_kernel, n_lhs=len(lhs), final=final)
    in_specs = [pl.BlockSpec((tm, a.shape[1]), row) for a in lhs]
    in_specs += [_resident(w.shape), pl.BlockSpec((tm, d), row), _resident(g.shape)]
    if final:
        out_specs = [pl.BlockSpec((tm, d), row)]
        out_shape = [jax.ShapeDtypeStruct((n, d), F32)]
    else:
        out_specs = [pl.BlockSpec((tm, d), row), pl.BlockSpec((tm, d), row)]
        out_shape = [jax.ShapeDtypeStruct((n, d), F32), jax.ShapeDtypeStruct((n, d), BF16)]
    return pl.pallas_call(
        kern, grid=(n // tm,), in_specs=in_specs, out_specs=out_specs, out_shape=out_shape,
        compiler_params=_cparams(1), name="out_proj",
    )(*lhs, w, x, g)


def _ffn_up_kernel(hn_ref, wv_ref, wg_ref, cw_ref, cb_ref, st_ref, act_ref, stout_ref, prev_ref,
                   *, seg, tiles_per_seq):
    i = pl.program_id(1)
    hn = hn_ref[...]
    tm = hn.shape[0]
    tf = act_ref.shape[1]
    if tiles_per_seq > 1:
        @pl.when((i % tiles_per_seq) == 0)
        def _():
            prev_ref[...] = st_ref[0]

    cwid = min(tf, MXU_DIM)
    for c0 in range(0, tf, cwid):
        cs = slice(c0, c0 + cwid)
        val = _dot(hn, wv_ref[:, cs])
        gate = _dot(hn, wg_ref[:, cs])
        cw = cw_ref[:, cs]
        cb = cb_ref[:, cs]
        for s in range(tm // seg):
            g0 = gate[s * seg:(s + 1) * seg]
            prev = st_ref[s, :, cs] if tiles_per_seq == 1 else prev_ref[:, cs]
            ext = jnp.concatenate([prev, g0], axis=0)
            g1 = pltpu.roll(ext, 1, 0)[SUBLANES:]
            g2 = pltpu.roll(ext, 2, 0)[SUBLANES:]
            c = cb + cw[0:1] * g2 + cw[1:2] * g1 + cw[2:3] * g0
            act_ref[s * seg:(s + 1) * seg, cs] = (jax.nn.silu(c) * val[s * seg:(s + 1) * seg]).astype(BF16)
            last = g0[seg - SUBLANES:seg]
            stout_ref[s, :, cs] = last
            if tiles_per_seq > 1:
                prev_ref[:, cs] = last


def _ffn_up(hn, w_in, conv_w, conv_b, conv_state, seq_len):
    n, d = hn.shape
    f = w_in.shape[1] // 2
    nseq = n // seq_len
    tf = max(t for t in range(MXU_DIM, f + 1, MXU_DIM) if f % t == 0 and 4 * d * t <= FFN_WEIGHT_BYTES)
    tm = _tile(n, 512)
    nf = f // tf
    wspec = lambda idx: pl.BlockSpec((d, tf), idx, pipeline_mode=pl.Buffered(1))
    if tm >= seq_len:
        seg, tps, spt = seq_len, 1, tm // seq_len
        st_idx = lambda j, i: (i, 0, j)
    else:
        seg, tps, spt = tm, seq_len // tm, 1
        st_idx = lambda j, i: (i // tps, 0, j)
    assert seg % SUBLANES == 0 and seg >= SUBLANES
    st = jnp.pad(conv_state, ((0, 0), (SUBLANES - (CONV_W - 1), 0), (0, 0)))
    cw = jnp.pad(conv_w, ((0, SUBLANES - CONV_W), (0, 0)))
    cb = conv_b.reshape(1, f)
    kern = functools.partial(_ffn_up_kernel, seg=seg, tiles_per_seq=tps)
    act, st_out = pl.pallas_call(
        kern,
        grid=(nf, n // tm),
        in_specs=[pl.BlockSpec((tm, d), lambda j, i: (i, 0)),
                  wspec(lambda j, i: (0, j)), wspec(lambda j, i: (0, nf + j)),
                  pl.BlockSpec((SUBLANES, tf), lambda j, i: (0, j)),
                  pl.BlockSpec((1, tf), lambda j, i: (0, j)),
                  pl.BlockSpec((spt, SUBLANES, tf), st_idx)],
        out_specs=[pl.BlockSpec((tm, tf), lambda j, i: (i, j)),
                   pl.BlockSpec((spt, SUBLANES, tf), st_idx)],
        out_shape=[jax.ShapeDtypeStruct((n, f), BF16), jax.ShapeDtypeStruct((nseq, SUBLANES, f), F32)],
        scratch_shapes=[pltpu.VMEM((SUBLANES, tf), F32)],
        compiler_params=_cparams(2),
        name="ffn_up",
    )(hn, w_in, w_in, cw, cb, st)
    return act, st_out[:, SUBLANES - (CONV_W - 1):, :]


def _rope_pair(x, cos, sin):
    return x * cos + pltpu.roll(x, MLA_ROPE, 1) * sin


def _odd_in_kernel(hn_ref, win_ref, gq_ref, gkv_ref, wuq_ref, cos_ref, sin_ref, *rest,
                   qr, kvr, heads, qscale, expand):
    if expand:
        wk_ref, wvt_ref, q_ref, ckv_ref, ckvb_ref, kpe_ref, kpeb_ref, kn_ref, vt_ref = rest
    else:
        q_ref, ckv_ref, ckvb_ref, kpe_ref, kpeb_ref = rest
    hn = hn_ref[...]
    cos = cos_ref[...]
    sin = sin_ref[...]
    cq = _rms(_dot(hn, win_ref[:, 0:qr]), gq_ref[...]).astype(BF16)
    ckv = _rms(_dot(hn, win_ref[:, qr:qr + kvr]), gkv_ref[...])
    kpe = _rope_pair(_dot(hn, win_ref[:, qr + kvr:qr + kvr + LANES]), cos, sin)
    ckv_ref[...] = ckv
    ckvb = ckv.astype(BF16)
    ckvb_ref[...] = ckvb
    kpe_ref[...] = kpe[:, 0:MLA_ROPE]
    kpeb_ref[...] = kpe.astype(BF16)
    hw = MXU_DIM
    for h in range(heads):
        qh = _dot(cq, wuq_ref[:, h * hw:(h + 1) * hw]) * qscale
        q_ref[:, h * hw:h * hw + LANES] = qh[:, 0:LANES].astype(BF16)
        q_ref[:, h * hw + LANES:(h + 1) * hw] = _rope_pair(qh[:, LANES:hw], cos, sin).astype(BF16)
    if expand:
        kn_ref[...] = _dot(ckvb, wk_ref[...]).astype(BF16)
        vt_ref[...] = _dot_t(wvt_ref[...], ckvb).astype(BF16)


def _odd_in(hn, win, gq, gkv, wuq, cos, sin, wkv, seq_len, heads):
    n, d = hn.shape
    qr, kvr = gq.shape[1], gkv.shape[1]
    tm = _tile(n, 256)
    row = lambda i: (i, 0)
    if tm <= seq_len:
        tps = seq_len // tm
        pos = lambda i: (i % tps, 0)
    else:
        cos = jnp.tile(cos, (tm // seq_len, 1))
        sin = jnp.tile(sin, (tm // seq_len, 1))
        pos = lambda i: (0, 0)
    expand = wkv is not None
    kern = functools.partial(_odd_in_kernel, qr=qr, kvr=kvr, heads=heads,
                             qscale=(MLA_NOPE + MLA_ROPE) ** -0.5 * LOG2E, expand=expand)
    in_specs = [pl.BlockSpec((tm, d), row), _resident(win.shape), _resident(gq.shape), _resident(gkv.shape),
                _resident(wuq.shape), pl.BlockSpec((tm, LANES), pos), pl.BlockSpec((tm, LANES), pos)]
    args = [hn, win, gq, gkv, wuq, cos, sin]
    out_specs = [pl.BlockSpec((tm, heads * MXU_DIM), row), pl.BlockSpec((tm, kvr), row),
                 pl.BlockSpec((tm, kvr), row), pl.BlockSpec((tm, MLA_ROPE), row), pl.BlockSpec((tm, LANES), row)]
    out_shape = [jax.ShapeDtypeStruct((n, heads * MXU_DIM), BF16), jax.ShapeDtypeStruct((n, kvr), F32),
                 jax.ShapeDtypeStruct((n, kvr), BF16), jax.ShapeDtypeStruct((n, MLA_ROPE), F32),
                 jax.ShapeDtypeStruct((n, LANES), BF16)]
    if expand:
        in_specs += [_resident(wkv[0].shape), _resident(wkv[1].shape)]
        args += list(wkv)
        out_specs += [pl.BlockSpec((tm, heads * MLA_NOPE), row),
                      pl.BlockSpec((heads * MLA_V, tm), lambda i: (0, i))]
        out_shape += [jax.ShapeDtypeStruct((n, heads * MLA_NOPE), BF16),
                      jax.ShapeDtypeStruct((heads * MLA_V, n), BF16)]
    return pl.pallas_call(
        kern, grid=(n // tm,), in_specs=in_specs, out_specs=out_specs, out_shape=out_shape,
        compiler_params=_cparams(1), name="odd_in",
    )(*args)


def _mla_prompt_kernel(q_ref, kn_ref, kpe_ref, vt_ref, o_ref, kx_ref, vxt_ref, m_ref, a_ref, *, tq, hp):
    i = pl.program_id(2)
    hw = MXU_DIM

    @pl.when(i == 0)
    def _():
        for j in range(hp):
            kx_ref[j, :, 0:MLA_NOPE] = kn_ref[:, j * MLA_NOPE:(j + 1) * MLA_NOPE]
            kx_ref[j, :, MLA_NOPE:hw] = kpe_ref[...]
            for kt in range(vxt_ref.shape[1]):
                vxt_ref[j, kt, 0:MLA_V, :] = vt_ref[j * MLA_V:(j + 1) * MLA_V, kt * tq:(kt + 1) * tq]
                vxt_ref[j, kt, MLA_V:VX_ROWS, :] = _unit_rows(tq)

    m_ref[...] = jnp.full(m_ref.shape, NEG_INF, F32)
    a_ref[...] = jnp.zeros(a_ref.shape, F32)

    def step(kt, mask):
        sl = pl.ds(pl.multiple_of(kt * tq, tq), tq)
        _flash_update([(kx_ref[j, sl, :], q_ref[:, j * hw:(j + 1) * hw], vxt_ref[j, kt], m_ref.at[j], a_ref.at[j])
                       for j in range(hp)], mask)

    def body(kt, carry):
        step(kt, None)
        return carry

    lax.fori_loop(0, i, body, 0)
    step(i, _chunk_mask_t(tq, tq))
    for j in range(hp):
        a = a_ref[j]
        o_ref[:, j * MLA_V:(j + 1) * MLA_V] = (a[0:MLA_V] / a[MLA_V:MLA_V + 1]).T.astype(BF16)


def _mla_prompt(q, kn, kpeb, vt, nbatch, seq_len, heads):
    n = q.shape[0]
    tq = _tile(seq_len, 512)
    hp = 4
    assert tq % CHUNK == 0 and heads % hp == 0 and MLA_V == LANES
    nq = seq_len // tq
    kern = functools.partial(_mla_prompt_kernel, tq=tq, hp=hp)
    return pl.pallas_call(
        kern,
        grid=(nbatch, heads // hp, nq),
        in_specs=[pl.BlockSpec((tq, hp * MXU_DIM), lambda b, h, i: (b * nq + i, h)),
                  pl.BlockSpec((seq_len, hp * MLA_NOPE), lambda b, h, i: (b, h)),
                  pl.BlockSpec((seq_len, LANES), lambda b, h, i: (b, 0)),
                  pl.BlockSpec((hp * MLA_V, seq_len), lambda b, h, i: (h, b))],
        out_specs=pl.BlockSpec((tq, hp * MLA_V), lambda b, h, i: (b * nq + i, h)),
        out_shape=jax.ShapeDtypeStruct((n, heads * MLA_V), BF16),
        scratch_shapes=[pltpu.VMEM((hp, seq_len, MXU_DIM), BF16),
                        pltpu.VMEM((hp, nq, VX_ROWS, tq), BF16),
                        pltpu.VMEM((hp, 1, tq), F32), pltpu.VMEM((hp, VX_ROWS, tq), F32)],
        compiler_params=_cparams(3),
        name="mla_attn_prompt",
    )(q, kn, kpeb, vt)


def _mla_sample_kernel(q_ref, cc_ref, pc_ref, cn_ref, pn_ref, wk_ref, wv_ref, o_ref, ql_ref, qp_ref,
                       *, heads, past):
    tq = q_ref.shape[0]
    hw = MXU_DIM
    for h in range(heads):
        qn = q_ref[:, h * hw:h * hw + MLA_NOPE]
        ql_ref[h * tq:(h + 1) * tq, :] = _dot(qn, wk_ref[h]).astype(BF16)
        qp_ref[h * tq:(h + 1) * tq, :] = q_ref[:, h * hw + MLA_NOPE:(h + 1) * hw]
    ql = ql_ref[...]
    qp = qp_ref[...]
    cc = cc_ref[...].astype(BF16)
    pc = pc_ref[...].astype(BF16)
    cn = cn_ref[...]
    mask_n = jnp.concatenate([_chunk_mask(past, past, tq, tq)] * heads, axis=0)
    sc = _dot_t(ql, cc) + _dot_t(qp[:, 0:MLA_ROPE], pc)
    sn = jnp.where(mask_n, _dot_t(ql, cn) + _dot_t(qp, pn_ref[...]), NEG_INF)
    m = jnp.maximum(jnp.max(sc, axis=-1, keepdims=True), jnp.max(sn, axis=-1, keepdims=True))
    ec = jnp.exp2(sc - m)
    en = jnp.exp2(sn - m)
    l = jnp.sum(ec, axis=-1, keepdims=True) + jnp.sum(en, axis=-1, keepdims=True)
    ol = ((_dot(ec.astype(BF16), cc) + _dot(en.astype(BF16), cn)) / l).astype(BF16)
    for h in range(heads):
        o_ref[:, h * MLA_V:(h + 1) * MLA_V] = _dot(ol[h * tq:(h + 1) * tq], wv_ref[h]).astype(BF16)


def _mla_sample(q, ckvb, kpeb, cache_ckv, cache_kpe, wk_t, wv, nbatch, seq_len, heads):
    n = q.shape[0]
    past, kvr = cache_ckv.shape[1], cache_ckv.shape[2]
    assert (past // CHUNK) * CHUNK == past
    cc = cache_ckv.reshape(nbatch * past, kvr)
    pc = cache_kpe.reshape(nbatch * past, MLA_ROPE)
    kern = functools.partial(_mla_sample_kernel, heads=heads, past=past)
    row = lambda b: (b, 0)
    return pl.pallas_call(
        kern,
        grid=(nbatch,),
        in_specs=[pl.BlockSpec((seq_len, heads * MXU_DIM), row),
                  pl.BlockSpec((past, kvr), row), pl.BlockSpec((past, MLA_ROPE), row),
                  pl.BlockSpec((seq_len, kvr), row), pl.BlockSpec((seq_len, LANES), row),
                  _resident(wk_t.shape), _resident(wv.shape)],
        out_specs=pl.BlockSpec((seq_len, heads * MLA_V), row),
        out_shape=jax.ShapeDtypeStruct((n, heads * MLA_V), BF16),
        scratch_shapes=[pltpu.VMEM((heads * seq_len, kvr), BF16), pltpu.VMEM((heads * seq_len, LANES), BF16)],
        compiler_params=_cparams(1),
        name="mla_attn_sample",
    )(q, cc, pc, ckvb, kpeb, wk_t, wv)


def _rope_tables(pos):
    half = MLA_ROPE // 2
    inv = ROPE_BASE ** (-jnp.arange(half, dtype=F32) / half)
    ang = pos.astype(F32)[:, None] * inv[None, :]
    cos, sin = jnp.cos(ang), jnp.sin(ang)
    zero = jnp.zeros_like(cos)
    return (jnp.concatenate([cos, cos, zero, zero], axis=1),
            jnp.concatenate([-sin, sin, zero, zero], axis=1))


def _swap_halves(w):
    half = MLA_ROPE // 2
    return jnp.concatenate([w[..., half:], w[..., :half]], axis=-1)


def _prepare_weights(p):
    w = {}
    heads = p['mla_w_uq'].shape[2]
    w['heads'] = heads
    w['norm_mix'] = p['norm_mix'][:, None, :]
    w['norm_ffn'] = p['norm_ffn'][:, None, :]
    w['norm_final'] = p['norm_final'][None, :]
    w['w_in_even'] = p['w_in_even'][0].astype(BF16)
    w['w_out_even'] = p['w_out_even'][0].astype(BF16)
    w['s5'] = dict(a_re=p['s5_a_re'][0], a_im=p['s5_a_im'][0], b_re=p['s5_b_re'][0], b_im=p['s5_b_im'][0],
                   c_re=p['s5_c_re'][0], c_im=p['s5_c_im'][0], d=p['s5_d'][0], log_dt=p['s5_log_dt'][0],
                   w_glu=p['s5_w_glu'][0], b_glu=p['s5_b_glu'][0])
    w['lam'] = [p[k][0][None, :] for k in ('diff_lambda_q1', 'diff_lambda_k1', 'diff_lambda_q2', 'diff_lambda_k2')]
    w['subln'] = p['diff_subln'][0][None, :]
    wi = p['w_in_odd'][0]
    qr = p['mla_q_norm'].shape[1]
    kvr = p['mla_kv_norm'].shape[1]
    wpe = wi[:, qr + kvr:]
    w['w_in_odd'] = jnp.concatenate([wi[:, :qr + kvr], wpe, _swap_halves(wpe)], axis=1).astype(BF16)
    w['gq'] = p['mla_q_norm'][0][None, :]
    w['gkv'] = p['mla_kv_norm'][0][None, :]
    wuq = p['mla_w_uq'][0]
    wuq = jnp.concatenate([wuq, _swap_halves(wuq[..., MLA_NOPE:])], axis=-1)
    w['w_uq'] = wuq.reshape(qr, heads * MXU_DIM).astype(BF16)
    wukv = p['mla_w_ukv'][0]
    w['w_kv'] = (wukv[..., :MLA_NOPE].reshape(kvr, heads * MLA_NOPE).astype(BF16),
                 wukv[..., MLA_NOPE:].reshape(kvr, heads * MLA_V).T.astype(BF16))
    w['w_uk_t'] = jnp.transpose(wukv[..., :MLA_NOPE], (1, 2, 0)).astype(BF16)
    w['w_uv'] = jnp.transpose(wukv[..., MLA_NOPE:], (1, 0, 2)).astype(BF16)
    w['w_out_odd'] = p['w_out_odd'][0].astype(BF16)
    depth = p['ffn_w_in'].shape[0]
    w['ffn_w_in'] = [p['ffn_w_in'][l].astype(BF16) for l in range(depth)]
    w['ffn_w_down'] = [p['ffn_w_down'][l].astype(BF16) for l in range(depth)]
    w['ffn_conv_w'] = p['ffn_conv_w']
    w['ffn_conv_b'] = p['ffn_conv_b']
    return w


def _trunk(x3, s5_re0, s5_im0, k_past, v_past, ckv_past, kpe_past, conv0, pos0, w):
    nb, t, d = x3.shape
    n = nb * t
    x = x3.reshape(n, d)
    heads = w['heads']
    s5w = w['s5']['d'].size
    dw = (w['w_in_even'].shape[1] - s5w) // 3
    dheads = dw // LANES

    lam_init = 0.8 - 0.6 * math.exp(-0.3 * 0)
    u, q, k, v, kb, vb, vbt = _even_in(x, w['norm_mix'][0], w['w_in_even'], s5w, dw)
    y_s5, ht_re, ht_im = _s5(u, s5_re0, s5_im0, t, w['s5'])
    if k_past is None:
        o = _diff_prompt(q, kb, vbt, w['lam'], w['subln'], nb, t, lam_init)
    else:
        o = _diff_sample(q, kb, vb, k_past, v_past, w['lam'], w['subln'], nb, t, lam_init)
    x, hn = _out_proj([y_s5, o], w['w_out_even'], x, w['norm_ffn'][0])
    act, conv_a = _ffn_up(hn, w['ffn_w_in'][0], w['ffn_conv_w'][0], w['ffn_conv_b'][0], conv0[0], t)
    x, hn = _out_proj([act], w['ffn_w_down'][0], x, w['norm_mix'][1], tm_pref=256)

    cos, sin = _rope_tables(pos0 + jnp.arange(t, dtype=jnp.int32))
    if ckv_past is None:
        qm, ckv, ckvb, kpe, kpeb, kn, vm = _odd_in(hn, w['w_in_odd'], w['gq'], w['gkv'], w['w_uq'], cos, sin,
                                                   w['w_kv'], t, heads)
        om = _mla_prompt(qm, kn, kpeb, vm, nb, t, heads)
    else:
        qm, ckv, ckvb, kpe, kpeb = _odd_in(hn, w['w_in_odd'], w['gq'], w['gkv'], w['w_uq'], cos, sin,
                                           None, t, heads)
        om = _mla_sample(qm, ckvb, kpeb, ckv_past, kpe_past, w['w_uk_t'], w['w_uv'], nb, t, heads)
    x, hn = _out_proj([om], w['w_out_odd'], x, w['norm_ffn'][1])
    act, conv_b = _ffn_up(hn, w['ffn_w_in'][1], w['ffn_conv_w'][1], w['ffn_conv_b'][1], conv0[1], t)
    (y,) = _out_proj([act], w['ffn_w_down'][1], x, w['norm_final'], final=True, tm_pref=256)

    groups, n_state = w['s5']['a_re'].shape
    return (y.reshape(nb, t, d), ht_re[None], ht_im[None],
            k.reshape(1, nb, t, dheads, LANES), v.reshape(1, nb, t, dheads, LANES),
            ckv.reshape(1, nb, t, -1), kpe.reshape(1, nb, t, MLA_ROPE), jnp.stack([conv_a, conv_b]))


def kernel(x_prompt, x_sample, state_s5_re, state_s5_im, cache_diff_k, cache_diff_v, cache_mla_ckv, cache_mla_kpe, state_ffn_conv, norm_mix, norm_ffn, norm_final, w_in_even, w_out_even, s5_a_re, s5_a_im, s5_b_re, s5_b_im, s5_c_re, s5_c_im, s5_d, s5_log_dt, s5_w_glu, s5_b_glu, diff_lambda_q1, diff_lambda_k1, diff_lambda_q2, diff_lambda_k2, diff_subln, w_in_odd, mla_q_norm, mla_kv_norm, mla_w_uq, mla_w_ukv, w_out_odd, ffn_w_in, ffn_conv_w, ffn_conv_b, ffn_w_down):
    w = _prepare_weights(dict(
        norm_mix=norm_mix, norm_ffn=norm_ffn, norm_final=norm_final, w_in_even=w_in_even, w_out_even=w_out_even,
        s5_a_re=s5_a_re, s5_a_im=s5_a_im, s5_b_re=s5_b_re, s5_b_im=s5_b_im, s5_c_re=s5_c_re, s5_c_im=s5_c_im,
        s5_d=s5_d, s5_log_dt=s5_log_dt, s5_w_glu=s5_w_glu, s5_b_glu=s5_b_glu,
        diff_lambda_q1=diff_lambda_q1, diff_lambda_k1=diff_lambda_k1, diff_lambda_q2=diff_lambda_q2,
        diff_lambda_k2=diff_lambda_k2, diff_subln=diff_subln, w_in_odd=w_in_odd, mla_q_norm=mla_q_norm,
        mla_kv_norm=mla_kv_norm, mla_w_uq=mla_w_uq, mla_w_ukv=mla_w_ukv, w_out_odd=w_out_odd,
        ffn_w_in=ffn_w_in, ffn_conv_w=ffn_conv_w, ffn_conv_b=ffn_conv_b, ffn_w_down=ffn_w_down))
    nb_p = x_prompt.shape[0]
    groups, n_state = s5_a_re.shape[1:]
    d_ff = ffn_conv_b.shape[1]
    depth = ffn_conv_b.shape[0]
    s5_zero = jnp.zeros((nb_p, groups, n_state), F32)
    conv_zero = jnp.zeros((depth, nb_p, CONV_W - 1, d_ff), F32)
    (y_p, re_p, im_p, k_p, v_p, ckv_p, kpe_p, conv_p) = _trunk(
        x_prompt, s5_zero, s5_zero, None, None, None, None, conv_zero, 0, w)
    past = cache_diff_k.shape[2]
    (y_s, re_s, im_s, k_s, v_s, ckv_s, kpe_s, conv_s) = _trunk(
        x_sample, state_s5_re[0], state_s5_im[0], cache_diff_k[0], cache_diff_v[0], cache_mla_ckv[0],
        cache_mla_kpe[0], state_ffn_conv, past, w)
    return (y_p, y_s, re_p, im_p, re_s, im_s, k_p, v_p, k_s, v_s, ckv_p, kpe_p, ckv_s, kpe_s, conv_p, conv_s)
```

```python
import functools
import math

import jax
import jax.numpy as jnp
from jax import lax
from jax.experimental import pallas as pl
from jax.experimental.pallas import tpu as pltpu

F32 = jnp.float32
BF16 = jnp.bfloat16

CHUNK = 64
EPS = 1e-6
NEG_INF = -1e30
ROPE_BASE = 10000.0
S5_GROUP = 16
DIFF_DK = 64
MLA_NOPE = 128
MLA_ROPE = 64
MLA_V = 128
CONV_W = 3
LOG2E = 1.4426950408889634

LANES = 128
SUBLANES = 8
MXU_DIM = 256
VMEM_LIMIT_BYTES = 56 * 1024 * 1024

FFN_WEIGHT_BYTES = 24 * 1024 * 1024
FLASH_COLS = 256
VX_ROWS = LANES + 16
S5_L = 16
S5_GB = MXU_DIM // S5_GROUP


def _cparams(n_axes):
    return pltpu.CompilerParams(
        dimension_semantics=("arbitrary",) * n_axes,
        vmem_limit_bytes=VMEM_LIMIT_BYTES)


def _resident(shape):
    nd = len(shape)
    return pl.BlockSpec(shape, lambda *_: (0,) * nd, pipeline_mode=pl.Buffered(1))


def _tile(n, pref):
    t = min(n, pref)
    while n % t:
        t //= 2
    return t


def _dot(a, b):
    return jnp.dot(a, b, preferred_element_type=F32)


def _dot_t(a, b):
    return lax.dot_general(a, b, (((1,), (1,)), ((), ())), preferred_element_type=F32)


def _rms(x, g):
    ms = jnp.mean(x * x, axis=-1, keepdims=True)
    return x * lax.rsqrt(ms + EPS) * g


def _even_in_kernel(x_ref, g_ref, w_ref, u_ref, q_ref, k_ref, v_ref, kb_ref, vb_ref, vbt_ref,
                    *, s5w, dw, qscale):
    xn = _rms(x_ref[...], g_ref[...]).astype(BF16)
    u_ref[...] = _dot(xn, w_ref[:, 0:s5w])
    q_ref[...] = (_dot(xn, w_ref[:, s5w:s5w + dw]) * qscale).astype(BF16)
    k = _dot(xn, w_ref[:, s5w + dw:s5w + 2 * dw])
    k_ref[...] = k
    kb_ref[...] = k.astype(BF16)
    v = _dot(xn, w_ref[:, s5w + 2 * dw:s5w + 3 * dw])
    v_ref[...] = v
    vb_ref[...] = v.astype(BF16)
    vbt_ref[...] = v.T.astype(BF16)


def _even_in(x, g, w, s5w, dw):
    n, d = x.shape
    tm = _tile(n, 512)
    row = lambda i: (i, 0)
    kern = functools.partial(_even_in_kernel, s5w=s5w, dw=dw, qscale=DIFF_DK ** -0.5 * LOG2E)
    return pl.pallas_call(
        kern,
        grid=(n // tm,),
        in_specs=[pl.BlockSpec((tm, d), row), _resident(g.shape), _resident(w.shape)],
        out_specs=[pl.BlockSpec((tm, s5w), row), pl.BlockSpec((tm, dw), row), pl.BlockSpec((tm, dw), row),
                   pl.BlockSpec((tm, dw), row), pl.BlockSpec((tm, dw), row), pl.BlockSpec((tm, dw), row),
                   pl.BlockSpec((dw, tm), lambda i: (0, i))],
        out_shape=[jax.ShapeDtypeStruct((n, s5w), F32), jax.ShapeDtypeStruct((n, dw), BF16),
                   jax.ShapeDtypeStruct((n, dw), F32), jax.ShapeDtypeStruct((n, dw), F32),
                   jax.ShapeDtypeStruct((n, dw), BF16), jax.ShapeDtypeStruct((n, dw), BF16),
                   jax.ShapeDtypeStruct((dw, n), BF16)],
        compiler_params=_cparams(1),
        name="even_in",
    )(x, g, w)


def _s5_kernel(u_ref, h0_ref, lre_ref, lim_ref, dt_ref, bre_ref, bim_ref, cbd_ref, d_ref, wglu_ref, bglu_ref,
               y_ref, ht_ref,
               a_ref, a16_ref, bbd_ref, us_ref, xs_ref, xe_ref, hs_ref, carry_ref, ys_ref, yn_ref,
               *, rows, width, cps, ngb):
    t = pl.program_id(0)
    nblk = ngb * 16
    half = 8
    gw = MXU_DIM

    @pl.when(t == 0)
    def _prepare():
        carry_ref[...] = jnp.zeros_like(carry_ref)
        for gb in range(ngb):
            for k in range(half):
                lre = jnp.minimum(lre_ref[gb * half + k], -1e-4)
                lim = lim_ref[gb * half + k]
                dt = jnp.exp(dt_ref[gb * half + k])
                mag = jnp.exp(lre * dt)
                are = mag * jnp.cos(lim * dt)
                aim = mag * jnp.sin(lim * dt)
                den = lre * lre + lim * lim
                cre = ((are - 1.0) * lre + aim * lim) / den
                cim = (aim * lre - (are - 1.0) * lim) / den
                ire, iim = gb * 16 + k, gb * 16 + half + k
                a_ref[ire] = are
                a_ref[iim] = aim
                pre, pim = are, aim
                for _ in range(4):
                    pre, pim = pre * pre - pim * pim, 2.0 * pre * pim
                a16_ref[ire] = pre
                a16_ref[iim] = pim
                br = bre_ref[gb, :, k * LANES:(k + 1) * LANES]
                bi = bim_ref[gb, :, k * LANES:(k + 1) * LANES]
                bbd_ref[gb, :, k * LANES:(k + 1) * LANES] = (cre * br - cim * bi).astype(BF16)
                bbd_ref[gb, :, (half + k) * LANES:(half + k + 1) * LANES] = (cre * bi + cim * br).astype(BF16)

    ncb = width // LANES
    for cb in range(ncb):
        us_ref[cb] = u_ref[:, cb * LANES:(cb + 1) * LANES]

    def step(l):
        return pl.ds(l * rows, rows)

    def at_step(l):
        return pl.ds(l, rows, stride=S5_L)

    for gb in range(ngb):
        cbs = range(gb * gw // LANES, (gb + 1) * gw // LANES)
        lhs = jnp.concatenate(
            [jnp.concatenate([us_ref[cb, at_step(l), :] for cb in cbs], axis=1) for l in range(S5_L)], axis=0)
        x = _dot(lhs.astype(BF16), bbd_ref[gb])
        for j in range(16):
            xs_ref[gb * 16 + j] = x[:, j * LANES:(j + 1) * LANES]
        for k in range(half):
            ire, iim = gb * 16 + k, gb * 16 + half + k
            are, aim = a_ref[ire], a_ref[iim]
            hre = xs_ref[ire, step(0), :]
            him = xs_ref[iim, step(0), :]
            for l in range(1, S5_L):
                xre = xs_ref[ire, step(l), :]
                xim = xs_ref[iim, step(l), :]
                hre, him = are * hre - aim * him + xre, are * him + aim * hre + xim
            xe_ref[ire] = hre
            xe_ref[iim] = him

    base_row = t * rows
    for grp in range(ngb * half // 8):
        ire0 = (grp // (half // 8)) * 16 + (grp % (half // 8)) * 8
        iim0 = ire0 + half
        a16re = a16_ref[ire0:ire0 + 8]
        a16im = a16_ref[iim0:iim0 + 8]

        def scan_body(r, carry, ire0=ire0, iim0=iim0, a16re=a16re, a16im=a16im):
            cre, cim = carry
            gr = base_row + r
            seq = gr // cps - (base_row // cps)
            is_start = (gr % cps) == 0
            cre = jnp.where(is_start, h0_ref[seq, ire0:ire0 + 8], cre)
            cim = jnp.where(is_start, h0_ref[seq, iim0:iim0 + 8], cim)
            hs_ref[ire0:ire0 + 8, pl.ds(r, 1), :] = cre
            hs_ref[iim0:iim0 + 8, pl.ds(r, 1), :] = cim
            xre = xe_ref[ire0:ire0 + 8, pl.ds(r, 1), :]
            xim = xe_ref[iim0:iim0 + 8, pl.ds(r, 1), :]
            nre = a16re * cre - a16im * cim + xre
            nim = a16re * cim + a16im * cre + xim

            @pl.when((gr % cps) == cps - 1)
            def _():
                ht_ref[seq, ire0:ire0 + 8] = nre
                ht_ref[seq, iim0:iim0 + 8] = nim

            return nre, nim

        cre, cim = lax.fori_loop(0, rows, scan_body, (carry_ref[ire0:ire0 + 8], carry_ref[iim0:iim0 + 8]))
        carry_ref[ire0:ire0 + 8] = cre
        carry_ref[iim0:iim0 + 8] = cim

    for gb in range(ngb):
        for k in range(half):
            ire, iim = gb * 16 + k, gb * 16 + half + k
            are, aim = a_ref[ire], a_ref[iim]
            hre = hs_ref[ire]
            him = hs_ref[iim]
            for l in range(S5_L):
                xre = xs_ref[ire, step(l), :]
                xim = xs_ref[iim, step(l), :]
                hre, him = are * hre - aim * him + xre, are * him + aim * hre + xim
                xs_ref[ire, step(l), :] = hre
                xs_ref[iim, step(l), :] = him
        h = jnp.concatenate([xs_ref[gb * 16 + j].astype(BF16) for j in range(16)], axis=1)
        yc = _dot(h, cbd_ref[gb])
        for c in range(gw // LANES):
            ys_ref[gb * gw // LANES + c] = yc[:, c * LANES:(c + 1) * LANES]

    for cb in range(ncb):
        for r in range(rows):
            yn_ref[r * S5_L:(r + 1) * S5_L, cb * LANES:(cb + 1) * LANES] = ys_ref[cb, pl.ds(r, S5_L, stride=rows), :]

    y = yn_ref[...] + d_ref[...] * u_ref[...]
    z = jax.nn.gelu(y)
    gate = jax.nn.sigmoid(_dot(z.astype(BF16), wglu_ref[...]) + bglu_ref[...])
    y_ref[...] = (z * gate).astype(BF16)


def _s5_layout(p, ngb, n_state):
    return p.reshape(ngb * (S5_GB * n_state // LANES), 1, LANES)


def _s5_state_to_blocks(s, ngb):
    b = s.shape[0]
    return s.reshape(b, ngb, 8, 1, LANES)


def _s5(u, h0_re, h0_im, seq_len, w):
    n, width = u.shape
    groups, n_state = w['a_re'].shape
    assert n_state * S5_GB == 8 * LANES and width == groups * S5_GROUP
    ngb = groups // S5_GB
    nseq = n // seq_len
    cps = seq_len // S5_L
    nrow = n // S5_L
    rows = _tile(nrow, 32)
    assert rows % SUBLANES == 0 and (cps % rows == 0 or rows % cps == 0)
    spt = max(1, rows // cps)
    tps = max(1, cps // rows)
    nblk = ngb * 16
    sw = S5_GB * n_state

    h0 = jnp.concatenate([_s5_state_to_blocks(h0_re, ngb), _s5_state_to_blocks(h0_im, ngb)], axis=2)
    h0 = h0.reshape(nseq, nblk, 1, LANES)
    lre = _s5_layout(w['a_re'], ngb, n_state)
    lim = _s5_layout(w['a_im'], ngb, n_state)
    dt = _s5_layout(jnp.broadcast_to(w['log_dt'][:, None], (groups, n_state)), ngb, n_state)
    eye = jnp.eye(S5_GB, dtype=F32)

    def bdiag_b(b):
        bb = b.reshape(ngb, S5_GB, n_state, S5_GROUP)
        return jnp.einsum('agnp,gh->agphn', bb, eye).reshape(ngb, S5_GB * S5_GROUP, sw)

    def bdiag_c(c):
        cc = c.reshape(ngb, S5_GB, S5_GROUP, n_state)
        return jnp.einsum('agpn,gh->ahngp', cc, eye).reshape(ngb, sw, S5_GB * S5_GROUP)

    bre = bdiag_b(w['b_re'])
    bim = bdiag_b(w['b_im'])
    cbd = jnp.concatenate([bdiag_c(w['c_re']), -bdiag_c(w['c_im'])], axis=1).astype(BF16)
    d = w['d'].reshape(1, width)
    wglu = w['w_glu'].astype(BF16)
    bglu = w['b_glu'].reshape(1, width)

    seq_idx = (lambda t: (t // tps, 0, 0, 0)) if tps > 1 else (lambda t: (t, 0, 0, 0))
    kern = functools.partial(_s5_kernel, rows=rows, width=width, cps=cps, ngb=ngb)
    y, ht = pl.pallas_call(
        kern,
        grid=(nrow // rows,),
        in_specs=[pl.BlockSpec((S5_L * rows, width), lambda t: (t, 0)),
                  pl.BlockSpec((spt, nblk, 1, LANES), seq_idx),
                  _resident(lre.shape), _resident(lim.shape), _resident(dt.shape),
                  _resident(bre.shape), _resident(bim.shape), _resident(cbd.shape),
                  _resident(d.shape), _resident(wglu.shape), _resident(bglu.shape)],
        out_specs=[pl.BlockSpec((S5_L * rows, width), lambda t: (t, 0)),
                   pl.BlockSpec((spt, nblk, 1, LANES), seq_idx)],
        out_shape=[jax.ShapeDtypeStruct((n, width), BF16),
                   jax.ShapeDtypeStruct((nseq, nblk, 1, LANES), F32)],
        scratch_shapes=[pltpu.VMEM((nblk, 1, LANES), F32),
                        pltpu.VMEM((nblk, 1, LANES), F32),
                        pltpu.VMEM((ngb, MXU_DIM, 2 * sw), BF16),
                        pltpu.VMEM((width // LANES, S5_L * rows, LANES), F32),
                        pltpu.VMEM((nblk, S5_L * rows, LANES), F32),
                        pltpu.VMEM((nblk, rows, LANES), F32),
                        pltpu.VMEM((nblk, rows, LANES), F32),
                        pltpu.VMEM((nblk, 1, LANES), F32),
                        pltpu.VMEM((width // LANES, S5_L * rows, LANES), F32),
                        pltpu.VMEM((S5_L * rows, width), F32)],
        compiler_params=_cparams(1),
        name="s5_mix",
    )(u, h0, lre, lim, dt, bre, bim, cbd, d, wglu, bglu)
    ht = ht.reshape(nseq, ngb, 2, groups // ngb, n_state)
    ht_re = ht[:, :, 0].reshape(nseq, groups, n_state)
    ht_im = ht[:, :, 1].reshape(nseq, groups, n_state)
    return y, ht_re, ht_im


def _diff_lambda(lq1_ref, lk1_ref, lq2_ref, lk2_ref, lam_init):
    s1 = jnp.sum(lq1_ref[...] * lk1_ref[...], axis=-1, keepdims=True)
    s2 = jnp.sum(lq2_ref[...] * lk2_ref[...], axis=-1, keepdims=True)
    return jnp.exp(s1) - jnp.exp(s2) + lam_init


def _split_maps(q):
    lane = lax.broadcasted_iota(jnp.int32, q.shape, 1)
    zero = jnp.zeros_like(q)
    return jnp.where(lane < DIFF_DK, q, zero), jnp.where(lane >= DIFF_DK, q, zero)


def _chunk_mask(q0, k0, tq, tk):
    qc = (q0 + lax.broadcasted_iota(jnp.int32, (tq, tk), 0)) // CHUNK
    kc = (k0 + lax.broadcasted_iota(jnp.int32, (tq, tk), 1)) // CHUNK
    return kc <= qc


def _subln(o, g, lam_init):
    return (_rms(o, g) * (1.0 - lam_init)).astype(BF16)


def _flash_update(streams, mask):
    chains = []
    for k, q, vxt, m_ref, a_ref in streams:
        tq = q.shape[0]
        sub = min(tq, FLASH_COLS)
        for c0 in range(0, tq, sub):
            nk = k.shape[0] if mask is None else min(k.shape[0], c0 + sub)
            chains.append((k[0:nk], q, vxt[:, 0:nk], m_ref, a_ref, slice(c0, c0 + sub), nk))
    scores = [_dot_t(k, q[cs]) for k, q, _, _, _, cs, _ in chains]
    probs = []
    for (_, _, _, m_ref, _, cs, nk), s in zip(chains, scores):
        if mask is not None:
            s = jnp.where(mask[0:nk, cs], s, NEG_INF)
        m_prev = m_ref[:, cs]
        m_new = jnp.maximum(m_prev, jnp.max(s, axis=0, keepdims=True))
        m_ref[:, cs] = m_new
        probs.append((jnp.exp2(m_prev - m_new), jnp.exp2(s - m_new).astype(BF16)))
    for (_, _, vxt, _, a_ref, cs, _), (alpha, p) in zip(chains, probs):
        a_ref[:, cs] = alpha * a_ref[:, cs] + _dot(vxt, p)


def _chunk_mask_t(tk, tq):
    kc = lax.broadcasted_iota(jnp.int32, (tk, tq), 0) // CHUNK
    qc = lax.broadcasted_iota(jnp.int32, (tk, tq), 1) // CHUNK
    return kc <= qc


def _unit_rows(cols):
    return (lax.broadcasted_iota(jnp.int32, (VX_ROWS - LANES, cols), 0) == 0).astype(BF16)


def _diff_prompt_kernel(q_ref, k_ref, vt_ref, lq1_ref, lk1_ref, lq2_ref, lk2_ref, g_ref, o_ref,
                        vxt_ref, m_ref, a_ref, *, tq, hq, lam_init):
    i = pl.program_id(2)

    @pl.when(i == 0)
    def _():
        for j in range(hq):
            for kt in range(vxt_ref.shape[1]):
                vxt_ref[j, kt, 0:LANES, :] = vt_ref[j * LANES:(j + 1) * LANES, kt * tq:(kt + 1) * tq]
                vxt_ref[j, kt, LANES:VX_ROWS, :] = _unit_rows(tq)

    qs = [_split_maps(q_ref[:, j * LANES:(j + 1) * LANES]) for j in range(hq)]
    m_ref[...] = jnp.full(m_ref.shape, NEG_INF, F32)
    a_ref[...] = jnp.zeros(a_ref.shape, F32)

    def step(kt, mask):
        sl = pl.ds(pl.multiple_of(kt * tq, tq), tq)
        streams = []
        for j in range(hq):
            kb = k_ref[sl, j * LANES:(j + 1) * LANES]
            for mp in range(2):
                streams.append((kb, qs[j][mp], vxt_ref[j, kt], m_ref.at[2 * j + mp], a_ref.at[2 * j + mp]))
        _flash_update(streams, mask)

    def body(kt, carry):
        step(kt, None)
        return carry

    lax.fori_loop(0, i, body, 0)
    step(i, _chunk_mask_t(tq, tq))

    lam = _diff_lambda(lq1_ref, lk1_ref, lq2_ref, lk2_ref, lam_init)
    for j in range(hq):
        a1 = a_ref[2 * j]
        a2 = a_ref[2 * j + 1]
        ot = a1[0:LANES] / a1[LANES:LANES + 1] - lam * (a2[0:LANES] / a2[LANES:LANES + 1])
        ms = jnp.mean(ot * ot, axis=0, keepdims=True)
        ot = ot * lax.rsqrt(ms + EPS) * g_ref[...] * (1.0 - lam_init)
        o_ref[:, j * LANES:(j + 1) * LANES] = ot.T.astype(BF16)


def _diff_prompt(q, kb, vbt, lam_w, g, nbatch, seq_len, lam_init):
    n, dw = q.shape
    gcol = g.reshape(LANES, 1)
    heads = dw // LANES
    tq = _tile(seq_len, 512)
    hq = 2
    assert tq % CHUNK == 0 and heads % hq == 0
    nq = seq_len // tq
    kern = functools.partial(_diff_prompt_kernel, tq=tq, hq=hq, lam_init=lam_init)
    vec = lambda a: _resident(a.shape)
    return pl.pallas_call(
        kern,
        grid=(nbatch, heads // hq, nq),
        in_specs=[pl.BlockSpec((tq, hq * LANES), lambda b, h, i: (b * nq + i, h)),
                  pl.BlockSpec((seq_len, hq * LANES), lambda b, h, i: (b, h)),
                  pl.BlockSpec((hq * LANES, seq_len), lambda b, h, i: (h, b)),
                  vec(lam_w[0]), vec(lam_w[1]), vec(lam_w[2]), vec(lam_w[3]), vec(gcol)],
        out_specs=pl.BlockSpec((tq, hq * LANES), lambda b, h, i: (b * nq + i, h)),
        out_shape=jax.ShapeDtypeStruct((n, dw), BF16),
        scratch_shapes=[pltpu.VMEM((hq, nq, VX_ROWS, tq), BF16),
                        pltpu.VMEM((2 * hq, 1, tq), F32), pltpu.VMEM((2 * hq, VX_ROWS, tq), F32)],
        compiler_params=_cparams(3),
        name="diff_attn_prompt",
    )(q, kb, vbt, *lam_w, gcol)


def _diff_sample_kernel(q_ref, kc_ref, vc_ref, kn_ref, vn_ref, lq1_ref, lk1_ref, lq2_ref, lk2_ref, g_ref, o_ref,
                        *, past, heads, lam_init):
    tq = q_ref.shape[0]
    mask_n = jnp.concatenate([_chunk_mask(past, past, tq, tq)] * 2, axis=0)
    lam = _diff_lambda(lq1_ref, lk1_ref, lq2_ref, lk2_ref, lam_init)
    for h in range(heads):
        hs = slice(h * LANES, (h + 1) * LANES)
        qm = jnp.concatenate(_split_maps(q_ref[:, hs]), axis=0)
        kc = kc_ref[pl.ds(h, past, stride=heads), :].astype(BF16)
        vc = vc_ref[pl.ds(h, past, stride=heads), :].astype(BF16)
        sc = _dot_t(qm, kc)
        sn = jnp.where(mask_n, _dot_t(qm, kn_ref[:, hs]), NEG_INF)
        m = jnp.maximum(jnp.max(sc, axis=-1, keepdims=True), jnp.max(sn, axis=-1, keepdims=True))
        pc = jnp.exp2(sc - m)
        pn = jnp.exp2(sn - m)
        l = jnp.sum(pc, axis=-1, keepdims=True) + jnp.sum(pn, axis=-1, keepdims=True)
        o = (_dot(pc.astype(BF16), vc) + _dot(pn.astype(BF16), vn_ref[:, hs])) / l
        o_ref[:, hs] = _subln(o[0:tq] - lam * o[tq:2 * tq], g_ref[...], lam_init)


def _diff_sample(q, kb, vb, cache_k, cache_v, lam_w, g, nbatch, seq_len, lam_init):
    n, dw = q.shape
    heads = dw // LANES
    past = cache_k.shape[1]
    assert cache_k.shape[0] == nbatch and (past // CHUNK) * CHUNK == past and heads == SUBLANES
    kc = cache_k.reshape(nbatch * past * heads, LANES)
    vc = cache_v.reshape(nbatch * past * heads, LANES)
    kern = functools.partial(_diff_sample_kernel, past=past, heads=heads, lam_init=lam_init)
    vec = lambda a: _resident(a.shape)
    row = lambda b: (b, 0)
    return pl.pallas_call(
        kern,
        grid=(nbatch,),
        in_specs=[pl.BlockSpec((seq_len, dw), row),
                  pl.BlockSpec((past * heads, LANES), row), pl.BlockSpec((past * heads, LANES), row),
                  pl.BlockSpec((seq_len, dw), row), pl.BlockSpec((seq_len, dw), row),
                  vec(lam_w[0]), vec(lam_w[1]), vec(lam_w[2]), vec(lam_w[3]), vec(g)],
        out_specs=pl.BlockSpec((seq_len, dw), row),
        out_shape=jax.ShapeDtypeStruct((n, dw), BF16),
        compiler_params=_cparams(1),
        name="diff_attn_sample",
    )(q, kc, vc, kb, vb, *lam_w, g)


def _out_proj_kernel(*refs, n_lhs, final):
    lhs = refs[:n_lhs]
    w_ref, x_ref, g_ref = refs[n_lhs:n_lhs + 3]
    outs = refs[n_lhs + 3:]
    acc = x_ref[...]
    off = 0
    for a in lhs:
        kdim = a.shape[1]
        acc = acc + _dot(a[...], w_ref[off:off + kdim, :])
        off += kdim
    if final:
        outs[0][...] = _rms(acc, g_ref[...])
    else:
        outs[0][...] = acc
        outs[1][...] = _rms(acc, g_ref[...]).astype(BF16)


def _out_proj(lhs, w, x, g, final=False, tm_pref=512, layer=None):
    n, d = x.shape
    tm = _tile(n, tm_pref)
    row = lambda i: (i, 0)
    kern = functools.partial(_out_proj_kernel, n_lhs=len(lhs), final=final)
    in_specs = [pl.BlockSpec((tm, a.shape[1]), row) for a in lhs]
    if layer is None:
        wspec = _resident(w.shape)
    else:
        wspec = pl.BlockSpec((None,) + w.shape[1:], lambda *_: (layer, 0, 0), pipeline_mode=pl.Buffered(1))
    in_specs += [wspec, pl.BlockSpec((tm, d), row), _resident(g.shape)]
    if final:
        out_specs = [pl.BlockSpec((tm, d), row)]
        out_shape = [jax.ShapeDtypeStruct((n, d), F32)]
    else:
        out_specs = [pl.BlockSpec((tm, d), row), pl.BlockSpec((tm, d), row)]
        out_shape = [jax.ShapeDtypeStruct((n, d), F32), jax.ShapeDtypeStruct((n, d), BF16)]
    return pl.pallas_call(
        kern, grid=(n // tm,), in_specs=in_specs, out_specs=out_specs, out_shape=out_shape,
        compiler_params=_cparams(1), name="out_proj",
    )(*lhs, w, x, g)


def _ffn_up_kernel(hn_ref, wv_ref, wg_ref, cw_ref, cb_ref, st_ref, act_ref, stout_ref, prev_ref,
                   *, seg, tiles_per_seq):
    i = pl.program_id(1)
    hn = hn_ref[...]
    tm = hn.shape[0]
    tf = act_ref.shape[1]
    if tiles_per_seq > 1:
        @pl.when((i % tiles_per_seq) == 0)
        def _():
            prev_ref[...] = st_ref[0]

    cwid = min(tf, MXU_DIM)
    for c0 in range(0, tf, cwid):
        cs = slice(c0, c0 + cwid)
        val = _dot(hn, wv_ref[:, cs])
        gate = _dot(hn, wg_ref[:, cs])
        cw = cw_ref[:, cs]
        cb = cb_ref[:, cs]
        for s in range(tm // seg):
            g0 = gate[s * seg:(s + 1) * seg]
            prev = st_ref[s, :, cs] if tiles_per_seq == 1 else prev_ref[:, cs]
            ext = jnp.concatenate([prev, g0], axis=0)
            g1 = pltpu.roll(ext, 1, 0)[SUBLANES:]
            g2 = pltpu.roll(ext, 2, 0)[SUBLANES:]
            c = cb + cw[0:1] * g2 + cw[1:2] * g1 + cw[2:3] * g0
            act_ref[s * seg:(s + 1) * seg, cs] = (jax.nn.silu(c) * val[s * seg:(s + 1) * seg]).astype(BF16)
            last = g0[seg - SUBLANES:seg]
            stout_ref[s, :, cs] = last
            if tiles_per_seq > 1:
                prev_ref[:, cs] = last


def _ffn_up(hn, w_in, layer, conv_w, conv_b, conv_state, seq_len):
    n, d = hn.shape
    f = w_in.shape[2] // 2
    nseq = n // seq_len
    tf = max(t for t in range(MXU_DIM, f + 1, MXU_DIM) if f % t == 0 and 4 * d * t <= FFN_WEIGHT_BYTES)
    tm = _tile(n, 512)
    nf = f // tf
    wspec = lambda col: pl.BlockSpec((None, d, tf), lambda j, i: (layer, 0, col(j)), pipeline_mode=pl.Buffered(1))
    if tm >= seq_len:
        seg, tps, spt = seq_len, 1, tm // seq_len
        st_idx = lambda j, i: (i, 0, j)
    else:
        seg, tps, spt = tm, seq_len // tm, 1
        st_idx = lambda j, i: (i // tps, 0, j)
    assert seg % SUBLANES == 0 and seg >= SUBLANES
    st = jnp.pad(conv_state, ((0, 0), (SUBLANES - (CONV_W - 1), 0), (0, 0)))
    cw = jnp.pad(conv_w, ((0, SUBLANES - CONV_W), (0, 0)))
    cb = conv_b.reshape(1, f)
    kern = functools.partial(_ffn_up_kernel, seg=seg, tiles_per_seq=tps)
    act, st_out = pl.pallas_call(
        kern,
        grid=(nf, n // tm),
        in_specs=[pl.BlockSpec((tm, d), lambda j, i: (i, 0)),
                  wspec(lambda j: j), wspec(lambda j: nf + j),
                  pl.BlockSpec((SUBLANES, tf), lambda j, i: (0, j)),
                  pl.BlockSpec((1, tf), lambda j, i: (0, j)),
                  pl.BlockSpec((spt, SUBLANES, tf), st_idx)],
        out_specs=[pl.BlockSpec((tm, tf), lambda j, i: (i, j)),
                   pl.BlockSpec((spt, SUBLANES, tf), st_idx)],
        out_shape=[jax.ShapeDtypeStruct((n, f), BF16), jax.ShapeDtypeStruct((nseq, SUBLANES, f), F32)],
        scratch_shapes=[pltpu.VMEM((SUBLANES, tf), F32)],
        compiler_params=_cparams(2),
        name="ffn_up",
    )(hn, w_in, w_in, cw, cb, st)
    return act, st_out[:, SUBLANES - (CONV_W - 1):, :]


def _rope_pair(x, cos, sin):
    return x * cos + pltpu.roll(x, MLA_ROPE, 1) * sin


def _odd_in_kernel(hn_ref, win_ref, gq_ref, gkv_ref, wuq_ref, cos_ref, sin_ref, *rest,
                   qr, kvr, heads, qscale, expand):
    if expand:
        wk_ref, wvt_ref, q_ref, ckv_ref, ckvb_ref, kpe_ref, kpeb_ref, kn_ref, vt_ref = rest
    else:
        q_ref, ckv_ref, ckvb_ref, kpe_ref, kpeb_ref = rest
    hn = hn_ref[...]
    cos = cos_ref[...]
    sin = sin_ref[...]
    cq = _rms(_dot(hn, win_ref[:, 0:qr]), gq_ref[...]).astype(BF16)
    ckv = _rms(_dot(hn, win_ref[:, qr:qr + kvr]), gkv_ref[...])
    kpe = _rope_pair(_dot(hn, win_ref[:, qr + kvr:qr + kvr + LANES]), cos, sin)
    ckv_ref[...] = ckv
    ckvb = ckv.astype(BF16)
    ckvb_ref[...] = ckvb
    kpe_ref[...] = kpe[:, 0:MLA_ROPE]
    kpeb_ref[...] = kpe.astype(BF16)
    hw = MXU_DIM
    for h in range(heads):
        qh = _dot(cq, wuq_ref[:, h * hw:(h + 1) * hw]) * qscale
        q_ref[:, h * hw:h * hw + LANES] = qh[:, 0:LANES].astype(BF16)
        q_ref[:, h * hw + LANES:(h + 1) * hw] = _rope_pair(qh[:, LANES:hw], cos, sin).astype(BF16)
    if expand:
        kn_ref[...] = _dot(ckvb, wk_ref[...]).astype(BF16)
        vt_ref[...] = _dot_t(wvt_ref[...], ckvb).astype(BF16)


def _odd_in(hn, win, gq, gkv, wuq, cos, sin, wkv, seq_len, heads):
    n, d = hn.shape
    qr, kvr = gq.shape[1], gkv.shape[1]
    tm = _tile(n, 256)
    row = lambda i: (i, 0)
    if tm <= seq_len:
        tps = seq_len // tm
        pos = lambda i: (i % tps, 0)
    else:
        cos = jnp.tile(cos, (tm // seq_len, 1))
        sin = jnp.tile(sin, (tm // seq_len, 1))
        pos = lambda i: (0, 0)
    expand = wkv is not None
    kern = functools.partial(_odd_in_kernel, qr=qr, kvr=kvr, heads=heads,
                             qscale=(MLA_NOPE + MLA_ROPE) ** -0.5 * LOG2E, expand=expand)
    in_specs = [pl.BlockSpec((tm, d), row), _resident(win.shape), _resident(gq.shape), _resident(gkv.shape),
                _resident(wuq.shape), pl.BlockSpec((tm, LANES), pos), pl.BlockSpec((tm, LANES), pos)]
    args = [hn, win, gq, gkv, wuq, cos, sin]
    out_specs = [pl.BlockSpec((tm, heads * MXU_DIM), row), pl.BlockSpec((tm, kvr), row),
                 pl.BlockSpec((tm, kvr), row), pl.BlockSpec((tm, MLA_ROPE), row), pl.BlockSpec((tm, LANES), row)]
    out_shape = [jax.ShapeDtypeStruct((n, heads * MXU_DIM), BF16), jax.ShapeDtypeStruct((n, kvr), F32),
                 jax.ShapeDtypeStruct((n, kvr), BF16), jax.ShapeDtypeStruct((n, MLA_ROPE), F32),
                 jax.ShapeDtypeStruct((n, LANES), BF16)]
    if expand:
        in_specs += [_resident(wkv[0].shape), _resident(wkv[1].shape)]
        args += list(wkv)
        out_specs += [pl.BlockSpec((tm, heads * MLA_NOPE), row),
                      pl.BlockSpec((heads * MLA_V, tm), lambda i: (0, i))]
        out_shape += [jax.ShapeDtypeStruct((n, heads * MLA_NOPE), BF16),
                      jax.ShapeDtypeStruct((heads * MLA_V, n), BF16)]
    return pl.pallas_call(
        kern, grid=(n // tm,), in_specs=in_specs, out_specs=out_specs, out_shape=out_shape,
        compiler_params=_cparams(1), name="odd_in",
    )(*args)


def _mla_prompt_kernel(q_ref, kn_ref, kpe_ref, vt_ref, o_ref, kx_ref, vxt_ref, m_ref, a_ref, *, tq, hp):
    i = pl.program_id(2)
    hw = MXU_DIM

    @pl.when(i == 0)
    def _():
        for j in range(hp):
            kx_ref[j, :, 0:MLA_NOPE] = kn_ref[:, j * MLA_NOPE:(j + 1) * MLA_NOPE]
            kx_ref[j, :, MLA_NOPE:hw] = kpe_ref[...]
            for kt in range(vxt_ref.shape[1]):
                vxt_ref[j, kt, 0:MLA_V, :] = vt_ref[j * MLA_V:(j + 1) * MLA_V, kt * tq:(kt + 1) * tq]
                vxt_ref[j, kt, MLA_V:VX_ROWS, :] = _unit_rows(tq)

    m_ref[...] = jnp.full(m_ref.shape, NEG_INF, F32)
    a_ref[...] = jnp.zeros(a_ref.shape, F32)

    def step(kt, mask):
        sl = pl.ds(pl.multiple_of(kt * tq, tq), tq)
        _flash_update([(kx_ref[j, sl, :], q_ref[:, j * hw:(j + 1) * hw], vxt_ref[j, kt], m_ref.at[j], a_ref.at[j])
                       for j in range(hp)], mask)

    def body(kt, carry):
        step(kt, None)
        return carry

    lax.fori_loop(0, i, body, 0)
    step(i, _chunk_mask_t(tq, tq))
    for j in range(hp):
        a = a_ref[j]
        o_ref[:, j * MLA_V:(j + 1) * MLA_V] = (a[0:MLA_V] / a[MLA_V:MLA_V + 1]).T.astype(BF16)


def _mla_prompt(q, kn, kpeb, vt, nbatch, seq_len, heads):
    n = q.shape[0]
    tq = _tile(seq_len, 512)
    hp = 4
    assert tq % CHUNK == 0 and heads % hp == 0 and MLA_V == LANES
    nq = seq_len // tq
    kern = functools.partial(_mla_prompt_kernel, tq=tq, hp=hp)
    return pl.pallas_call(
        kern,
        grid=(nbatch, heads // hp, nq),
        in_specs=[pl.BlockSpec((tq, hp * MXU_DIM), lambda b, h, i: (b * nq + i, h)),
                  pl.BlockSpec((seq_len, hp * MLA_NOPE), lambda b, h, i: (b, h)),
                  pl.BlockSpec((seq_len, LANES), lambda b, h, i: (b, 0)),
                  pl.BlockSpec((hp * MLA_V, seq_len), lambda b, h, i: (h, b))],
        out_specs=pl.BlockSpec((tq, hp * MLA_V), lambda b, h, i: (b * nq + i, h)),
        out_shape=jax.ShapeDtypeStruct((n, heads * MLA_V), BF16),
        scratch_shapes=[pltpu.VMEM((hp, seq_len, MXU_DIM), BF16),
                        pltpu.VMEM((hp, nq, VX_ROWS, tq), BF16),
                        pltpu.VMEM((hp, 1, tq), F32), pltpu.VMEM((hp, VX_ROWS, tq), F32)],
        compiler_params=_cparams(3),
        name="mla_attn_prompt",
    )(q, kn, kpeb, vt)


def _mla_sample_kernel(q_ref, cc_ref, pc_ref, cn_ref, pn_ref, wk_ref, wv_ref, o_ref, ql_ref, qp_ref,
                       *, heads, past):
    tq = q_ref.shape[0]
    hw = MXU_DIM
    for h in range(heads):
        qn = q_ref[:, h * hw:h * hw + MLA_NOPE]
        ql_ref[h * tq:(h + 1) * tq, :] = _dot(qn, wk_ref[h]).astype(BF16)
        qp_ref[h * tq:(h + 1) * tq, :] = q_ref[:, h * hw + MLA_NOPE:(h + 1) * hw]
    ql = ql_ref[...]
    qp = qp_ref[...]
    cc = cc_ref[...].astype(BF16)
    pc = pc_ref[...].astype(BF16)
    cn = cn_ref[...]
    mask_n = jnp.concatenate([_chunk_mask(past, past, tq, tq)] * heads, axis=0)
    sc = _dot_t(ql, cc) + _dot_t(qp[:, 0:MLA_ROPE], pc)
    sn = jnp.where(mask_n, _dot_t(ql, cn) + _dot_t(qp, pn_ref[...]), NEG_INF)
    m = jnp.maximum(jnp.max(sc, axis=-1, keepdims=True), jnp.max(sn, axis=-1, keepdims=True))
    ec = jnp.exp2(sc - m)
    en = jnp.exp2(sn - m)
    l = jnp.sum(ec, axis=-1, keepdims=True) + jnp.sum(en, axis=-1, keepdims=True)
    ol = ((_dot(ec.astype(BF16), cc) + _dot(en.astype(BF16), cn)) / l).astype(BF16)
    for h in range(heads):
        o_ref[:, h * MLA_V:(h + 1) * MLA_V] = _dot(ol[h * tq:(h + 1) * tq], wv_ref[h]).astype(BF16)


def _mla_sample(q, ckvb, kpeb, cache_ckv, cache_kpe, wk_t, wv, nbatch, seq_len, heads):
    n = q.shape[0]
    past, kvr = cache_ckv.shape[1], cache_ckv.shape[2]
    assert (past // CHUNK) * CHUNK == past
    cc = cache_ckv.reshape(nbatch * past, kvr)
    pc = cache_kpe.reshape(nbatch * past, MLA_ROPE)
    kern = functools.partial(_mla_sample_kernel, heads=heads, past=past)
    row = lambda b: (b, 0)
    return pl.pallas_call(
        kern,
        grid=(nbatch,),
        in_specs=[pl.BlockSpec((seq_len, heads * MXU_DIM), row),
                  pl.BlockSpec((past, kvr), row), pl.BlockSpec((past, MLA_ROPE), row),
                  pl.BlockSpec((seq_len, kvr), row), pl.BlockSpec((seq_len, LANES), row),
                  _resident(wk_t.shape), _resident(wv.shape)],
        out_specs=pl.BlockSpec((seq_len, heads * MLA_V), row),
        out_shape=jax.ShapeDtypeStruct((n, heads * MLA_V), BF16),
        scratch_shapes=[pltpu.VMEM((heads * seq_len, kvr), BF16), pltpu.VMEM((heads * seq_len, LANES), BF16)],
        compiler_params=_cparams(1),
        name="mla_attn_sample",
    )(q, cc, pc, ckvb, kpeb, wk_t, wv)


def _rope_tables(pos):
    half = MLA_ROPE // 2
    inv = ROPE_BASE ** (-jnp.arange(half, dtype=F32) / half)
    ang = pos.astype(F32)[:, None] * inv[None, :]
    cos, sin = jnp.cos(ang), jnp.sin(ang)
    zero = jnp.zeros_like(cos)
    return (jnp.concatenate([cos, cos, zero, zero], axis=1),
            jnp.concatenate([-sin, sin, zero, zero], axis=1))


def _swap_halves(w):
    half = MLA_ROPE // 2
    return jnp.concatenate([w[..., half:], w[..., :half]], axis=-1)


def _prepare_weights(p):
    w = {}
    heads = p['mla_w_uq'].shape[2]
    w['heads'] = heads
    w['norm_mix'] = p['norm_mix'][:, None, :]
    w['norm_ffn'] = p['norm_ffn'][:, None, :]
    w['norm_final'] = p['norm_final'][None, :]
    w['w_in_even'] = p['w_in_even'][0].astype(BF16)
    w['w_out_even'] = p['w_out_even'][0].astype(BF16)
    w['s5'] = dict(a_re=p['s5_a_re'][0], a_im=p['s5_a_im'][0], b_re=p['s5_b_re'][0], b_im=p['s5_b_im'][0],
                   c_re=p['s5_c_re'][0], c_im=p['s5_c_im'][0], d=p['s5_d'][0], log_dt=p['s5_log_dt'][0],
                   w_glu=p['s5_w_glu'][0], b_glu=p['s5_b_glu'][0])
    w['lam'] = [p[k][0][None, :] for k in ('diff_lambda_q1', 'diff_lambda_k1', 'diff_lambda_q2', 'diff_lambda_k2')]
    w['subln'] = p['diff_subln'][0][None, :]
    wi = p['w_in_odd'][0]
    qr = p['mla_q_norm'].shape[1]
    kvr = p['mla_kv_norm'].shape[1]
    wpe = wi[:, qr + kvr:]
    w['w_in_odd'] = jnp.concatenate([wi[:, :qr + kvr], wpe, _swap_halves(wpe)], axis=1).astype(BF16)
    w['gq'] = p['mla_q_norm'][0][None, :]
    w['gkv'] = p['mla_kv_norm'][0][None, :]
    wuq = p['mla_w_uq'][0]
    wuq = jnp.concatenate([wuq, _swap_halves(wuq[..., MLA_NOPE:])], axis=-1)
    w['w_uq'] = wuq.reshape(qr, heads * MXU_DIM).astype(BF16)
    wukv = p['mla_w_ukv'][0]
    w['w_kv'] = (wukv[..., :MLA_NOPE].reshape(kvr, heads * MLA_NOPE).astype(BF16),
                 wukv[..., MLA_NOPE:].reshape(kvr, heads * MLA_V).T.astype(BF16))
    w['w_uk_t'] = jnp.transpose(wukv[..., :MLA_NOPE], (1, 2, 0)).astype(BF16)
    w['w_uv'] = jnp.transpose(wukv[..., MLA_NOPE:], (1, 0, 2)).astype(BF16)
    w['w_out_odd'] = p['w_out_odd'][0].astype(BF16)
    w['ffn_w_in'] = p['ffn_w_in'].astype(BF16)
    w['ffn_w_down'] = p['ffn_w_down'].astype(BF16)
    w['ffn_conv_w'] = p['ffn_conv_w']
    w['ffn_conv_b'] = p['ffn_conv_b']
    return w


def _trunk(x3, s5_re0, s5_im0, k_past, v_past, ckv_past, kpe_past, conv0, pos0, w):
    nb, t, d = x3.shape
    n = nb * t
    x = x3.reshape(n, d)
    heads = w['heads']
    s5w = w['s5']['d'].size
    dw = (w['w_in_even'].shape[1] - s5w) // 3
    dheads = dw // LANES

    lam_init = 0.8 - 0.6 * math.exp(-0.3 * 0)
    u, q, k, v, kb, vb, vbt = _even_in(x, w['norm_mix'][0], w['w_in_even'], s5w, dw)
    y_s5, ht_re, ht_im = _s5(u, s5_re0, s5_im0, t, w['s5'])
    if k_past is None:
        o = _diff_prompt(q, kb, vbt, w['lam'], w['subln'], nb, t, lam_init)
    else:
        o = _diff_sample(q, kb, vb, k_past, v_past, w['lam'], w['subln'], nb, t, lam_init)
    x, hn = _out_proj([y_s5, o], w['w_out_even'], x, w['norm_ffn'][0])
    act, conv_a = _ffn_up(hn, w['ffn_w_in'], 0, w['ffn_conv_w'][0], w['ffn_conv_b'][0], conv0[0], t)
    x, hn = _out_proj([act], w['ffn_w_down'], x, w['norm_mix'][1], tm_pref=256, layer=0)

    cos, sin = _rope_tables(pos0 + jnp.arange(t, dtype=jnp.int32))
    if ckv_past is None:
        qm, ckv, ckvb, kpe, kpeb, kn, vm = _odd_in(hn, w['w_in_odd'], w['gq'], w['gkv'], w['w_uq'], cos, sin,
                                                   w['w_kv'], t, heads)
        om = _mla_prompt(qm, kn, kpeb, vm, nb, t, heads)
    else:
        qm, ckv, ckvb, kpe, kpeb = _odd_in(hn, w['w_in_odd'], w['gq'], w['gkv'], w['w_uq'], cos, sin,
                                           None, t, heads)
        om = _mla_sample(qm, ckvb, kpeb, ckv_past, kpe_past, w['w_uk_t'], w['w_uv'], nb, t, heads)
    x, hn = _out_proj([om], w['w_out_odd'], x, w['norm_ffn'][1])
    act, conv_b = _ffn_up(hn, w['ffn_w_in'], 1, w['ffn_conv_w'][1], w['ffn_conv_b'][1], conv0[1], t)
    (y,) = _out_proj([act], w['ffn_w_down'], x, w['norm_final'], final=True, tm_pref=256, layer=1)

    groups, n_state = w['s5']['a_re'].shape
    return (y.reshape(nb, t, d), ht_re[None], ht_im[None],
            k.reshape(1, nb, t, dheads, LANES), v.reshape(1, nb, t, dheads, LANES),
            ckv.reshape(1, nb, t, -1), kpe.reshape(1, nb, t, MLA_ROPE), jnp.stack([conv_a, conv_b]))


def kernel(x_prompt, x_sample, state_s5_re, state_s5_im, cache_diff_k, cache_diff_v, cache_mla_ckv, cache_mla_kpe, state_ffn_conv, norm_mix, norm_ffn, norm_final, w_in_even, w_out_even, s5_a_re, s5_a_im, s5_b_re, s5_b_im, s5_c_re, s5_c_im, s5_d, s5_log_dt, s5_w_glu, s5_b_glu, diff_lambda_q1, diff_lambda_k1, diff_lambda_q2, diff_lambda_k2, diff_subln, w_in_odd, mla_q_norm, mla_kv_norm, mla_w_uq, mla_w_ukv, w_out_odd, ffn_w_in, ffn_conv_w, ffn_conv_b, ffn_w_down):
    w = _prepare_weights(dict(
        norm_mix=norm_mix, norm_ffn=norm_ffn, norm_final=norm_final, w_in_even=w_in_even, w_out_even=w_out_even,
        s5_a_re=s5_a_re, s5_a_im=s5_a_im, s5_b_re=s5_b_re, s5_b_im=s5_b_im, s5_c_re=s5_c_re, s5_c_im=s5_c_im,
        s5_d=s5_d, s5_log_dt=s5_log_dt, s5_w_glu=s5_w_glu, s5_b_glu=s5_b_glu,
        diff_lambda_q1=diff_lambda_q1, diff_lambda_k1=diff_lambda_k1, diff_lambda_q2=diff_lambda_q2,
        diff_lambda_k2=diff_lambda_k2, diff_subln=diff_subln, w_in_odd=w_in_odd, mla_q_norm=mla_q_norm,
        mla_kv_norm=mla_kv_norm, mla_w_uq=mla_w_uq, mla_w_ukv=mla_w_ukv, w_out_odd=w_out_odd,
        ffn_w_in=ffn_w_in, ffn_conv_w=ffn_conv_w, ffn_conv_b=ffn_conv_b, ffn_w_down=ffn_w_down))
    nb_p = x_prompt.shape[0]
    groups, n_state = s5_a_re.shape[1:]
    d_ff = ffn_conv_b.shape[1]
    depth = ffn_conv_b.shape[0]
    s5_zero = jnp.zeros((nb_p, groups, n_state), F32)
    conv_zero = jnp.zeros((depth, nb_p, CONV_W - 1, d_ff), F32)
    (y_p, re_p, im_p, k_p, v_p, ckv_p, kpe_p, conv_p) = _trunk(
        x_prompt, s5_zero, s5_zero, None, None, None, None, conv_zero, 0, w)
    past = cache_diff_k.shape[2]
    (y_s, re_s, im_s, k_s, v_s, ckv_s, kpe_s, conv_s) = _trunk(
        x_sample, state_s5_re[0], state_s5_im[0], cache_diff_k[0], cache_diff_v[0], cache_mla_ckv[0],
        cache_mla_kpe[0], state_ffn_conv, past, w)
    return (y_p, y_s, re_p, im_p, re_s, im_s, k_p, v_p, k_s, v_s, ckv_p, kpe_p, ckv_s, kpe_s, conv_p, conv_s)
```

```python
import functools
import math

import jax
import jax.numpy as jnp
from jax import lax
from jax.experimental import pallas as pl
from jax.experimental.pallas import tpu as pltpu

F32 = jnp.float32
BF16 = jnp.bfloat16

CHUNK = 64
EPS = 1e-6
NEG_INF = -1e30
ROPE_BASE = 10000.0
S5_GROUP = 16
DIFF_DK = 64
MLA_NOPE = 128
MLA_ROPE = 64
MLA_V = 128
CONV_W = 3
LOG2E = 1.4426950408889634

LANES = 128
SUBLANES = 8
MXU_DIM = 256
VMEM_LIMIT_BYTES = 56 * 1024 * 1024

FFN_WEIGHT_BYTES = 24 * 1024 * 1024
FLASH_COLS = 256
VX_ROWS = LANES + 16
S5_L = 16
S5_GB = MXU_DIM // S5_GROUP


def _cparams(n_axes):
    return pltpu.CompilerParams(
        dimension_semantics=("arbitrary",) * n_axes,
        vmem_limit_bytes=VMEM_LIMIT_BYTES)


def _resident(shape):
    nd = len(shape)
    return pl.BlockSpec(shape, lambda *_: (0,) * nd, pipeline_mode=pl.Buffered(1))


def _tile(n, pref):
    t = min(n, pref)
    while n % t:
        t //= 2
    return t


def _dot(a, b):
    return jnp.dot(a, b, preferred_element_type=F32)


def _dot_t(a, b):
    return lax.dot_general(a, b, (((1,), (1,)), ((), ())), preferred_element_type=F32)


def _rms(x, g):
    ms = jnp.mean(x * x, axis=-1, keepdims=True)
    return x * lax.rsqrt(ms + EPS) * g


def _even_in_kernel(x_ref, g_ref, w_ref, u_ref, q_ref, k_ref, v_ref, kb_ref, vb_ref, vbt_ref,
                    *, s5w, dw, qscale):
    xn = _rms(x_ref[...], g_ref[...]).astype(BF16)
    u_ref[...] = _dot(xn, w_ref[:, 0:s5w])
    q_ref[...] = (_dot(xn, w_ref[:, s5w:s5w + dw]) * qscale).astype(BF16)
    k = _dot(xn, w_ref[:, s5w + dw:s5w + 2 * dw])
    k_ref[...] = k
    kb_ref[...] = k.astype(BF16)
    v = _dot(xn, w_ref[:, s5w + 2 * dw:s5w + 3 * dw])
    v_ref[...] = v
    vb_ref[...] = v.astype(BF16)
    vbt_ref[...] = v.T.astype(BF16)


def _even_in(x, g, w, s5w, dw):
    n, d = x.shape
    tm = _tile(n, 512)
    row = lambda i: (i, 0)
    kern = functools.partial(_even_in_kernel, s5w=s5w, dw=dw, qscale=DIFF_DK ** -0.5 * LOG2E)
    return pl.pallas_call(
        kern,
        grid=(n // tm,),
        in_specs=[pl.BlockSpec((tm, d), row), _resident(g.shape), _resident(w.shape)],
        out_specs=[pl.BlockSpec((tm, s5w), row), pl.BlockSpec((tm, dw), row), pl.BlockSpec((tm, dw), row),
                   pl.BlockSpec((tm, dw), row), pl.BlockSpec((tm, dw), row), pl.BlockSpec((tm, dw), row),
                   pl.BlockSpec((dw, tm), lambda i: (0, i))],
        out_shape=[jax.ShapeDtypeStruct((n, s5w), F32), jax.ShapeDtypeStruct((n, dw), BF16),
                   jax.ShapeDtypeStruct((n, dw), F32), jax.ShapeDtypeStruct((n, dw), F32),
                   jax.ShapeDtypeStruct((n, dw), BF16), jax.ShapeDtypeStruct((n, dw), BF16),
                   jax.ShapeDtypeStruct((dw, n), BF16)],
        compiler_params=_cparams(1),
        name="even_in",
    )(x, g, w)


def _s5_kernel(u_ref, h0_ref, lre_ref, lim_ref, dt_ref, bre_ref, bim_ref, cbd_ref, d_ref, wglu_ref, bglu_ref,
               y_ref, ht_ref,
               a_ref, a16_ref, bbd_ref, us_ref, xs_ref, xe_ref, hs_ref, carry_ref, ys_ref, yn_ref,
               *, rows, width, cps, ngb):
    t = pl.program_id(0)
    nblk = ngb * 16
    half = 8
    gw = MXU_DIM

    @pl.when(t == 0)
    def _prepare():
        carry_ref[...] = jnp.zeros_like(carry_ref)
        for gb in range(ngb):
            for k in range(half):
                lre = jnp.minimum(lre_ref[gb * half + k], -1e-4)
                lim = lim_ref[gb * half + k]
                dt = jnp.exp(dt_ref[gb * half + k])
                mag = jnp.exp(lre * dt)
                are = mag * jnp.cos(lim * dt)
                aim = mag * jnp.sin(lim * dt)
                den = lre * lre + lim * lim
                cre = ((are - 1.0) * lre + aim * lim) / den
                cim = (aim * lre - (are - 1.0) * lim) / den
                ire, iim = gb * 16 + k, gb * 16 + half + k
                a_ref[ire] = are
                a_ref[iim] = aim
                pre, pim = are, aim
                for _ in range(4):
                    pre, pim = pre * pre - pim * pim, 2.0 * pre * pim
                a16_ref[ire] = pre
                a16_ref[iim] = pim
                br = bre_ref[gb, :, k * LANES:(k + 1) * LANES]
                bi = bim_ref[gb, :, k * LANES:(k + 1) * LANES]
                bbd_ref[gb, :, k * LANES:(k + 1) * LANES] = (cre * br - cim * bi).astype(BF16)
                bbd_ref[gb, :, (half + k) * LANES:(half + k + 1) * LANES] = (cre * bi + cim * br).astype(BF16)

    ncb = width // LANES
    for cb in range(ncb):
        us_ref[cb] = u_ref[:, cb * LANES:(cb + 1) * LANES]

    def step(l):
        return pl.ds(l * rows, rows)

    def at_step(l):
        return pl.ds(l, rows, stride=S5_L)

    for gb in range(ngb):
        cbs = range(gb * gw // LANES, (gb + 1) * gw // LANES)
        lhs = jnp.concatenate(
            [jnp.concatenate([us_ref[cb, at_step(l), :] for cb in cbs], axis=1) for l in range(S5_L)], axis=0)
        x = _dot(lhs.astype(BF16), bbd_ref[gb])
        for j in range(16):
            xs_ref[gb * 16 + j] = x[:, j * LANES:(j + 1) * LANES]
        for k in range(half):
            ire, iim = gb * 16 + k, gb * 16 + half + k
            are, aim = a_ref[ire], a_ref[iim]
            hre = xs_ref[ire, step(0), :]
            him = xs_ref[iim, step(0), :]
            for l in range(1, S5_L):
                xre = xs_ref[ire, step(l), :]
                xim = xs_ref[iim, step(l), :]
                hre, him = are * hre - aim * him + xre, are * him + aim * hre + xim
            xe_ref[ire] = hre
            xe_ref[iim] = him

    base_row = t * rows
    for grp in range(ngb * half // 8):
        ire0 = (grp // (half // 8)) * 16 + (grp % (half // 8)) * 8
        iim0 = ire0 + half
        a16re = a16_ref[ire0:ire0 + 8]
        a16im = a16_ref[iim0:iim0 + 8]

        def scan_body(r, carry, ire0=ire0, iim0=iim0, a16re=a16re, a16im=a16im):
            cre, cim = carry
            gr = base_row + r
            seq = gr // cps - (base_row // cps)
            is_start = (gr % cps) == 0
            cre = jnp.where(is_start, h0_ref[seq, ire0:ire0 + 8], cre)
            cim = jnp.where(is_start, h0_ref[seq, iim0:iim0 + 8], cim)
            hs_ref[ire0:ire0 + 8, pl.ds(r, 1), :] = cre
            hs_ref[iim0:iim0 + 8, pl.ds(r, 1), :] = cim
            xre = xe_ref[ire0:ire0 + 8, pl.ds(r, 1), :]
            xim = xe_ref[iim0:iim0 + 8, pl.ds(r, 1), :]
            nre = a16re * cre - a16im * cim + xre
            nim = a16re * cim + a16im * cre + xim

            @pl.when((gr % cps) == cps - 1)
            def _():
                ht_ref[seq, ire0:ire0 + 8] = nre
                ht_ref[seq, iim0:iim0 + 8] = nim

            return nre, nim

        cre, cim = lax.fori_loop(0, rows, scan_body, (carry_ref[ire0:ire0 + 8], carry_ref[iim0:iim0 + 8]))
        carry_ref[ire0:ire0 + 8] = cre
        carry_ref[iim0:iim0 + 8] = cim

    for gb in range(ngb):
        for k in range(half):
            ire, iim = gb * 16 + k, gb * 16 + half + k
            are, aim = a_ref[ire], a_ref[iim]
            hre = hs_ref[ire]
            him = hs_ref[iim]
            for l in range(S5_L):
                xre = xs_ref[ire, step(l), :]
                xim = xs_ref[iim, step(l), :]
                hre, him = are * hre - aim * him + xre, are * him + aim * hre + xim
                xs_ref[ire, step(l), :] = hre
                xs_ref[iim, step(l), :] = him
        h = jnp.concatenate([xs_ref[gb * 16 + j].astype(BF16) for j in range(16)], axis=1)
        yc = _dot(h, cbd_ref[gb])
        for c in range(gw // LANES):
            ys_ref[gb * gw // LANES + c] = yc[:, c * LANES:(c + 1) * LANES]

    for cb in range(ncb):
        for r in range(rows):
            yn_ref[r * S5_L:(r + 1) * S5_L, cb * LANES:(cb + 1) * LANES] = ys_ref[cb, pl.ds(r, S5_L, stride=rows), :]

    y = yn_ref[...] + d_ref[...] * u_ref[...]
    z = jax.nn.gelu(y)
    gate = jax.nn.sigmoid(_dot(z.astype(BF16), wglu_ref[...]) + bglu_ref[...])
    y_ref[...] = (z * gate).astype(BF16)


def _s5_layout(p, ngb, n_state):
    return p.reshape(ngb * (S5_GB * n_state // LANES), 1, LANES)


def _s5_state_to_blocks(s, ngb):
    b = s.shape[0]
    return s.reshape(b, ngb, 8, 1, LANES)


def _s5(u, h0_re, h0_im, seq_len, w):
    n, width = u.shape
    groups, n_state = w['a_re'].shape
    assert n_state * S5_GB == 8 * LANES and width == groups * S5_GROUP
    ngb = groups // S5_GB
    nseq = n // seq_len
    cps = seq_len // S5_L
    nrow = n // S5_L
    rows = _tile(nrow, 32)
    assert rows % SUBLANES == 0 and (cps % rows == 0 or rows % cps == 0)
    spt = max(1, rows // cps)
    tps = max(1, cps // rows)
    nblk = ngb * 16
    sw = S5_GB * n_state

    h0 = jnp.concatenate([_s5_state_to_blocks(h0_re, ngb), _s5_state_to_blocks(h0_im, ngb)], axis=2)
    h0 = h0.reshape(nseq, nblk, 1, LANES)
    lre = _s5_layout(w['a_re'], ngb, n_state)
    lim = _s5_layout(w['a_im'], ngb, n_state)
    dt = _s5_layout(jnp.broadcast_to(w['log_dt'][:, None], (groups, n_state)), ngb, n_state)
    eye = jnp.eye(S5_GB, dtype=F32)

    def bdiag_b(b):
        bb = b.reshape(ngb, S5_GB, n_state, S5_GROUP)
        return jnp.einsum('agnp,gh->agphn', bb, eye).reshape(ngb, S5_GB * S5_GROUP, sw)

    def bdiag_c(c):
        cc = c.reshape(ngb, S5_GB, S5_GROUP, n_state)
        return jnp.einsum('agpn,gh->ahngp', cc, eye).reshape(ngb, sw, S5_GB * S5_GROUP)

    bre = bdiag_b(w['b_re'])
    bim = bdiag_b(w['b_im'])
    cbd = jnp.concatenate([bdiag_c(w['c_re']), -bdiag_c(w['c_im'])], axis=1).astype(BF16)
    d = w['d'].reshape(1, width)
    wglu = w['w_glu'].astype(BF16)
    bglu = w['b_glu'].reshape(1, width)

    seq_idx = (lambda t: (t // tps, 0, 0, 0)) if tps > 1 else (lambda t: (t, 0, 0, 0))
    kern = functools.partial(_s5_kernel, rows=rows, width=width, cps=cps, ngb=ngb)
    y, ht = pl.pallas_call(
        kern,
        grid=(nrow // rows,),
        in_specs=[pl.BlockSpec((S5_L * rows, width), lambda t: (t, 0)),
                  pl.BlockSpec((spt, nblk, 1, LANES), seq_idx),
                  _resident(lre.shape), _resident(lim.shape), _resident(dt.shape),
                  _resident(bre.shape), _resident(bim.shape), _resident(cbd.shape),
                  _resident(d.shape), _resident(wglu.shape), _resident(bglu.shape)],
        out_specs=[pl.BlockSpec((S5_L * rows, width), lambda t: (t, 0)),
                   pl.BlockSpec((spt, nblk, 1, LANES), seq_idx)],
        out_shape=[jax.ShapeDtypeStruct((n, width), BF16),
                   jax.ShapeDtypeStruct((nseq, nblk, 1, LANES), F32)],
        scratch_shapes=[pltpu.VMEM((nblk, 1, LANES), F32),
                        pltpu.VMEM((nblk, 1, LANES), F32),
                        pltpu.VMEM((ngb, MXU_DIM, 2 * sw), BF16),
                        pltpu.VMEM((width // LANES, S5_L * rows, LANES), F32),
                        pltpu.VMEM((nblk, S5_L * rows, LANES), F32),
                        pltpu.VMEM((nblk, rows, LANES), F32),
                        pltpu.VMEM((nblk, rows, LANES), F32),
                        pltpu.VMEM((nblk, 1, LANES), F32),
                        pltpu.VMEM((width // LANES, S5_L * rows, LANES), F32),
                        pltpu.VMEM((S5_L * rows, width), F32)],
        compiler_params=_cparams(1),
        name="s5_mix",
    )(u, h0, lre, lim, dt, bre, bim, cbd, d, wglu, bglu)
    ht = ht.reshape(nseq, ngb, 2, groups // ngb, n_state)
    ht_re = ht[:, :, 0].reshape(nseq, groups, n_state)
    ht_im = ht[:, :, 1].reshape(nseq, groups, n_state)
    return y, ht_re, ht_im


def _diff_lambda(lq1_ref, lk1_ref, lq2_ref, lk2_ref, lam_init):
    s1 = jnp.sum(lq1_ref[...] * lk1_ref[...], axis=-1, keepdims=True)
    s2 = jnp.sum(lq2_ref[...] * lk2_ref[...], axis=-1, keepdims=True)
    return jnp.exp(s1) - jnp.exp(s2) + lam_init


def _split_maps(q):
    lane = lax.broadcasted_iota(jnp.int32, q.shape, 1)
    zero = jnp.zeros_like(q)
    return jnp.where(lane < DIFF_DK, q, zero), jnp.where(lane >= DIFF_DK, q, zero)


def _chunk_mask(q0, k0, tq, tk):
    qc = (q0 + lax.broadcasted_iota(jnp.int32, (tq, tk), 0)) // CHUNK
    kc = (k0 + lax.broadcasted_iota(jnp.int32, (tq, tk), 1)) // CHUNK
    return kc <= qc


def _subln(o, g, lam_init):
    return (_rms(o, g) * (1.0 - lam_init)).astype(BF16)


def _flash_update(blocks):
    work = []
    for streams, mask in blocks:
        chains = []
        for k, q, vxt, m_ref, a_ref in streams:
            tq = q.shape[0]
            sub = min(tq, FLASH_COLS)
            for c0 in range(0, tq, sub):
                nk = k.shape[0] if mask is None else min(k.shape[0], c0 + sub)
                chains.append((k[0:nk], q, vxt[:, 0:nk], m_ref, a_ref, slice(c0, c0 + sub), nk))
        work.append((chains, [_dot_t(k, q[cs]) for k, q, _, _, _, cs, _ in chains], mask))
    for chains, scores, mask in work:
        probs = []
        for (_, _, _, m_ref, _, cs, nk), s in zip(chains, scores):
            if mask is not None:
                s = jnp.where(mask[0:nk, cs], s, NEG_INF)
            m_prev = m_ref[:, cs]
            m_new = jnp.maximum(m_prev, jnp.max(s, axis=0, keepdims=True))
            m_ref[:, cs] = m_new
            probs.append((jnp.exp2(m_prev - m_new), jnp.exp2(s - m_new).astype(BF16)))
        for (_, _, vxt, _, a_ref, cs, _), (alpha, p) in zip(chains, probs):
            a_ref[:, cs] = alpha * a_ref[:, cs] + _dot(vxt, p)


def _flash_sweep(i, streams_at, tq):
    diag = _chunk_mask_t(tq, tq)

    def pair(p, carry):
        _flash_update([(streams_at(2 * p), None), (streams_at(2 * p + 1), None)])
        return carry

    lax.fori_loop(0, i // 2, pair, 0)
    odd = (i % 2) == 1

    @pl.when(odd)
    def _():
        _flash_update([(streams_at(i - 1), None), (streams_at(i), diag)])

    @pl.when(jnp.logical_not(odd))
    def _():
        _flash_update([(streams_at(i), diag)])


def _chunk_mask_t(tk, tq):
    kc = lax.broadcasted_iota(jnp.int32, (tk, tq), 0) // CHUNK
    qc = lax.broadcasted_iota(jnp.int32, (tk, tq), 1) // CHUNK
    return kc <= qc


def _unit_rows(cols):
    return (lax.broadcasted_iota(jnp.int32, (VX_ROWS - LANES, cols), 0) == 0).astype(BF16)


def _diff_prompt_kernel(q_ref, k_ref, vt_ref, lq1_ref, lk1_ref, lq2_ref, lk2_ref, g_ref, o_ref,
                        vxt_ref, m_ref, a_ref, *, tq, hq, lam_init):
    i = pl.program_id(2)

    @pl.when(i == 0)
    def _():
        for j in range(hq):
            for kt in range(vxt_ref.shape[1]):
                vxt_ref[j, kt, 0:LANES, :] = vt_ref[j * LANES:(j + 1) * LANES, kt * tq:(kt + 1) * tq]
                vxt_ref[j, kt, LANES:VX_ROWS, :] = _unit_rows(tq)

    qs = [_split_maps(q_ref[:, j * LANES:(j + 1) * LANES]) for j in range(hq)]
    m_ref[...] = jnp.full(m_ref.shape, NEG_INF, F32)
    a_ref[...] = jnp.zeros(a_ref.shape, F32)

    def streams_at(kt):
        sl = pl.ds(pl.multiple_of(kt * tq, tq), tq)
        streams = []
        for j in range(hq):
            kb = k_ref[sl, j * LANES:(j + 1) * LANES]
            for mp in range(2):
                streams.append((kb, qs[j][mp], vxt_ref[j, kt], m_ref.at[2 * j + mp], a_ref.at[2 * j + mp]))
        return streams

    _flash_sweep(i, streams_at, tq)

    lam = _diff_lambda(lq1_ref, lk1_ref, lq2_ref, lk2_ref, lam_init)
    for j in range(hq):
        a1 = a_ref[2 * j]
        a2 = a_ref[2 * j + 1]
        ot = a1[0:LANES] / a1[LANES:LANES + 1] - lam * (a2[0:LANES] / a2[LANES:LANES + 1])
        ms = jnp.mean(ot * ot, axis=0, keepdims=True)
        ot = ot * lax.rsqrt(ms + EPS) * g_ref[...] * (1.0 - lam_init)
        o_ref[:, j * LANES:(j + 1) * LANES] = ot.T.astype(BF16)


def _diff_prompt(q, kb, vbt, lam_w, g, nbatch, seq_len, lam_init):
    n, dw = q.shape
    gcol = g.reshape(LANES, 1)
    heads = dw // LANES
    tq = _tile(seq_len, 512)
    hq = 4
    assert tq % CHUNK == 0 and heads % hq == 0
    nq = seq_len // tq
    kern = functools.partial(_diff_prompt_kernel, tq=tq, hq=hq, lam_init=lam_init)
    vec = lambda a: _resident(a.shape)
    return pl.pallas_call(
        kern,
        grid=(nbatch, heads // hq, nq),
        in_specs=[pl.BlockSpec((tq, hq * LANES), lambda b, h, i: (b * nq + i, h)),
                  pl.BlockSpec((seq_len, hq * LANES), lambda b, h, i: (b, h)),
                  pl.BlockSpec((hq * LANES, seq_len), lambda b, h, i: (h, b)),
                  vec(lam_w[0]), vec(lam_w[1]), vec(lam_w[2]), vec(lam_w[3]), vec(gcol)],
        out_specs=pl.BlockSpec((tq, hq * LANES), lambda b, h, i: (b * nq + i, h)),
        out_shape=jax.ShapeDtypeStruct((n, dw), BF16),
        scratch_shapes=[pltpu.VMEM((hq, nq, VX_ROWS, tq), BF16),
                        pltpu.VMEM((2 * hq, 1, tq), F32), pltpu.VMEM((2 * hq, VX_ROWS, tq), F32)],
        compiler_params=_cparams(3),
        name="diff_attn_prompt",
    )(q, kb, vbt, *lam_w, gcol)


def _diff_sample_kernel(q_ref, kc_ref, vc_ref, kn_ref, vn_ref, lq1_ref, lk1_ref, lq2_ref, lk2_ref, g_ref, o_ref,
                        *, past, heads, lam_init):
    tq = q_ref.shape[0]
    mask_n = jnp.concatenate([_chunk_mask(past, past, tq, tq)] * 2, axis=0)
    lam = _diff_lambda(lq1_ref, lk1_ref, lq2_ref, lk2_ref, lam_init)
    for h in range(heads):
        hs = slice(h * LANES, (h + 1) * LANES)
        qm = jnp.concatenate(_split_maps(q_ref[:, hs]), axis=0)
        kc = kc_ref[pl.ds(h, past, stride=heads), :].astype(BF16)
        vc = vc_ref[pl.ds(h, past, stride=heads), :].astype(BF16)
        sc = _dot_t(qm, kc)
        sn = jnp.where(mask_n, _dot_t(qm, kn_ref[:, hs]), NEG_INF)
        m = jnp.maximum(jnp.max(sc, axis=-1, keepdims=True), jnp.max(sn, axis=-1, keepdims=True))
        pc = jnp.exp2(sc - m)
        pn = jnp.exp2(sn - m)
        l = jnp.sum(pc, axis=-1, keepdims=True) + jnp.sum(pn, axis=-1, keepdims=True)
        o = (_dot(pc.astype(BF16), vc) + _dot(pn.astype(BF16), vn_ref[:, hs])) / l
        o_ref[:, hs] = _subln(o[0:tq] - lam * o[tq:2 * tq], g_ref[...], lam_init)


def _diff_sample(q, kb, vb, cache_k, cache_v, lam_w, g, nbatch, seq_len, lam_init):
    n, dw = q.shape
    heads = dw // LANES
    past = cache_k.shape[1]
    assert cache_k.shape[0] == nbatch and (past // CHUNK) * CHUNK == past and heads == SUBLANES
    kc = cache_k.reshape(nbatch * past * heads, LANES)
    vc = cache_v.reshape(nbatch * past * heads, LANES)
    kern = functools.partial(_diff_sample_kernel, past=past, heads=heads, lam_init=lam_init)
    vec = lambda a: _resident(a.shape)
    row = lambda b: (b, 0)
    return pl.pallas_call(
        kern,
        grid=(nbatch,),
        in_specs=[pl.BlockSpec((seq_len, dw), row),
                  pl.BlockSpec((past * heads, LANES), row), pl.BlockSpec((past * heads, LANES), row),
                  pl.BlockSpec((seq_len, dw), row), pl.BlockSpec((seq_len, dw), row),
                  vec(lam_w[0]), vec(lam_w[1]), vec(lam_w[2]), vec(lam_w[3]), vec(g)],
        out_specs=pl.BlockSpec((seq_len, dw), row),
        out_shape=jax.ShapeDtypeStruct((n, dw), BF16),
        compiler_params=_cparams(1),
        name="diff_attn_sample",
    )(q, kc, vc, kb, vb, *lam_w, g)


def _out_proj_kernel(*refs, n_lhs, final):
    lhs = refs[:n_lhs]
    w_ref, x_ref, g_ref = refs[n_lhs:n_lhs + 3]
    outs = refs[n_lhs + 3:]
    acc = x_ref[...]
    off = 0
    for a in lhs:
        kdim = a.shape[1]
        acc = acc + _dot(a[...], w_ref[off:off + kdim, :])
        off += kdim
    if final:
        outs[0][...] = _rms(acc, g_ref[...])
    else:
        outs[0][...] = acc
        outs[1][...] = _rms(acc, g_ref[...]).astype(BF16)


def _out_proj(lhs, w, x, g, final=False, tm_pref=512, layer=None):
    n, d = x.shape
    tm = _tile(n, tm_pref)
    row = lambda i: (i, 0)
    kern = functools.partial(_out_proj_kernel, n_lhs=len(lhs), final=final)
    in_specs = [pl.BlockSpec((tm, a.shape[1]), row) for a in lhs]
    if layer is None:
        wspec = _resident(w.shape)
    else:
        wspec = pl.BlockSpec((None,) + w.shape[1:], lambda *_: (layer, 0, 0), pipeline_mode=pl.Buffered(1))
    in_specs += [wspec, pl.BlockSpec((tm, d), row), _resident(g.shape)]
    if final:
        out_specs = [pl.BlockSpec((tm, d), row)]
        out_shape = [jax.ShapeDtypeStruct((n, d), F32)]
    else:
        out_specs = [pl.BlockSpec((tm, d), row), pl.BlockSpec((tm, d), row)]
        out_shape = [jax.ShapeDtypeStruct((n, d), F32), jax.ShapeDtypeStruct((n, d), BF16)]
    return pl.pallas_call(
        kern, grid=(n // tm,), in_specs=in_specs, out_specs=out_specs, out_shape=out_shape,
        compiler_params=_cparams(1), name="out_proj",
    )(*lhs, w, x, g)


def _ffn_up_kernel(hn_ref, wv_ref, wg_ref, cw_ref, cb_ref, st_ref, act_ref, stout_ref, prev_ref,
                   *, seg, tiles_per_seq):
    i = pl.program_id(1)
    hn = hn_ref[...]
    tm = hn.shape[0]
    tf = act_ref.shape[1]
    if tiles_per_seq > 1:
        @pl.when((i % tiles_per_seq) == 0)
        def _():
            prev_ref[...] = st_ref[0]

    cwid = min(tf, MXU_DIM)
    for c0 in range(0, tf, cwid):
        cs = slice(c0, c0 + cwid)
        gate = _dot(hn, wg_ref[:, cs])
        val = _dot(hn, wv_ref[:, cs])
        cw = cw_ref[:, cs]
        cb = cb_ref[:, cs]
        for s in range(tm // seg):
            g0 = gate[s * seg:(s + 1) * seg]
            prev = st_ref[s, :, cs] if tiles_per_seq == 1 else prev_ref[:, cs]
            ext = jnp.concatenate([prev, g0], axis=0)
            g1 = pltpu.roll(ext, 1, 0)[SUBLANES:]
            g2 = pltpu.roll(ext, 2, 0)[SUBLANES:]
            c = cb + cw[0:1] * g2 + cw[1:2] * g1 + cw[2:3] * g0
            act_ref[s * seg:(s + 1) * seg, cs] = (jax.nn.silu(c) * val[s * seg:(s + 1) * seg]).astype(BF16)
            last = g0[seg - SUBLANES:seg]
            stout_ref[s, :, cs] = last
            if tiles_per_seq > 1:
                prev_ref[:, cs] = last


def _ffn_up(hn, w_in, layer, conv_w, conv_b, conv_state, seq_len):
    n, d = hn.shape
    f = w_in.shape[2] // 2
    nseq = n // seq_len
    tf = max(t for t in range(MXU_DIM, f + 1, MXU_DIM) if f % t == 0 and 4 * d * t <= FFN_WEIGHT_BYTES)
    tm = _tile(n, 1024)
    nf = f // tf
    wspec = lambda col: pl.BlockSpec((None, d, tf), lambda j, i: (layer, 0, col(j)), pipeline_mode=pl.Buffered(1))
    if tm >= seq_len:
        seg, tps, spt = seq_len, 1, tm // seq_len
        st_idx = lambda j, i: (i, 0, j)
    else:
        seg, tps, spt = tm, seq_len // tm, 1
        st_idx = lambda j, i: (i // tps, 0, j)
    assert seg % SUBLANES == 0 and seg >= SUBLANES
    st = jnp.pad(conv_state, ((0, 0), (SUBLANES - (CONV_W - 1), 0), (0, 0)))
    cw = jnp.pad(conv_w, ((0, SUBLANES - CONV_W), (0, 0)))
    cb = conv_b.reshape(1, f)
    kern = functools.partial(_ffn_up_kernel, seg=seg, tiles_per_seq=tps)
    act, st_out = pl.pallas_call(
        kern,
        grid=(nf, n // tm),
        in_specs=[pl.BlockSpec((tm, d), lambda j, i: (i, 0)),
                  wspec(lambda j: j), wspec(lambda j: nf + j),
                  pl.BlockSpec((SUBLANES, tf), lambda j, i: (0, j)),
                  pl.BlockSpec((1, tf), lambda j, i: (0, j)),
                  pl.BlockSpec((spt, SUBLANES, tf), st_idx)],
        out_specs=[pl.BlockSpec((tm, tf), lambda j, i: (i, j)),
                   pl.BlockSpec((spt, SUBLANES, tf), st_idx)],
        out_shape=[jax.ShapeDtypeStruct((n, f), BF16), jax.ShapeDtypeStruct((nseq, SUBLANES, f), F32)],
        scratch_shapes=[pltpu.VMEM((SUBLANES, tf), F32)],
        compiler_params=_cparams(2),
        name="ffn_up",
    )(hn, w_in, w_in, cw, cb, st)
    return act, st_out[:, SUBLANES - (CONV_W - 1):, :]


def _rope_pair(x, cos, sin):
    return x * cos + pltpu.roll(x, MLA_ROPE, 1) * sin


def _odd_in_kernel(hn_ref, win_ref, gq_ref, gkv_ref, wuq_ref, cos_ref, sin_ref, *rest,
                   qr, kvr, heads, qscale, expand):
    if expand:
        wk_ref, wvt_ref, q_ref, ckv_ref, ckvb_ref, kpe_ref, kpeb_ref, kn_ref, vt_ref = rest
    else:
        q_ref, ckv_ref, ckvb_ref, kpe_ref, kpeb_ref = rest
    hn = hn_ref[...]
    cos = cos_ref[...]
    sin = sin_ref[...]
    cq = _rms(_dot(hn, win_ref[:, 0:qr]), gq_ref[...]).astype(BF16)
    ckv = _rms(_dot(hn, win_ref[:, qr:qr + kvr]), gkv_ref[...])
    kpe = _rope_pair(_dot(hn, win_ref[:, qr + kvr:qr + kvr + LANES]), cos, sin)
    ckv_ref[...] = ckv
    ckvb = ckv.astype(BF16)
    ckvb_ref[...] = ckvb
    kpe_ref[...] = kpe[:, 0:MLA_ROPE]
    kpeb_ref[...] = kpe.astype(BF16)
    hw = MXU_DIM
    for h in range(heads):
        qh = _dot(cq, wuq_ref[:, h * hw:(h + 1) * hw]) * qscale
        q_ref[:, h * hw:h * hw + LANES] = qh[:, 0:LANES].astype(BF16)
        q_ref[:, h * hw + LANES:(h + 1) * hw] = _rope_pair(qh[:, LANES:hw], cos, sin).astype(BF16)
    if expand:
        kn_ref[...] = _dot(ckvb, wk_ref[...]).astype(BF16)
        vt_ref[...] = _dot_t(wvt_ref[...], ckvb).astype(BF16)


def _odd_in(hn, win, gq, gkv, wuq, cos, sin, wkv, seq_len, heads):
    n, d = hn.shape
    qr, kvr = gq.shape[1], gkv.shape[1]
    tm = _tile(n, 256)
    row = lambda i: (i, 0)
    if tm <= seq_len:
        tps = seq_len // tm
        pos = lambda i: (i % tps, 0)
    else:
        cos = jnp.tile(cos, (tm // seq_len, 1))
        sin = jnp.tile(sin, (tm // seq_len, 1))
        pos = lambda i: (0, 0)
    expand = wkv is not None
    kern = functools.partial(_odd_in_kernel, qr=qr, kvr=kvr, heads=heads,
                             qscale=(MLA_NOPE + MLA_ROPE) ** -0.5 * LOG2E, expand=expand)
    in_specs = [pl.BlockSpec((tm, d), row), _resident(win.shape), _resident(gq.shape), _resident(gkv.shape),
                _resident(wuq.shape), pl.BlockSpec((tm, LANES), pos), pl.BlockSpec((tm, LANES), pos)]
    args = [hn, win, gq, gkv, wuq, cos, sin]
    out_specs = [pl.BlockSpec((tm, heads * MXU_DIM), row), pl.BlockSpec((tm, kvr), row),
                 pl.BlockSpec((tm, kvr), row), pl.BlockSpec((tm, MLA_ROPE), row), pl.BlockSpec((tm, LANES), row)]
    out_shape = [jax.ShapeDtypeStruct((n, heads * MXU_DIM), BF16), jax.ShapeDtypeStruct((n, kvr), F32),
                 jax.ShapeDtypeStruct((n, kvr), BF16), jax.ShapeDtypeStruct((n, MLA_ROPE), F32),
                 jax.ShapeDtypeStruct((n, LANES), BF16)]
    if expand:
        in_specs += [_resident(wkv[0].shape), _resident(wkv[1].shape)]
        args += list(wkv)
        out_specs += [pl.BlockSpec((tm, heads * MLA_NOPE), row),
                      pl.BlockSpec((heads * MLA_V, tm), lambda i: (0, i))]
        out_shape += [jax.ShapeDtypeStruct((n, heads * MLA_NOPE), BF16),
                      jax.ShapeDtypeStruct((heads * MLA_V, n), BF16)]
    return pl.pallas_call(
        kern, grid=(n // tm,), in_specs=in_specs, out_specs=out_specs, out_shape=out_shape,
        compiler_params=_cparams(1), name="odd_in",
    )(*args)


def _mla_prompt_kernel(q_ref, kn_ref, kpe_ref, vt_ref, o_ref, kx_ref, vxt_ref, m_ref, a_ref, *, tq, hp):
    i = pl.program_id(2)
    hw = MXU_DIM

    @pl.when(i == 0)
    def _():
        for j in range(hp):
            kx_ref[j, :, 0:MLA_NOPE] = kn_ref[:, j * MLA_NOPE:(j + 1) * MLA_NOPE]
            kx_ref[j, :, MLA_NOPE:hw] = kpe_ref[...]
            for kt in range(vxt_ref.shape[1]):
                vxt_ref[j, kt, 0:MLA_V, :] = vt_ref[j * MLA_V:(j + 1) * MLA_V, kt * tq:(kt + 1) * tq]
                vxt_ref[j, kt, MLA_V:VX_ROWS, :] = _unit_rows(tq)

    m_ref[...] = jnp.full(m_ref.shape, NEG_INF, F32)
    a_ref[...] = jnp.zeros(a_ref.shape, F32)

    def streams_at(kt):
        sl = pl.ds(pl.multiple_of(kt * tq, tq), tq)
        return [(kx_ref[j, sl, :], q_ref[:, j * hw:(j + 1) * hw], vxt_ref[j, kt], m_ref.at[j], a_ref.at[j])
                for j in range(hp)]

    _flash_sweep(i, streams_at, tq)
    for j in range(hp):
        a = a_ref[j]
        o_ref[:, j * MLA_V:(j + 1) * MLA_V] = (a[0:MLA_V] / a[MLA_V:MLA_V + 1]).T.astype(BF16)


def _mla_prompt(q, kn, kpeb, vt, nbatch, seq_len, heads):
    n = q.shape[0]
    tq = _tile(seq_len, 512)
    hp = 4
    assert tq % CHUNK == 0 and heads % hp == 0 and MLA_V == LANES
    nq = seq_len // tq
    kern = functools.partial(_mla_prompt_kernel, tq=tq, hp=hp)
    return pl.pallas_call(
        kern,
        grid=(nbatch, heads // hp, nq),
        in_specs=[pl.BlockSpec((tq, hp * MXU_DIM), lambda b, h, i: (b * nq + i, h)),
                  pl.BlockSpec((seq_len, hp * MLA_NOPE), lambda b, h, i: (b, h)),
                  pl.BlockSpec((seq_len, LANES), lambda b, h, i: (b, 0)),
                  pl.BlockSpec((hp * MLA_V, seq_len), lambda b, h, i: (h, b))],
        out_specs=pl.BlockSpec((tq, hp * MLA_V), lambda b, h, i: (b * nq + i, h)),
        out_shape=jax.ShapeDtypeStruct((n, heads * MLA_V), BF16),
        scratch_shapes=[pltpu.VMEM((hp, seq_len, MXU_DIM), BF16),
                        pltpu.VMEM((hp, nq, VX_ROWS, tq), BF16),
                        pltpu.VMEM((hp, 1, tq), F32), pltpu.VMEM((hp, VX_ROWS, tq), F32)],
        compiler_params=_cparams(3),
        name="mla_attn_prompt",
    )(q, kn, kpeb, vt)


def _mla_sample_kernel(q_ref, cc_ref, pc_ref, cn_ref, pn_ref, wk_ref, wv_ref, o_ref, ql_ref, qp_ref,
                       *, heads, past):
    tq = q_ref.shape[0]
    hw = MXU_DIM
    for h in range(heads):
        qn = q_ref[:, h * hw:h * hw + MLA_NOPE]
        ql_ref[h * tq:(h + 1) * tq, :] = _dot(qn, wk_ref[h]).astype(BF16)
        qp_ref[h * tq:(h + 1) * tq, :] = q_ref[:, h * hw + MLA_NOPE:(h + 1) * hw]
    ql = ql_ref[...]
    qp = qp_ref[...]
    cc = cc_ref[...].astype(BF16)
    pc = pc_ref[...].astype(BF16)
    cn = cn_ref[...]
    mask_n = jnp.concatenate([_chunk_mask(past, past, tq, tq)] * heads, axis=0)
    sc = _dot_t(ql, cc) + _dot_t(qp[:, 0:MLA_ROPE], pc)
    sn = jnp.where(mask_n, _dot_t(ql, cn) + _dot_t(qp, pn_ref[...]), NEG_INF)
    m = jnp.maximum(jnp.max(sc, axis=-1, keepdims=True), jnp.max(sn, axis=-1, keepdims=True))
    ec = jnp.exp2(sc - m)
    en = jnp.exp2(sn - m)
    l = jnp.sum(ec, axis=-1, keepdims=True) + jnp.sum(en, axis=-1, keepdims=True)
    ol = ((_dot(ec.astype(BF16), cc) + _dot(en.astype(BF16), cn)) / l).astype(BF16)
    for h in range(heads):
        o_ref[:, h * MLA_V:(h + 1) * MLA_V] = _dot(ol[h * tq:(h + 1) * tq], wv_ref[h]).astype(BF16)


def _mla_sample(q, ckvb, kpeb, cache_ckv, cache_kpe, wk_t, wv, nbatch, seq_len, heads):
    n = q.shape[0]
    past, kvr = cache_ckv.shape[1], cache_ckv.shape[2]
    assert (past // CHUNK) * CHUNK == past
    cc = cache_ckv.reshape(nbatch * past, kvr)
    pc = cache_kpe.reshape(nbatch * past, MLA_ROPE)
    kern = functools.partial(_mla_sample_kernel, heads=heads, past=past)
    row = lambda b: (b, 0)
    return pl.pallas_call(
        kern,
        grid=(nbatch,),
        in_specs=[pl.BlockSpec((seq_len, heads * MXU_DIM), row),
                  pl.BlockSpec((past, kvr), row), pl.BlockSpec((past, MLA_ROPE), row),
                  pl.BlockSpec((seq_len, kvr), row), pl.BlockSpec((seq_len, LANES), row),
                  _resident(wk_t.shape), _resident(wv.shape)],
        out_specs=pl.BlockSpec((seq_len, heads * MLA_V), row),
        out_shape=jax.ShapeDtypeStruct((n, heads * MLA_V), BF16),
        scratch_shapes=[pltpu.VMEM((heads * seq_len, kvr), BF16), pltpu.VMEM((heads * seq_len, LANES), BF16)],
        compiler_params=_cparams(1),
        name="mla_attn_sample",
    )(q, cc, pc, ckvb, kpeb, wk_t, wv)


def _rope_tables(pos):
    half = MLA_ROPE // 2
    inv = ROPE_BASE ** (-jnp.arange(half, dtype=F32) / half)
    ang = pos.astype(F32)[:, None] * inv[None, :]
    cos, sin = jnp.cos(ang), jnp.sin(ang)
    zero = jnp.zeros_like(cos)
    return (jnp.concatenate([cos, cos, zero, zero], axis=1),
            jnp.concatenate([-sin, sin, zero, zero], axis=1))


def _swap_halves(w):
    half = MLA_ROPE // 2
    return jnp.concatenate([w[..., half:], w[..., :half]], axis=-1)


def _prepare_weights(p):
    w = {}
    heads = p['mla_w_uq'].shape[2]
    w['heads'] = heads
    w['norm_mix'] = p['norm_mix'][:, None, :]
    w['norm_ffn'] = p['norm_ffn'][:, None, :]
    w['norm_final'] = p['norm_final'][None, :]
    w['w_in_even'] = p['w_in_even'][0].astype(BF16)
    w['w_out_even'] = p['w_out_even'][0].astype(BF16)
    w['s5'] = dict(a_re=p['s5_a_re'][0], a_im=p['s5_a_im'][0], b_re=p['s5_b_re'][0], b_im=p['s5_b_im'][0],
                   c_re=p['s5_c_re'][0], c_im=p['s5_c_im'][0], d=p['s5_d'][0], log_dt=p['s5_log_dt'][0],
                   w_glu=p['s5_w_glu'][0], b_glu=p['s5_b_glu'][0])
    w['lam'] = [p[k][0][None, :] for k in ('diff_lambda_q1', 'diff_lambda_k1', 'diff_lambda_q2', 'diff_lambda_k2')]
    w['subln'] = p['diff_subln'][0][None, :]
    wi = p['w_in_odd'][0]
    qr = p['mla_q_norm'].shape[1]
    kvr = p['mla_kv_norm'].shape[1]
    wpe = wi[:, qr + kvr:]
    w['w_in_odd'] = jnp.concatenate([wi[:, :qr + kvr], wpe, _swap_halves(wpe)], axis=1).astype(BF16)
    w['gq'] = p['mla_q_norm'][0][None, :]
    w['gkv'] = p['mla_kv_norm'][0][None, :]
    wuq = p['mla_w_uq'][0]
    wuq = jnp.concatenate([wuq, _swap_halves(wuq[..., MLA_NOPE:])], axis=-1)
    w['w_uq'] = wuq.reshape(qr, heads * MXU_DIM).astype(BF16)
    wukv = p['mla_w_ukv'][0]
    w['w_kv'] = (wukv[..., :MLA_NOPE].reshape(kvr, heads * MLA_NOPE).astype(BF16),
                 wukv[..., MLA_NOPE:].reshape(kvr, heads * MLA_V).T.astype(BF16))
    w['w_uk_t'] = jnp.transpose(wukv[..., :MLA_NOPE], (1, 2, 0)).astype(BF16)
    w['w_uv'] = jnp.transpose(wukv[..., MLA_NOPE:], (1, 0, 2)).astype(BF16)
    w['w_out_odd'] = p['w_out_odd'][0].astype(BF16)
    w['ffn_w_in'] = p['ffn_w_in'].astype(BF16)
    w['ffn_w_down'] = p['ffn_w_down'].astype(BF16)
    w['ffn_conv_w'] = p['ffn_conv_w']
    w['ffn_conv_b'] = p['ffn_conv_b']
    return w


def _trunk(x3, s5_re0, s5_im0, k_past, v_past, ckv_past, kpe_past, conv0, pos0, w):
    nb, t, d = x3.shape
    n = nb * t
    x = x3.reshape(n, d)
    heads = w['heads']
    s5w = w['s5']['d'].size
    dw = (w['w_in_even'].shape[1] - s5w) // 3
    dheads = dw // LANES

    lam_init = 0.8 - 0.6 * math.exp(-0.3 * 0)
    u, q, k, v, kb, vb, vbt = _even_in(x, w['norm_mix'][0], w['w_in_even'], s5w, dw)
    y_s5, ht_re, ht_im = _s5(u, s5_re0, s5_im0, t, w['s5'])
    if k_past is None:
        o = _diff_prompt(q, kb, vbt, w['lam'], w['subln'], nb, t, lam_init)
    else:
        o = _diff_sample(q, kb, vb, k_past, v_past, w['lam'], w['subln'], nb, t, lam_init)
    x, hn = _out_proj([y_s5, o], w['w_out_even'], x, w['norm_ffn'][0])
    act, conv_a = _ffn_up(hn, w['ffn_w_in'], 0, w['ffn_conv_w'][0], w['ffn_conv_b'][0], conv0[0], t)
    x, hn = _out_proj([act], w['ffn_w_down'], x, w['norm_mix'][1], tm_pref=256, layer=0)

    cos, sin = _rope_tables(pos0 + jnp.arange(t, dtype=jnp.int32))
    if ckv_past is None:
        qm, ckv, ckvb, kpe, kpeb, kn, vm = _odd_in(hn, w['w_in_odd'], w['gq'], w['gkv'], w['w_uq'], cos, sin,
                                                   w['w_kv'], t, heads)
        om = _mla_prompt(qm, kn, kpeb, vm, nb, t, heads)
    else:
        qm, ckv, ckvb, kpe, kpeb = _odd_in(hn, w['w_in_odd'], w['gq'], w['gkv'], w['w_uq'], cos, sin,
                                           None, t, heads)
        om = _mla_sample(qm, ckvb, kpeb, ckv_past, kpe_past, w['w_uk_t'], w['w_uv'], nb, t, heads)
    x, hn = _out_proj([om], w['w_out_odd'], x, w['norm_ffn'][1])
    act, conv_b = _ffn_up(hn, w['ffn_w_in'], 1, w['ffn_conv_w'][1], w['ffn_conv_b'][1], conv0[1], t)
    (y,) = _out_proj([act], w['ffn_w_down'], x, w['norm_final'], final=True, tm_pref=256, layer=1)

    groups, n_state = w['s5']['a_re'].shape
    return (y.reshape(nb, t, d), ht_re[None], ht_im[None],
            k.reshape(1, nb, t, dheads, LANES), v.reshape(1, nb, t, dheads, LANES),
            ckv.reshape(1, nb, t, -1), kpe.reshape(1, nb, t, MLA_ROPE), jnp.stack([conv_a, conv_b]))


def kernel(x_prompt, x_sample, state_s5_re, state_s5_im, cache_diff_k, cache_diff_v, cache_mla_ckv, cache_mla_kpe, state_ffn_conv, norm_mix, norm_ffn, norm_final, w_in_even, w_out_even, s5_a_re, s5_a_im, s5_b_re, s5_b_im, s5_c_re, s5_c_im, s5_d, s5_log_dt, s5_w_glu, s5_b_glu, diff_lambda_q1, diff_lambda_k1, diff_lambda_q2, diff_lambda_k2, diff_subln, w_in_odd, mla_q_norm, mla_kv_norm, mla_w_uq, mla_w_ukv, w_out_odd, ffn_w_in, ffn_conv_w, ffn_conv_b, ffn_w_down):
    w = _prepare_weights(dict(
        norm_mix=norm_mix, norm_ffn=norm_ffn, norm_final=norm_final, w_in_even=w_in_even, w_out_even=w_out_even,
        s5_a_re=s5_a_re, s5_a_im=s5_a_im, s5_b_re=s5_b_re, s5_b_im=s5_b_im, s5_c_re=s5_c_re, s5_c_im=s5_c_im,
        s5_d=s5_d, s5_log_dt=s5_log_dt, s5_w_glu=s5_w_glu, s5_b_glu=s5_b_glu,
        diff_lambda_q1=diff_lambda_q1, diff_lambda_k1=diff_lambda_k1, diff_lambda_q2=diff_lambda_q2,
        diff_lambda_k2=diff_lambda_k2, diff_subln=diff_subln, w_in_odd=w_in_odd, mla_q_norm=mla_q_norm,
        mla_kv_norm=mla_kv_norm, mla_w_uq=mla_w_uq, mla_w_ukv=mla_w_ukv, w_out_odd=w_out_odd,
        ffn_w_in=ffn_w_in, ffn_conv_w=ffn_conv_w, ffn_conv_b=ffn_conv_b, ffn_w_down=ffn_w_down))
    nb_p = x_prompt.shape[0]
    groups, n_state = s5_a_re.shape[1:]
    d_ff = ffn_conv_b.shape[1]
    depth = ffn_conv_b.shape[0]
    s5_zero = jnp.zeros((nb_p, groups, n_state), F32)
    conv_zero = jnp.zeros((depth, nb_p, CONV_W - 1, d_ff), F32)
    (y_p, re_p, im_p, k_p, v_p, ckv_p, kpe_p, conv_p) = _trunk(
        x_prompt, s5_zero, s5_zero, None, None, None, None, conv_zero, 0, w)
    past = cache_diff_k.shape[2]
    (y_s, re_s, im_s, k_s, v_s, ckv_s, kpe_s, conv_s) = _trunk(
        x_sample, state_s5_re[0], state_s5_im[0], cache_diff_k[0], cache_diff_v[0], cache_mla_ckv[0],
        cache_mla_kpe[0], state_ffn_conv, past, w)
    return (y_p, y_s, re_p, im_p, re_s, im_s, k_p, v_p, k_s, v_s, ckv_p, kpe_p, ckv_s, kpe_s, conv_p, conv_s)
```

```python
import functools
import math

import jax
import jax.numpy as jnp
from jax import lax
from jax.experimental import pallas as pl
from jax.experimental.pallas import tpu as pltpu

F32 = jnp.float32
BF16 = jnp.bfloat16

CHUNK = 64
EPS = 1e-6
NEG_INF = -1e30
ROPE_BASE = 10000.0
S5_GROUP = 16
DIFF_DK = 64
MLA_NOPE = 128
MLA_ROPE = 64
MLA_V = 128
CONV_W = 3
LOG2E = 1.4426950408889634

LANES = 128
SUBLANES = 8
MXU_DIM = 256
VMEM_LIMIT_BYTES = 56 * 1024 * 1024

FFN_WEIGHT_BYTES = 24 * 1024 * 1024
FLASH_COLS = 256
VX_ROWS = LANES + 16
S5_L = 16
S5_GB = MXU_DIM // S5_GROUP


def _cparams(n_axes):
    return pltpu.CompilerParams(
        dimension_semantics=("arbitrary",) * n_axes,
        vmem_limit_bytes=VMEM_LIMIT_BYTES)


def _resident(shape):
    nd = len(shape)
    return pl.BlockSpec(shape, lambda *_: (0,) * nd, pipeline_mode=pl.Buffered(1))


def _tile(n, pref):
    t = min(n, pref)
    while n % t:
        t //= 2
    return t


def _dot(a, b):
    return jnp.dot(a, b, preferred_element_type=F32)


def _dot_t(a, b):
    return lax.dot_general(a, b, (((1,), (1,)), ((), ())), preferred_element_type=F32)


def _rms(x, g):
    ms = jnp.mean(x * x, axis=-1, keepdims=True)
    return x * lax.rsqrt(ms + EPS) * g


def _even_in_kernel(x_ref, g_ref, w_ref, u_ref, q_ref, k_ref, v_ref, kb_ref, vb_ref,
                    *, s5w, dw, qscale, v_transposed):
    xn = _rms(x_ref[...], g_ref[...]).astype(BF16)
    u_ref[...] = _dot(xn, w_ref[:, 0:s5w])
    q_ref[...] = (_dot(xn, w_ref[:, s5w:s5w + dw]) * qscale).astype(BF16)
    k = _dot(xn, w_ref[:, s5w + dw:s5w + 2 * dw])
    k_ref[...] = k
    kb_ref[...] = k.astype(BF16)
    v = _dot(xn, w_ref[:, s5w + 2 * dw:s5w + 3 * dw])
    v_ref[...] = v
    vb_ref[...] = (v.T if v_transposed else v).astype(BF16)


def _even_in(x, g, w, s5w, dw, v_transposed):
    n, d = x.shape
    tm = _tile(n, 512)
    row = lambda i: (i, 0)
    kern = functools.partial(_even_in_kernel, s5w=s5w, dw=dw, qscale=DIFF_DK ** -0.5 * LOG2E,
                             v_transposed=v_transposed)
    vb_spec = pl.BlockSpec((dw, tm), lambda i: (0, i)) if v_transposed else pl.BlockSpec((tm, dw), row)
    return pl.pallas_call(
        kern,
        grid=(n // tm,),
        in_specs=[pl.BlockSpec((tm, d), row), _resident(g.shape), _resident(w.shape)],
        out_specs=[pl.BlockSpec((tm, s5w), row), pl.BlockSpec((tm, dw), row), pl.BlockSpec((tm, dw), row),
                   pl.BlockSpec((tm, dw), row), pl.BlockSpec((tm, dw), row), vb_spec],
        out_shape=[jax.ShapeDtypeStruct((n, s5w), F32), jax.ShapeDtypeStruct((n, dw), BF16),
                   jax.ShapeDtypeStruct((n, dw), F32), jax.ShapeDtypeStruct((n, dw), F32),
                   jax.ShapeDtypeStruct((n, dw), BF16),
                   jax.ShapeDtypeStruct((dw, n) if v_transposed else (n, dw), BF16)],
        compiler_params=_cparams(1),
        name="even_in",
    )(x, g, w)


def _s5_kernel(u_ref, h0_ref, lre_ref, lim_ref, dt_ref, bre_ref, bim_ref, cbd_ref, d_ref, wglu_ref, bglu_ref,
               y_ref, ht_ref,
               a_ref, a16_ref, bbd_ref, us_ref, xs_ref, xe_ref, hs_ref, carry_ref, ys_ref, yn_ref,
               *, rows, width, cps, ngb):
    t = pl.program_id(0)
    nblk = ngb * 16
    half = 8
    gw = MXU_DIM

    @pl.when(t == 0)
    def _prepare():
        carry_ref[...] = jnp.zeros_like(carry_ref)
        for gb in range(ngb):
            for k in range(half):
                lre = jnp.minimum(lre_ref[gb * half + k], -1e-4)
                lim = lim_ref[gb * half + k]
                dt = jnp.exp(dt_ref[gb * half + k])
                mag = jnp.exp(lre * dt)
                are = mag * jnp.cos(lim * dt)
                aim = mag * jnp.sin(lim * dt)
                den = lre * lre + lim * lim
                cre = ((are - 1.0) * lre + aim * lim) / den
                cim = (aim * lre - (are - 1.0) * lim) / den
                ire, iim = gb * 16 + k, gb * 16 + half + k
                a_ref[ire] = are
                a_ref[iim] = aim
                pre, pim = are, aim
                for _ in range(4):
                    pre, pim = pre * pre - pim * pim, 2.0 * pre * pim
                a16_ref[ire] = pre
                a16_ref[iim] = pim
                br = bre_ref[gb, :, k * LANES:(k + 1) * LANES]
                bi = bim_ref[gb, :, k * LANES:(k + 1) * LANES]
                bbd_ref[gb, :, k * LANES:(k + 1) * LANES] = (cre * br - cim * bi).astype(BF16)
                bbd_ref[gb, :, (half + k) * LANES:(half + k + 1) * LANES] = (cre * bi + cim * br).astype(BF16)

    ncb = width // LANES
    for cb in range(ncb):
        us_ref[cb] = u_ref[:, cb * LANES:(cb + 1) * LANES]

    def step(l):
        return pl.ds(l * rows, rows)

    def at_step(l):
        return pl.ds(l, rows, stride=S5_L)

    for gb in range(ngb):
        cbs = range(gb * gw // LANES, (gb + 1) * gw // LANES)
        lhs = jnp.concatenate(
            [jnp.concatenate([us_ref[cb, at_step(l), :] for cb in cbs], axis=1) for l in range(S5_L)], axis=0)
        x = _dot(lhs.astype(BF16), bbd_ref[gb])
        for j in range(16):
            xs_ref[gb * 16 + j] = x[:, j * LANES:(j + 1) * LANES]
        for k in range(half):
            ire, iim = gb * 16 + k, gb * 16 + half + k
            are, aim = a_ref[ire], a_ref[iim]
            hre = xs_ref[ire, step(0), :]
            him = xs_ref[iim, step(0), :]
            for l in range(1, S5_L):
                xre = xs_ref[ire, step(l), :]
                xim = xs_ref[iim, step(l), :]
                hre, him = are * hre - aim * him + xre, are * him + aim * hre + xim
            xe_ref[ire] = hre
            xe_ref[iim] = him

    base_row = t * rows
    for grp in range(ngb * half // 8):
        ire0 = (grp // (half // 8)) * 16 + (grp % (half // 8)) * 8
        iim0 = ire0 + half
        a16re = a16_ref[ire0:ire0 + 8]
        a16im = a16_ref[iim0:iim0 + 8]

        cre = carry_ref[ire0:ire0 + 8]
        cim = carry_ref[iim0:iim0 + 8]
        for r in range(rows):
            if cps >= rows:
                if r == 0:
                    is_start = (base_row % cps) == 0
                    cre = jnp.where(is_start, h0_ref[0, ire0:ire0 + 8], cre)
                    cim = jnp.where(is_start, h0_ref[0, iim0:iim0 + 8], cim)
            elif r % cps == 0:
                cre = h0_ref[r // cps, ire0:ire0 + 8]
                cim = h0_ref[r // cps, iim0:iim0 + 8]
            hs_ref[ire0:ire0 + 8, r:r + 1, :] = cre
            hs_ref[iim0:iim0 + 8, r:r + 1, :] = cim
            xre = xe_ref[ire0:ire0 + 8, r:r + 1, :]
            xim = xe_ref[iim0:iim0 + 8, r:r + 1, :]
            cre, cim = a16re * cre - a16im * cim + xre, a16re * cim + a16im * cre + xim
            if cps < rows and r % cps == cps - 1:
                ht_ref[r // cps, ire0:ire0 + 8] = cre
                ht_ref[r // cps, iim0:iim0 + 8] = cim
        if cps >= rows:
            ht_ref[0, ire0:ire0 + 8] = cre
            ht_ref[0, iim0:iim0 + 8] = cim
        carry_ref[ire0:ire0 + 8] = cre
        carry_ref[iim0:iim0 + 8] = cim

    for gb in range(ngb):
        for k in range(half):
            ire, iim = gb * 16 + k, gb * 16 + half + k
            are, aim = a_ref[ire], a_ref[iim]
            hre = hs_ref[ire]
            him = hs_ref[iim]
            for l in range(S5_L):
                xre = xs_ref[ire, step(l), :]
                xim = xs_ref[iim, step(l), :]
                hre, him = are * hre - aim * him + xre, are * him + aim * hre + xim
                xs_ref[ire, step(l), :] = hre
                xs_ref[iim, step(l), :] = him
        h = jnp.concatenate([xs_ref[gb * 16 + j].astype(BF16) for j in range(16)], axis=1)
        yc = _dot(h, cbd_ref[gb])
        for c in range(gw // LANES):
            ys_ref[gb * gw // LANES + c] = yc[:, c * LANES:(c + 1) * LANES]

    for cb in range(ncb):
        for r in range(rows):
            yn_ref[r * S5_L:(r + 1) * S5_L, cb * LANES:(cb + 1) * LANES] = ys_ref[cb, pl.ds(r, S5_L, stride=rows), :]

    y = yn_ref[...] + d_ref[...] * u_ref[...]
    z = jax.nn.gelu(y)
    gate = jax.nn.sigmoid(_dot(z.astype(BF16), wglu_ref[...]) + bglu_ref[...])
    y_ref[...] = (z * gate).astype(BF16)


def _s5_layout(p, ngb, n_state):
    return p.reshape(ngb * (S5_GB * n_state // LANES), 1, LANES)


def _s5_state_to_blocks(s, ngb):
    b = s.shape[0]
    return s.reshape(b, ngb, 8, 1, LANES)


def _s5(u, h0_re, h0_im, seq_len, w):
    n, width = u.shape
    groups, n_state = w['a_re'].shape
    assert n_state * S5_GB == 8 * LANES and width == groups * S5_GROUP
    ngb = groups // S5_GB
    nseq = n // seq_len
    cps = seq_len // S5_L
    nrow = n // S5_L
    rows = _tile(nrow, 32)
    assert rows % SUBLANES == 0 and (cps % rows == 0 or rows % cps == 0)
    spt = max(1, rows // cps)
    tps = max(1, cps // rows)
    nblk = ngb * 16
    sw = S5_GB * n_state

    h0 = jnp.concatenate([_s5_state_to_blocks(h0_re, ngb), _s5_state_to_blocks(h0_im, ngb)], axis=2)
    h0 = h0.reshape(nseq, nblk, 1, LANES)
    lre = _s5_layout(w['a_re'], ngb, n_state)
    lim = _s5_layout(w['a_im'], ngb, n_state)
    dt = _s5_layout(jnp.broadcast_to(w['log_dt'][:, None], (groups, n_state)), ngb, n_state)
    eye = jnp.eye(S5_GB, dtype=F32)

    def bdiag_b(b):
        bb = b.reshape(ngb, S5_GB, n_state, S5_GROUP)
        return jnp.einsum('agnp,gh->agphn', bb, eye).reshape(ngb, S5_GB * S5_GROUP, sw)

    def bdiag_c(c):
        cc = c.reshape(ngb, S5_GB, S5_GROUP, n_state)
        return jnp.einsum('agpn,gh->ahngp', cc, eye).reshape(ngb, sw, S5_GB * S5_GROUP)

    bre = bdiag_b(w['b_re'])
    bim = bdiag_b(w['b_im'])
    cbd = jnp.concatenate([bdiag_c(w['c_re']), -bdiag_c(w['c_im'])], axis=1).astype(BF16)
    d = w['d'].reshape(1, width)
    wglu = w['w_glu'].astype(BF16)
    bglu = w['b_glu'].reshape(1, width)

    seq_idx = (lambda t: (t // tps, 0, 0, 0)) if tps > 1 else (lambda t: (t, 0, 0, 0))
    kern = functools.partial(_s5_kernel, rows=rows, width=width, cps=cps, ngb=ngb)
    y, ht = pl.pallas_call(
        kern,
        grid=(nrow // rows,),
        in_specs=[pl.BlockSpec((S5_L * rows, width), lambda t: (t, 0)),
                  pl.BlockSpec((spt, nblk, 1, LANES), seq_idx),
                  _resident(lre.shape), _resident(lim.shape), _resident(dt.shape),
                  _resident(bre.shape), _resident(bim.shape), _resident(cbd.shape),
                  _resident(d.shape), _resident(wglu.shape), _resident(bglu.shape)],
        out_specs=[pl.BlockSpec((S5_L * rows, width), lambda t: (t, 0)),
                   pl.BlockSpec((spt, nblk, 1, LANES), seq_idx)],
        out_shape=[jax.ShapeDtypeStruct((n, width), BF16),
                   jax.ShapeDtypeStruct((nseq, nblk, 1, LANES), F32)],
        scratch_shapes=[pltpu.VMEM((nblk, 1, LANES), F32),
                        pltpu.VMEM((nblk, 1, LANES), F32),
                        pltpu.VMEM((ngb, MXU_DIM, 2 * sw), BF16),
                        pltpu.VMEM((width // LANES, S5_L * rows, LANES), F32),
                        pltpu.VMEM((nblk, S5_L * rows, LANES), F32),
                        pltpu.VMEM((nblk, rows, LANES), F32),
                        pltpu.VMEM((nblk, rows, LANES), F32),
                        pltpu.VMEM((nblk, 1, LANES), F32),
                        pltpu.VMEM((width // LANES, S5_L * rows, LANES), F32),
                        pltpu.VMEM((S5_L * rows, width), F32)],
        compiler_params=_cparams(1),
        name="s5_mix",
    )(u, h0, lre, lim, dt, bre, bim, cbd, d, wglu, bglu)
    ht = ht.reshape(nseq, ngb, 2, groups // ngb, n_state)
    ht_re = ht[:, :, 0].reshape(nseq, groups, n_state)
    ht_im = ht[:, :, 1].reshape(nseq, groups, n_state)
    return y, ht_re, ht_im


def _diff_lambda(lq1_ref, lk1_ref, lq2_ref, lk2_ref, lam_init):
    s1 = jnp.sum(lq1_ref[...] * lk1_ref[...], axis=-1, keepdims=True)
    s2 = jnp.sum(lq2_ref[...] * lk2_ref[...], axis=-1, keepdims=True)
    return jnp.exp(s1) - jnp.exp(s2) + lam_init


def _split_maps(q):
    lane = lax.broadcasted_iota(jnp.int32, q.shape, 1)
    zero = jnp.zeros_like(q)
    return jnp.where(lane < DIFF_DK, q, zero), jnp.where(lane >= DIFF_DK, q, zero)


def _chunk_mask(q0, k0, tq, tk):
    qc = (q0 + lax.broadcasted_iota(jnp.int32, (tq, tk), 0)) // CHUNK
    kc = (k0 + lax.broadcasted_iota(jnp.int32, (tq, tk), 1)) // CHUNK
    return kc <= qc


def _subln(o, g, lam_init):
    return (_rms(o, g) * (1.0 - lam_init)).astype(BF16)


def _flash_update(blocks):
    work = []
    for streams, mask in blocks:
        chains = []
        for k, q, vxt, m_ref, a_ref in streams:
            tq = q.shape[0]
            sub = min(tq, FLASH_COLS)
            for c0 in range(0, tq, sub):
                nk = k.shape[0] if mask is None else min(k.shape[0], c0 + sub)
                chains.append((k[0:nk], q, vxt[:, 0:nk], m_ref, a_ref, slice(c0, c0 + sub), nk))
        work.append((chains, [_dot_t(k, q[cs]) for k, q, _, _, _, cs, _ in chains], mask))
    for chains, scores, mask in work:
        probs = []
        for (_, _, _, m_ref, _, cs, nk), s in zip(chains, scores):
            if mask is not None:
                s = jnp.where(mask[0:nk, cs], s, NEG_INF)
            m_prev = m_ref[:, cs]
            m_new = jnp.maximum(m_prev, jnp.max(s, axis=0, keepdims=True))
            m_ref[:, cs] = m_new
            probs.append((jnp.exp2(m_prev - m_new), jnp.exp2(s - m_new).astype(BF16)))
        for (_, _, vxt, _, a_ref, cs, _), (alpha, p) in zip(chains, probs):
            a_ref[:, cs] = alpha * a_ref[:, cs] + _dot(vxt, p)


def _flash_sweep(i, streams_at, tq):
    diag = _chunk_mask_t(tq, tq)

    def pair(p, carry):
        _flash_update([(streams_at(2 * p), None), (streams_at(2 * p + 1), None)])
        return carry

    lax.fori_loop(0, i // 2, pair, 0)
    odd = (i % 2) == 1

    @pl.when(odd)
    def _():
        _flash_update([(streams_at(i - 1), None), (streams_at(i), diag)])

    @pl.when(jnp.logical_not(odd))
    def _():
        _flash_update([(streams_at(i), diag)])


def _chunk_mask_t(tk, tq):
    kc = lax.broadcasted_iota(jnp.int32, (tk, tq), 0) // CHUNK
    qc = lax.broadcasted_iota(jnp.int32, (tk, tq), 1) // CHUNK
    return kc <= qc


def _unit_rows(cols):
    return (lax.broadcasted_iota(jnp.int32, (VX_ROWS - LANES, cols), 0) == 0).astype(BF16)


def _diff_prompt_kernel(q_ref, k_ref, vt_ref, lq1_ref, lk1_ref, lq2_ref, lk2_ref, g_ref, o_ref,
                        vxt_ref, m_ref, a_ref, *, tq, hq, lam_init):
    i = pl.program_id(2)

    @pl.when(i == 0)
    def _():
        for j in range(hq):
            for kt in range(vxt_ref.shape[1]):
                vxt_ref[j, kt, 0:LANES, :] = vt_ref[j * LANES:(j + 1) * LANES, kt * tq:(kt + 1) * tq]
                vxt_ref[j, kt, LANES:VX_ROWS, :] = _unit_rows(tq)

    qs = [_split_maps(q_ref[:, j * LANES:(j + 1) * LANES]) for j in range(hq)]
    m_ref[...] = jnp.full(m_ref.shape, NEG_INF, F32)
    a_ref[...] = jnp.zeros(a_ref.shape, F32)

    def streams_at(kt):
        sl = pl.ds(pl.multiple_of(kt * tq, tq), tq)
        streams = []
        for j in range(hq):
            kb = k_ref[sl, j * LANES:(j + 1) * LANES]
            for mp in range(2):
                streams.append((kb, qs[j][mp], vxt_ref[j, kt], m_ref.at[2 * j + mp], a_ref.at[2 * j + mp]))
        return streams

    _flash_sweep(i, streams_at, tq)

    lam = _diff_lambda(lq1_ref, lk1_ref, lq2_ref, lk2_ref, lam_init)
    for j in range(hq):
        a1 = a_ref[2 * j]
        a2 = a_ref[2 * j + 1]
        ot = a1[0:LANES] / a1[LANES:LANES + 1] - lam * (a2[0:LANES] / a2[LANES:LANES + 1])
        ms = jnp.mean(ot * ot, axis=0, keepdims=True)
        ot = ot * lax.rsqrt(ms + EPS) * g_ref[...] * (1.0 - lam_init)
        o_ref[:, j * LANES:(j + 1) * LANES] = ot.T.astype(BF16)


def _diff_prompt(q, kb, vbt, lam_w, g, nbatch, seq_len, lam_init):
    n, dw = q.shape
    gcol = g.reshape(LANES, 1)
    heads = dw // LANES
    tq = _tile(seq_len, 512)
    hq = 4
    assert tq % CHUNK == 0 and heads % hq == 0
    nq = seq_len // tq
    kern = functools.partial(_diff_prompt_kernel, tq=tq, hq=hq, lam_init=lam_init)
    vec = lambda a: _resident(a.shape)
    return pl.pallas_call(
        kern,
        grid=(nbatch, heads // hq, nq),
        in_specs=[pl.BlockSpec((tq, hq * LANES), lambda b, h, i: (b * nq + i, h)),
                  pl.BlockSpec((seq_len, hq * LANES), lambda b, h, i: (b, h)),
                  pl.BlockSpec((hq * LANES, seq_len), lambda b, h, i: (h, b)),
                  vec(lam_w[0]), vec(lam_w[1]), vec(lam_w[2]), vec(lam_w[3]), vec(gcol)],
        out_specs=pl.BlockSpec((tq, hq * LANES), lambda b, h, i: (b * nq + i, h)),
        out_shape=jax.ShapeDtypeStruct((n, dw), BF16),
        scratch_shapes=[pltpu.VMEM((hq, nq, VX_ROWS, tq), BF16),
                        pltpu.VMEM((2 * hq, 1, tq), F32), pltpu.VMEM((2 * hq, VX_ROWS, tq), F32)],
        compiler_params=_cparams(3),
        name="diff_attn_prompt",
    )(q, kb, vbt, *lam_w, gcol)


def _diff_sample_kernel(q_ref, kc_ref, vc_ref, kn_ref, vn_ref, lq1_ref, lk1_ref, lq2_ref, lk2_ref, g_ref, o_ref,
                        *, past, heads, lam_init):
    tq = q_ref.shape[0]
    mask_n = jnp.concatenate([_chunk_mask(past, past, tq, tq)] * 2, axis=0)
    lam = _diff_lambda(lq1_ref, lk1_ref, lq2_ref, lk2_ref, lam_init)
    for h in range(heads):
        hs = slice(h * LANES, (h + 1) * LANES)
        qm = jnp.concatenate(_split_maps(q_ref[:, hs]), axis=0)
        kc = kc_ref[pl.ds(h, past, stride=heads), :].astype(BF16)
        vc = vc_ref[pl.ds(h, past, stride=heads), :].astype(BF16)
        sc = _dot_t(qm, kc)
        sn = jnp.where(mask_n, _dot_t(qm, kn_ref[:, hs]), NEG_INF)
        m = jnp.maximum(jnp.max(sc, axis=-1, keepdims=True), jnp.max(sn, axis=-1, keepdims=True))
        pc = jnp.exp2(sc - m)
        pn = jnp.exp2(sn - m)
        l = jnp.sum(pc, axis=-1, keepdims=True) + jnp.sum(pn, axis=-1, keepdims=True)
        o = (_dot(pc.astype(BF16), vc) + _dot(pn.astype(BF16), vn_ref[:, hs])) / l
        o_ref[:, hs] = _subln(o[0:tq] - lam * o[tq:2 * tq], g_ref[...], lam_init)


def _diff_sample(q, kb, vb, cache_k, cache_v, lam_w, g, nbatch, seq_len, lam_init):
    n, dw = q.shape
    heads = dw // LANES
    past = cache_k.shape[1]
    assert cache_k.shape[0] == nbatch and (past // CHUNK) * CHUNK == past and heads == SUBLANES
    kc = cache_k.reshape(nbatch * past * heads, LANES)
    vc = cache_v.reshape(nbatch * past * heads, LANES)
    kern = functools.partial(_diff_sample_kernel, past=past, heads=heads, lam_init=lam_init)
    vec = lambda a: _resident(a.shape)
    row = lambda b: (b, 0)
    return pl.pallas_call(
        kern,
        grid=(nbatch,),
        in_specs=[pl.BlockSpec((seq_len, dw), row),
                  pl.BlockSpec((past * heads, LANES), row), pl.BlockSpec((past * heads, LANES), row),
                  pl.BlockSpec((seq_len, dw), row), pl.BlockSpec((seq_len, dw), row),
                  vec(lam_w[0]), vec(lam_w[1]), vec(lam_w[2]), vec(lam_w[3]), vec(g)],
        out_specs=pl.BlockSpec((seq_len, dw), row),
        out_shape=jax.ShapeDtypeStruct((n, dw), BF16),
        compiler_params=_cparams(1),
        name="diff_attn_sample",
    )(q, kc, vc, kb, vb, *lam_w, g)


def _out_proj_kernel(*refs, n_lhs, final):
    lhs = refs[:n_lhs]
    w_ref, x_ref, g_ref = refs[n_lhs:n_lhs + 3]
    outs = refs[n_lhs + 3:]
    acc = x_ref[...]
    off = 0
    for a in lhs:
        kdim = a.shape[1]
        acc = acc + _dot(a[...], w_ref[off:off + kdim, :])
        off += kdim
    if final:
        outs[0][...] = _rms(acc, g_ref[...])
    else:
        outs[0][...] = acc
        outs[1][...] = _rms(acc, g_ref[...]).astype(BF16)


def _out_proj(lhs, w, x, g, final=False, tm_pref=512, layer=None):
    n, d = x.shape
    tm = _tile(n, tm_pref)
    row = lambda i: (i, 0)
    kern = functools.partial(_out_proj_kernel, n_lhs=len(lhs), final=final)
    in_specs = [pl.BlockSpec((tm, a.shape[1]), row) for a in lhs]
    if layer is None:
        wspec = _resident(w.shape)
    else:
        wspec = pl.BlockSpec((None,) + w.shape[1:], lambda *_: (layer, 0, 0), pipeline_mode=pl.Buffered(1))
    in_specs += [wspec, pl.BlockSpec((tm, d), row), _resident(g.shape)]
    if final:
        out_specs = [pl.BlockSpec((tm, d), row)]
        out_shape = [jax.ShapeDtypeStruct((n, d), F32)]
    else:
        out_specs = [pl.BlockSpec((tm, d), row), pl.BlockSpec((tm, d), row)]
        out_shape = [jax.ShapeDtypeStruct((n, d), F32), jax.ShapeDtypeStruct((n, d), BF16)]
    return pl.pallas_call(
        kern, grid=(n // tm,), in_specs=in_specs, out_specs=out_specs, out_shape=out_shape,
        compiler_params=_cparams(1), name="out_proj",
    )(*lhs, w, x, g)


def _ffn_up_kernel(hn_ref, wv_ref, wg_ref, cw_ref, cb_ref, st_ref, act_ref, stout_ref, prev_ref,
                   *, seg, tiles_per_seq):
    i = pl.program_id(1)
    hn = hn_ref[...]
    tm = hn.shape[0]
    tf = act_ref.shape[1]
    if tiles_per_seq > 1:
        @pl.when((i % tiles_per_seq) == 0)
        def _():
            prev_ref[...] = st_ref[0]

    cwid = min(tf, MXU_DIM)
    for c0 in range(0, tf, cwid):
        cs = slice(c0, c0 + cwid)
        gate = _dot(hn, wg_ref[:, cs])
        val = _dot(hn, wv_ref[:, cs])
        cw = cw_ref[:, cs]
        cb = cb_ref[:, cs]
        for s in range(tm // seg):
            g0 = gate[s * seg:(s + 1) * seg]
            prev = st_ref[s, :, cs] if tiles_per_seq == 1 else prev_ref[:, cs]
            ext = jnp.concatenate([prev, g0], axis=0)
            g1 = pltpu.roll(ext, 1, 0)[SUBLANES:]
            g2 = pltpu.roll(ext, 2, 0)[SUBLANES:]
            c = cb + cw[0:1] * g2 + cw[1:2] * g1 + cw[2:3] * g0
            act_ref[s * seg:(s + 1) * seg, cs] = (jax.nn.silu(c) * val[s * seg:(s + 1) * seg]).astype(BF16)
            last = g0[seg - SUBLANES:seg]
            stout_ref[s, :, cs] = last
            if tiles_per_seq > 1:
                prev_ref[:, cs] = last


def _ffn_up(hn, w_in, layer, conv_w, conv_b, conv_state, seq_len):
    n, d = hn.shape
    f = w_in.shape[2] // 2
    nseq = n // seq_len
    tf = max(t for t in range(MXU_DIM, f + 1, MXU_DIM) if f % t == 0 and 4 * d * t <= FFN_WEIGHT_BYTES)
    tm = _tile(n, 1024)
    nf = f // tf
    wspec = lambda col: pl.BlockSpec((None, d, tf), lambda j, i: (layer, 0, col(j)), pipeline_mode=pl.Buffered(1))
    if tm >= seq_len:
        seg, tps, spt = seq_len, 1, tm // seq_len
        st_idx = lambda j, i: (i, 0, j)
    else:
        seg, tps, spt = tm, seq_len // tm, 1
        st_idx = lambda j, i: (i // tps, 0, j)
    assert seg % SUBLANES == 0 and seg >= SUBLANES
    st = jnp.pad(conv_state, ((0, 0), (SUBLANES - (CONV_W - 1), 0), (0, 0)))
    cw = jnp.pad(conv_w, ((0, SUBLANES - CONV_W), (0, 0)))
    cb = conv_b.reshape(1, f)
    kern = functools.partial(_ffn_up_kernel, seg=seg, tiles_per_seq=tps)
    act, st_out = pl.pallas_call(
        kern,
        grid=(nf, n // tm),
        in_specs=[pl.BlockSpec((tm, d), lambda j, i: (i, 0)),
                  wspec(lambda j: j), wspec(lambda j: nf + j),
                  pl.BlockSpec((SUBLANES, tf), lambda j, i: (0, j)),
                  pl.BlockSpec((1, tf), lambda j, i: (0, j)),
                  pl.BlockSpec((spt, SUBLANES, tf), st_idx)],
        out_specs=[pl.BlockSpec((tm, tf), lambda j, i: (i, j)),
                   pl.BlockSpec((spt, SUBLANES, tf), st_idx)],
        out_shape=[jax.ShapeDtypeStruct((n, f), BF16), jax.ShapeDtypeStruct((nseq, SUBLANES, f), F32)],
        scratch_shapes=[pltpu.VMEM((SUBLANES, tf), F32)],
        compiler_params=_cparams(2),
        name="ffn_up",
    )(hn, w_in, w_in, cw, cb, st)
    return act, st_out[:, SUBLANES - (CONV_W - 1):, :]


def _rope_pair(x, cos, sin):
    return x * cos + pltpu.roll(x, MLA_ROPE, 1) * sin


def _odd_in_kernel(hn_ref, win_ref, gq_ref, gkv_ref, wuq_ref, cos_ref, sin_ref, *rest,
                   qr, kvr, heads, qscale, expand):
    if expand:
        wk_ref, wvt_ref, q_ref, ckv_ref, ckvb_ref, kpe_ref, kpeb_ref, kn_ref, vt_ref = rest
    else:
        q_ref, ckv_ref, ckvb_ref, kpe_ref, kpeb_ref = rest
    hn = hn_ref[...]
    cos = cos_ref[...]
    sin = sin_ref[...]
    cq = _rms(_dot(hn, win_ref[:, 0:qr]), gq_ref[...]).astype(BF16)
    ckv = _rms(_dot(hn, win_ref[:, qr:qr + kvr]), gkv_ref[...])
    kpe = _rope_pair(_dot(hn, win_ref[:, qr + kvr:qr + kvr + LANES]), cos, sin)
    ckv_ref[...] = ckv
    ckvb = ckv.astype(BF16)
    ckvb_ref[...] = ckvb
    kpe_ref[...] = kpe[:, 0:MLA_ROPE]
    kpeb_ref[...] = kpe.astype(BF16)
    hw = MXU_DIM
    for h in range(heads):
        qh = _dot(cq, wuq_ref[:, h * hw:(h + 1) * hw]) * qscale
        q_ref[:, h * hw:h * hw + LANES] = qh[:, 0:LANES].astype(BF16)
        q_ref[:, h * hw + LANES:(h + 1) * hw] = _rope_pair(qh[:, LANES:hw], cos, sin).astype(BF16)
    if expand:
        kn_ref[...] = _dot(ckvb, wk_ref[...]).astype(BF16)
        vt_ref[...] = _dot_t(wvt_ref[...], ckvb).astype(BF16)


def _odd_in(hn, win, gq, gkv, wuq, cos, sin, wkv, seq_len, heads):
    n, d = hn.shape
    qr, kvr = gq.shape[1], gkv.shape[1]
    tm = _tile(n, 512)
    row = lambda i: (i, 0)
    if tm <= seq_len:
        tps = seq_len // tm
        pos = lambda i: (i % tps, 0)
    else:
        cos = jnp.tile(cos, (tm // seq_len, 1))
        sin = jnp.tile(sin, (tm // seq_len, 1))
        pos = lambda i: (0, 0)
    expand = wkv is not None
    kern = functools.partial(_odd_in_kernel, qr=qr, kvr=kvr, heads=heads,
                             qscale=(MLA_NOPE + MLA_ROPE) ** -0.5 * LOG2E, expand=expand)
    in_specs = [pl.BlockSpec((tm, d), row), _resident(win.shape), _resident(gq.shape), _resident(gkv.shape),
                _resident(wuq.shape), pl.BlockSpec((tm, LANES), pos), pl.BlockSpec((tm, LANES), pos)]
    args = [hn, win, gq, gkv, wuq, cos, sin]
    out_specs = [pl.BlockSpec((tm, heads * MXU_DIM), row), pl.BlockSpec((tm, kvr), row),
                 pl.BlockSpec((tm, kvr), row), pl.BlockSpec((tm, MLA_ROPE), row), pl.BlockSpec((tm, LANES), row)]
    out_shape = [jax.ShapeDtypeStruct((n, heads * MXU_DIM), BF16), jax.ShapeDtypeStruct((n, kvr), F32),
                 jax.ShapeDtypeStruct((n, kvr), BF16), jax.ShapeDtypeStruct((n, MLA_ROPE), F32),
                 jax.ShapeDtypeStruct((n, LANES), BF16)]
    if expand:
        in_specs += [_resident(wkv[0].shape), _resident(wkv[1].shape)]
        args += list(wkv)
        out_specs += [pl.BlockSpec((tm, heads * MLA_NOPE), row),
                      pl.BlockSpec((heads * MLA_V, tm), lambda i: (0, i))]
        out_shape += [jax.ShapeDtypeStruct((n, heads * MLA_NOPE), BF16),
                      jax.ShapeDtypeStruct((heads * MLA_V, n), BF16)]
    return pl.pallas_call(
        kern, grid=(n // tm,), in_specs=in_specs, out_specs=out_specs, out_shape=out_shape,
        compiler_params=_cparams(1), name="odd_in",
    )(*args)


def _mla_prompt_kernel(q_ref, kn_ref, kpe_ref, vt_ref, o_ref, kx_ref, vxt_ref, m_ref, a_ref, *, tq, hp):
    i = pl.program_id(2)
    hw = MXU_DIM

    @pl.when(i == 0)
    def _():
        for j in range(hp):
            kx_ref[j, :, 0:MLA_NOPE] = kn_ref[:, j * MLA_NOPE:(j + 1) * MLA_NOPE]
            kx_ref[j, :, MLA_NOPE:hw] = kpe_ref[...]
            for kt in range(vxt_ref.shape[1]):
                vxt_ref[j, kt, 0:MLA_V, :] = vt_ref[j * MLA_V:(j + 1) * MLA_V, kt * tq:(kt + 1) * tq]
                vxt_ref[j, kt, MLA_V:VX_ROWS, :] = _unit_rows(tq)

    m_ref[...] = jnp.full(m_ref.shape, NEG_INF, F32)
    a_ref[...] = jnp.zeros(a_ref.shape, F32)

    def streams_at(kt):
        sl = pl.ds(pl.multiple_of(kt * tq, tq), tq)
        return [(kx_ref[j, sl, :], q_ref[:, j * hw:(j + 1) * hw], vxt_ref[j, kt], m_ref.at[j], a_ref.at[j])
                for j in range(hp)]

    _flash_sweep(i, streams_at, tq)
    for j in range(hp):
        a = a_ref[j]
        o_ref[:, j * MLA_V:(j + 1) * MLA_V] = (a[0:MLA_V] / a[MLA_V:MLA_V + 1]).T.astype(BF16)


def _mla_prompt(q, kn, kpeb, vt, nbatch, seq_len, heads):
    n = q.shape[0]
    tq = _tile(seq_len, 512)
    hp = 4
    assert tq % CHUNK == 0 and heads % hp == 0 and MLA_V == LANES
    nq = seq_len // tq
    kern = functools.partial(_mla_prompt_kernel, tq=tq, hp=hp)
    return pl.pallas_call(
        kern,
        grid=(nbatch, heads // hp, nq),
        in_specs=[pl.BlockSpec((tq, hp * MXU_DIM), lambda b, h, i: (b * nq + i, h)),
                  pl.BlockSpec((seq_len, hp * MLA_NOPE), lambda b, h, i: (b, h)),
                  pl.BlockSpec((seq_len, LANES), lambda b, h, i: (b, 0)),
                  pl.BlockSpec((hp * MLA_V, seq_len), lambda b, h, i: (h, b))],
        out_specs=pl.BlockSpec((tq, hp * MLA_V), lambda b, h, i: (b * nq + i, h)),
        out_shape=jax.ShapeDtypeStruct((n, heads * MLA_V), BF16),
        scratch_shapes=[pltpu.VMEM((hp, seq_len, MXU_DIM), BF16),
                        pltpu.VMEM((hp, nq, VX_ROWS, tq), BF16),
                        pltpu.VMEM((hp, 1, tq), F32), pltpu.VMEM((hp, VX_ROWS, tq), F32)],
        compiler_params=_cparams(3),
        name="mla_attn_prompt",
    )(q, kn, kpeb, vt)


def _mla_sample_kernel(q_ref, cc_ref, pc_ref, cn_ref, pn_ref, wk_ref, wv_ref, o_ref, ql_ref, qp_ref,
                       *, heads, past):
    tq = q_ref.shape[0]
    hw = MXU_DIM
    for h in range(heads):
        qn = q_ref[:, h * hw:h * hw + MLA_NOPE]
        ql_ref[h * tq:(h + 1) * tq, :] = _dot(qn, wk_ref[h]).astype(BF16)
        qp_ref[h * tq:(h + 1) * tq, :] = q_ref[:, h * hw + MLA_NOPE:(h + 1) * hw]
    ql = ql_ref[...]
    qp = qp_ref[...]
    cc = cc_ref[...].astype(BF16)
    pc = pc_ref[...].astype(BF16)
    cn = cn_ref[...]
    mask_n = jnp.concatenate([_chunk_mask(past, past, tq, tq)] * heads, axis=0)
    sc = _dot_t(ql, cc) + _dot_t(qp[:, 0:MLA_ROPE], pc)
    sn = jnp.where(mask_n, _dot_t(ql, cn) + _dot_t(qp, pn_ref[...]), NEG_INF)
    m = jnp.maximum(jnp.max(sc, axis=-1, keepdims=True), jnp.max(sn, axis=-1, keepdims=True))
    ec = jnp.exp2(sc - m)
    en = jnp.exp2(sn - m)
    l = jnp.sum(ec, axis=-1, keepdims=True) + jnp.sum(en, axis=-1, keepdims=True)
    ol = ((_dot(ec.astype(BF16), cc) + _dot(en.astype(BF16), cn)) / l).astype(BF16)
    for h in range(heads):
        o_ref[:, h * MLA_V:(h + 1) * MLA_V] = _dot(ol[h * tq:(h + 1) * tq], wv_ref[h]).astype(BF16)


def _mla_sample(q, ckvb, kpeb, cache_ckv, cache_kpe, wk_t, wv, nbatch, seq_len, heads):
    n = q.shape[0]
    past, kvr = cache_ckv.shape[1], cache_ckv.shape[2]
    assert (past // CHUNK) * CHUNK == past
    cc = cache_ckv.reshape(nbatch * past, kvr)
    pc = cache_kpe.reshape(nbatch * past, MLA_ROPE)
    kern = functools.partial(_mla_sample_kernel, heads=heads, past=past)
    row = lambda b: (b, 0)
    return pl.pallas_call(
        kern,
        grid=(nbatch,),
        in_specs=[pl.BlockSpec((seq_len, heads * MXU_DIM), row),
                  pl.BlockSpec((past, kvr), row), pl.BlockSpec((past, MLA_ROPE), row),
                  pl.BlockSpec((seq_len, kvr), row), pl.BlockSpec((seq_len, LANES), row),
                  _resident(wk_t.shape), _resident(wv.shape)],
        out_specs=pl.BlockSpec((seq_len, heads * MLA_V), row),
        out_shape=jax.ShapeDtypeStruct((n, heads * MLA_V), BF16),
        scratch_shapes=[pltpu.VMEM((heads * seq_len, kvr), BF16), pltpu.VMEM((heads * seq_len, LANES), BF16)],
        compiler_params=_cparams(1),
        name="mla_attn_sample",
    )(q, cc, pc, ckvb, kpeb, wk_t, wv)


def _rope_tables(pos):
    half = MLA_ROPE // 2
    inv = ROPE_BASE ** (-jnp.arange(half, dtype=F32) / half)
    ang = pos.astype(F32)[:, None] * inv[None, :]
    cos, sin = jnp.cos(ang), jnp.sin(ang)
    zero = jnp.zeros_like(cos)
    return (jnp.concatenate([cos, cos, zero, zero], axis=1),
            jnp.concatenate([-sin, sin, zero, zero], axis=1))


def _swap_halves(w):
    half = MLA_ROPE // 2
    return jnp.concatenate([w[..., half:], w[..., :half]], axis=-1)


def _prepare_weights(p):
    w = {}
    heads = p['mla_w_uq'].shape[2]
    w['heads'] = heads
    w['norm_mix'] = p['norm_mix'][:, None, :]
    w['norm_ffn'] = p['norm_ffn'][:, None, :]
    w['norm_final'] = p['norm_final'][None, :]
    w['w_in_even'] = p['w_in_even'][0].astype(BF16)
    w['w_out_even'] = p['w_out_even'][0].astype(BF16)
    w['s5'] = dict(a_re=p['s5_a_re'][0], a_im=p['s5_a_im'][0], b_re=p['s5_b_re'][0], b_im=p['s5_b_im'][0],
                   c_re=p['s5_c_re'][0], c_im=p['s5_c_im'][0], d=p['s5_d'][0], log_dt=p['s5_log_dt'][0],
                   w_glu=p['s5_w_glu'][0], b_glu=p['s5_b_glu'][0])
    w['lam'] = [p[k][0][None, :] for k in ('diff_lambda_q1', 'diff_lambda_k1', 'diff_lambda_q2', 'diff_lambda_k2')]
    w['subln'] = p['diff_subln'][0][None, :]
    wi = p['w_in_odd'][0]
    qr = p['mla_q_norm'].shape[1]
    kvr = p['mla_kv_norm'].shape[1]
    wpe = wi[:, qr + kvr:]
    w['w_in_odd'] = jnp.concatenate([wi[:, :qr + kvr], wpe, _swap_halves(wpe)], axis=1).astype(BF16)
    w['gq'] = p['mla_q_norm'][0][None, :]
    w['gkv'] = p['mla_kv_norm'][0][None, :]
    wuq = p['mla_w_uq'][0]
    wuq = jnp.concatenate([wuq, _swap_halves(wuq[..., MLA_NOPE:])], axis=-1)
    w['w_uq'] = wuq.reshape(qr, heads * MXU_DIM).astype(BF16)
    wukv = p['mla_w_ukv'][0]
    w['w_kv'] = (wukv[..., :MLA_NOPE].reshape(kvr, heads * MLA_NOPE).astype(BF16),
                 wukv[..., MLA_NOPE:].reshape(kvr, heads * MLA_V).T.astype(BF16))
    w['w_uk_t'] = jnp.transpose(wukv[..., :MLA_NOPE], (1, 2, 0)).astype(BF16)
    w['w_uv'] = jnp.transpose(wukv[..., MLA_NOPE:], (1, 0, 2)).astype(BF16)
    w['w_out_odd'] = p['w_out_odd'][0].astype(BF16)
    w['ffn_w_in'] = p['ffn_w_in'].astype(BF16)
    w['ffn_w_down'] = p['ffn_w_down'].astype(BF16)
    w['ffn_conv_w'] = p['ffn_conv_w']
    w['ffn_conv_b'] = p['ffn_conv_b']
    return w


def _trunk(x3, s5_re0, s5_im0, k_past, v_past, ckv_past, kpe_past, conv0, pos0, w):
    nb, t, d = x3.shape
    n = nb * t
    x = x3.reshape(n, d)
    heads = w['heads']
    s5w = w['s5']['d'].size
    dw = (w['w_in_even'].shape[1] - s5w) // 3
    dheads = dw // LANES

    lam_init = 0.8 - 0.6 * math.exp(-0.3 * 0)
    u, q, k, v, kb, vb = _even_in(x, w['norm_mix'][0], w['w_in_even'], s5w, dw, v_transposed=k_past is None)
    y_s5, ht_re, ht_im = _s5(u, s5_re0, s5_im0, t, w['s5'])
    if k_past is None:
        o = _diff_prompt(q, kb, vb, w['lam'], w['subln'], nb, t, lam_init)
    else:
        o = _diff_sample(q, kb, vb, k_past, v_past, w['lam'], w['subln'], nb, t, lam_init)
    x, hn = _out_proj([y_s5, o], w['w_out_even'], x, w['norm_ffn'][0])
    act, conv_a = _ffn_up(hn, w['ffn_w_in'], 0, w['ffn_conv_w'][0], w['ffn_conv_b'][0], conv0[0], t)
    x, hn = _out_proj([act], w['ffn_w_down'], x, w['norm_mix'][1], tm_pref=256, layer=0)

    cos, sin = _rope_tables(pos0 + jnp.arange(t, dtype=jnp.int32))
    if ckv_past is None:
        qm, ckv, ckvb, kpe, kpeb, kn, vm = _odd_in(hn, w['w_in_odd'], w['gq'], w['gkv'], w['w_uq'], cos, sin,
                                                   w['w_kv'], t, heads)
        om = _mla_prompt(qm, kn, kpeb, vm, nb, t, heads)
    else:
        qm, ckv, ckvb, kpe, kpeb = _odd_in(hn, w['w_in_odd'], w['gq'], w['gkv'], w['w_uq'], cos, sin,
                                           None, t, heads)
        om = _mla_sample(qm, ckvb, kpeb, ckv_past, kpe_past, w['w_uk_t'], w['w_uv'], nb, t, heads)
    x, hn = _out_proj([om], w['w_out_odd'], x, w['norm_ffn'][1])
    act, conv_b = _ffn_up(hn, w['ffn_w_in'], 1, w['ffn_conv_w'][1], w['ffn_conv_b'][1], conv0[1], t)
    (y,) = _out_proj([act], w['ffn_w_down'], x, w['norm_final'], final=True, tm_pref=256, layer=1)

    groups, n_state = w['s5']['a_re'].shape
    return (y.reshape(nb, t, d), ht_re[None], ht_im[None],
            k.reshape(1, nb, t, dheads, LANES), v.reshape(1, nb, t, dheads, LANES),
            ckv.reshape(1, nb, t, -1), kpe.reshape(1, nb, t, MLA_ROPE), jnp.stack([conv_a, conv_b]))


def kernel(x_prompt, x_sample, state_s5_re, state_s5_im, cache_diff_k, cache_diff_v, cache_mla_ckv, cache_mla_kpe, state_ffn_conv, norm_mix, norm_ffn, norm_final, w_in_even, w_out_even, s5_a_re, s5_a_im, s5_b_re, s5_b_im, s5_c_re, s5_c_im, s5_d, s5_log_dt, s5_w_glu, s5_b_glu, diff_lambda_q1, diff_lambda_k1, diff_lambda_q2, diff_lambda_k2, diff_subln, w_in_odd, mla_q_norm, mla_kv_norm, mla_w_uq, mla_w_ukv, w_out_odd, ffn_w_in, ffn_conv_w, ffn_conv_b, ffn_w_down):
    w = _prepare_weights(dict(
        norm_mix=norm_mix, norm_ffn=norm_ffn, norm_final=norm_final, w_in_even=w_in_even, w_out_even=w_out_even,
        s5_a_re=s5_a_re, s5_a_im=s5_a_im, s5_b_re=s5_b_re, s5_b_im=s5_b_im, s5_c_re=s5_c_re, s5_c_im=s5_c_im,
        s5_d=s5_d, s5_log_dt=s5_log_dt, s5_w_glu=s5_w_glu, s5_b_glu=s5_b_glu,
        diff_lambda_q1=diff_lambda_q1, diff_lambda_k1=diff_lambda_k1, diff_lambda_q2=diff_lambda_q2,
        diff_lambda_k2=diff_lambda_k2, diff_subln=diff_subln, w_in_odd=w_in_odd, mla_q_norm=mla_q_norm,
        mla_kv_norm=mla_kv_norm, mla_w_uq=mla_w_uq, mla_w_ukv=mla_w_ukv, w_out_odd=w_out_odd,
        ffn_w_in=ffn_w_in, ffn_conv_w=ffn_conv_w, ffn_conv_b=ffn_conv_b, ffn_w_down=ffn_w_down))
    nb_p = x_prompt.shape[0]
    groups, n_state = s5_a_re.shape[1:]
    d_ff = ffn_conv_b.shape[1]
    depth = ffn_conv_b.shape[0]
    s5_zero = jnp.zeros((nb_p, groups, n_state), F32)
    conv_zero = jnp.zeros((depth, nb_p, CONV_W - 1, d_ff), F32)
    (y_p, re_p, im_p, k_p, v_p, ckv_p, kpe_p, conv_p) = _trunk(
        x_prompt, s5_zero, s5_zero, None, None, None, None, conv_zero, 0, w)
    past = cache_diff_k.shape[2]
    (y_s, re_s, im_s, k_s, v_s, ckv_s, kpe_s, conv_s) = _trunk(
        x_sample, state_s5_re[0], state_s5_im[0], cache_diff_k[0], cache_diff_v[0], cache_mla_ckv[0],
        cache_mla_kpe[0], state_ffn_conv, past, w)
    return (y_p, y_s, re_p, im_p, re_s, im_s, k_p, v_p, k_s, v_s, ckv_p, kpe_p, ckv_s, kpe_s, conv_p, conv_s)
```

```python
import functools
import math

import jax
import jax.numpy as jnp
from jax import lax
from jax.experimental import pallas as pl
from jax.experimental.pallas import tpu as pltpu

F32 = jnp.float32
BF16 = jnp.bfloat16

CHUNK = 64
EPS = 1e-6
NEG_INF = -1e30
ROPE_BASE = 10000.0
S5_GROUP = 16
DIFF_DK = 64
MLA_NOPE = 128
MLA_ROPE = 64
MLA_V = 128
CONV_W = 3
LOG2E = 1.4426950408889634

LANES = 128
SUBLANES = 8
MXU_DIM = 256
VMEM_LIMIT_BYTES = 56 * 1024 * 1024

FFN_WEIGHT_BYTES = 24 * 1024 * 1024
FLASH_TQ = 1024
FLASH_TK = 512
FLASH_COLS = 256
VX_ROWS = LANES + 16
S5_L = 16
S5_GB = MXU_DIM // S5_GROUP


def _cparams(n_axes):
    return pltpu.CompilerParams(
        dimension_semantics=("arbitrary",) * n_axes,
        vmem_limit_bytes=VMEM_LIMIT_BYTES)


def _resident(shape):
    nd = len(shape)
    return pl.BlockSpec(shape, lambda *_: (0,) * nd, pipeline_mode=pl.Buffered(1))


def _tile(n, pref):
    t = min(n, pref)
    while n % t:
        t //= 2
    return t


def _dot(a, b):
    return jnp.dot(a, b, preferred_element_type=F32)


def _dot_t(a, b):
    return lax.dot_general(a, b, (((1,), (1,)), ((), ())), preferred_element_type=F32)


def _rms(x, g):
    ms = jnp.mean(x * x, axis=-1, keepdims=True)
    return x * lax.rsqrt(ms + EPS) * g


def _even_in_kernel(x_ref, g_ref, w_ref, u_ref, q_ref, k_ref, v_ref, kb_ref, vb_ref,
                    *, s5w, dw, qscale, v_transposed):
    xn = _rms(x_ref[...], g_ref[...]).astype(BF16)
    u_ref[...] = _dot(xn, w_ref[:, 0:s5w])
    q_ref[...] = (_dot(xn, w_ref[:, s5w:s5w + dw]) * qscale).astype(BF16)
    k = _dot(xn, w_ref[:, s5w + dw:s5w + 2 * dw])
    k_ref[...] = k
    kb_ref[...] = k.astype(BF16)
    v = _dot(xn, w_ref[:, s5w + 2 * dw:s5w + 3 * dw])
    v_ref[...] = v
    vb_ref[...] = (v.T if v_transposed else v).astype(BF16)


def _even_in(x, g, w, s5w, dw, v_transposed):
    n, d = x.shape
    tm = _tile(n, 512)
    row = lambda i: (i, 0)
    kern = functools.partial(_even_in_kernel, s5w=s5w, dw=dw, qscale=DIFF_DK ** -0.5 * LOG2E,
                             v_transposed=v_transposed)
    vb_spec = pl.BlockSpec((dw, tm), lambda i: (0, i)) if v_transposed else pl.BlockSpec((tm, dw), row)
    return pl.pallas_call(
        kern,
        grid=(n // tm,),
        in_specs=[pl.BlockSpec((tm, d), row), _resident(g.shape), _resident(w.shape)],
        out_specs=[pl.BlockSpec((tm, s5w), row), pl.BlockSpec((tm, dw), row), pl.BlockSpec((tm, dw), row),
                   pl.BlockSpec((tm, dw), row), pl.BlockSpec((tm, dw), row), vb_spec],
        out_shape=[jax.ShapeDtypeStruct((n, s5w), F32), jax.ShapeDtypeStruct((n, dw), BF16),
                   jax.ShapeDtypeStruct((n, dw), F32), jax.ShapeDtypeStruct((n, dw), F32),
                   jax.ShapeDtypeStruct((n, dw), BF16),
                   jax.ShapeDtypeStruct((dw, n) if v_transposed else (n, dw), BF16)],
        compiler_params=_cparams(1),
        name="even_in",
    )(x, g, w)


def _s5_kernel(u_ref, h0_ref, lre_ref, lim_ref, dt_ref, bre_ref, bim_ref, cbd_ref, d_ref, wglu_ref, bglu_ref,
               y_ref, ht_ref,
               a_ref, a16_ref, bbd_ref, us_ref, xs_ref, xe_ref, hs_ref, carry_ref, ys_ref, yn_ref,
               *, rows, width, cps, ngb):
    t = pl.program_id(0)
    nblk = ngb * 16
    half = 8
    gw = MXU_DIM

    @pl.when(t == 0)
    def _prepare():
        carry_ref[...] = jnp.zeros_like(carry_ref)
        for gb in range(ngb):
            for k in range(half):
                lre = jnp.minimum(lre_ref[gb * half + k], -1e-4)
                lim = lim_ref[gb * half + k]
                dt = jnp.exp(dt_ref[gb * half + k])
                mag = jnp.exp(lre * dt)
                are = mag * jnp.cos(lim * dt)
                aim = mag * jnp.sin(lim * dt)
                den = lre * lre + lim * lim
                cre = ((are - 1.0) * lre + aim * lim) / den
                cim = (aim * lre - (are - 1.0) * lim) / den
                ire, iim = gb * 16 + k, gb * 16 + half + k
                a_ref[ire] = are
                a_ref[iim] = aim
                pre, pim = are, aim
                for _ in range(4):
                    pre, pim = pre * pre - pim * pim, 2.0 * pre * pim
                a16_ref[ire] = pre
                a16_ref[iim] = pim
                br = bre_ref[gb, :, k * LANES:(k + 1) * LANES]
                bi = bim_ref[gb, :, k * LANES:(k + 1) * LANES]
                bbd_ref[gb, :, k * LANES:(k + 1) * LANES] = (cre * br - cim * bi).astype(BF16)
                bbd_ref[gb, :, (half + k) * LANES:(half + k + 1) * LANES] = (cre * bi + cim * br).astype(BF16)

    ncb = width // LANES
    for cb in range(ncb):
        us_ref[cb] = u_ref[:, cb * LANES:(cb + 1) * LANES]

    def step(l):
        return pl.ds(l * rows, rows)

    def at_step(l):
        return pl.ds(l, rows, stride=S5_L)

    for gb in range(ngb):
        cbs = range(gb * gw // LANES, (gb + 1) * gw // LANES)
        lhs = jnp.concatenate(
            [jnp.concatenate([us_ref[cb, at_step(l), :] for cb in cbs], axis=1) for l in range(S5_L)], axis=0)
        x = _dot(lhs.astype(BF16), bbd_ref[gb])
        for j in range(16):
            xs_ref[gb * 16 + j] = x[:, j * LANES:(j + 1) * LANES]
        for k in range(half):
            ire, iim = gb * 16 + k, gb * 16 + half + k
            are, aim = a_ref[ire], a_ref[iim]
            hre = xs_ref[ire, step(0), :]
            him = xs_ref[iim, step(0), :]
            for l in range(1, S5_L):
                xre = xs_ref[ire, step(l), :]
                xim = xs_ref[iim, step(l), :]
                hre, him = are * hre - aim * him + xre, are * him + aim * hre + xim
            xe_ref[ire] = hre
            xe_ref[iim] = him

    base_row = t * rows
    for grp in range(ngb * half // 8):
        ire0 = (grp // (half // 8)) * 16 + (grp % (half // 8)) * 8
        iim0 = ire0 + half
        a16re = a16_ref[ire0:ire0 + 8]
        a16im = a16_ref[iim0:iim0 + 8]

        cre = carry_ref[ire0:ire0 + 8]
        cim = carry_ref[iim0:iim0 + 8]
        for r in range(rows):
            if cps >= rows:
                if r == 0:
                    is_start = (base_row % cps) == 0
                    cre = jnp.where(is_start, h0_ref[0, ire0:ire0 + 8], cre)
                    cim = jnp.where(is_start, h0_ref[0, iim0:iim0 + 8], cim)
            elif r % cps == 0:
                cre = h0_ref[r // cps, ire0:ire0 + 8]
                cim = h0_ref[r // cps, iim0:iim0 + 8]
            hs_ref[ire0:ire0 + 8, r:r + 1, :] = cre
            hs_ref[iim0:iim0 + 8, r:r + 1, :] = cim
            xre = xe_ref[ire0:ire0 + 8, r:r + 1, :]
            xim = xe_ref[iim0:iim0 + 8, r:r + 1, :]
            cre, cim = a16re * cre - a16im * cim + xre, a16re * cim + a16im * cre + xim
            if cps < rows and r % cps == cps - 1:
                ht_ref[r // cps, ire0:ire0 + 8] = cre
                ht_ref[r // cps, iim0:iim0 + 8] = cim
        if cps >= rows:
            ht_ref[0, ire0:ire0 + 8] = cre
            ht_ref[0, iim0:iim0 + 8] = cim
        carry_ref[ire0:ire0 + 8] = cre
        carry_ref[iim0:iim0 + 8] = cim

    for gb in range(ngb):
        for k in range(half):
            ire, iim = gb * 16 + k, gb * 16 + half + k
            are, aim = a_ref[ire], a_ref[iim]
            hre = hs_ref[ire]
            him = hs_ref[iim]
            for l in range(S5_L):
                xre = xs_ref[ire, step(l), :]
                xim = xs_ref[iim, step(l), :]
                hre, him = are * hre - aim * him + xre, are * him + aim * hre + xim
                xs_ref[ire, step(l), :] = hre
                xs_ref[iim, step(l), :] = him
        h = jnp.concatenate([xs_ref[gb * 16 + j].astype(BF16) for j in range(16)], axis=1)
        yc = _dot(h, cbd_ref[gb])
        for c in range(gw // LANES):
            ys_ref[gb * gw // LANES + c] = yc[:, c * LANES:(c + 1) * LANES]

    for cb in range(ncb):
        for r in range(rows):
            yn_ref[r * S5_L:(r + 1) * S5_L, cb * LANES:(cb + 1) * LANES] = ys_ref[cb, pl.ds(r, S5_L, stride=rows), :]

    y = yn_ref[...] + d_ref[...] * u_ref[...]
    z = jax.nn.gelu(y)
    gate = jax.nn.sigmoid(_dot(z.astype(BF16), wglu_ref[...]) + bglu_ref[...])
    y_ref[...] = (z * gate).astype(BF16)


def _s5_layout(p, ngb, n_state):
    return p.reshape(ngb * (S5_GB * n_state // LANES), 1, LANES)


def _s5_state_to_blocks(s, ngb):
    b = s.shape[0]
    return s.reshape(b, ngb, 8, 1, LANES)


def _s5(u, h0_re, h0_im, seq_len, w):
    n, width = u.shape
    groups, n_state = w['a_re'].shape
    assert n_state * S5_GB == 8 * LANES and width == groups * S5_GROUP
    ngb = groups // S5_GB
    nseq = n // seq_len
    cps = seq_len // S5_L
    nrow = n // S5_L
    rows = _tile(nrow, 32)
    assert rows % SUBLANES == 0 and (cps % rows == 0 or rows % cps == 0)
    spt = max(1, rows // cps)
    tps = max(1, cps // rows)
    nblk = ngb * 16
    sw = S5_GB * n_state

    h0 = jnp.concatenate([_s5_state_to_blocks(h0_re, ngb), _s5_state_to_blocks(h0_im, ngb)], axis=2)
    h0 = h0.reshape(nseq, nblk, 1, LANES)
    lre = _s5_layout(w['a_re'], ngb, n_state)
    lim = _s5_layout(w['a_im'], ngb, n_state)
    dt = _s5_layout(jnp.broadcast_to(w['log_dt'][:, None], (groups, n_state)), ngb, n_state)
    eye = jnp.eye(S5_GB, dtype=F32)

    def bdiag_b(b):
        bb = b.reshape(ngb, S5_GB, n_state, S5_GROUP)
        return jnp.einsum('agnp,gh->agphn', bb, eye).reshape(ngb, S5_GB * S5_GROUP, sw)

    def bdiag_c(c):
        cc = c.reshape(ngb, S5_GB, S5_GROUP, n_state)
        return jnp.einsum('agpn,gh->ahngp', cc, eye).reshape(ngb, sw, S5_GB * S5_GROUP)

    bre = bdiag_b(w['b_re'])
    bim = bdiag_b(w['b_im'])
    cbd = jnp.concatenate([bdiag_c(w['c_re']), -bdiag_c(w['c_im'])], axis=1).astype(BF16)
    d = w['d'].reshape(1, width)
    wglu = w['w_glu'].astype(BF16)
    bglu = w['b_glu'].reshape(1, width)

    seq_idx = (lambda t: (t // tps, 0, 0, 0)) if tps > 1 else (lambda t: (t, 0, 0, 0))
    kern = functools.partial(_s5_kernel, rows=rows, width=width, cps=cps, ngb=ngb)
    y, ht = pl.pallas_call(
        kern,
        grid=(nrow // rows,),
        in_specs=[pl.BlockSpec((S5_L * rows, width), lambda t: (t, 0)),
                  pl.BlockSpec((spt, nblk, 1, LANES), seq_idx),
                  _resident(lre.shape), _resident(lim.shape), _resident(dt.shape),
                  _resident(bre.shape), _resident(bim.shape), _resident(cbd.shape),
                  _resident(d.shape), _resident(wglu.shape), _resident(bglu.shape)],
        out_specs=[pl.BlockSpec((S5_L * rows, width), lambda t: (t, 0)),
                   pl.BlockSpec((spt, nblk, 1, LANES), seq_idx)],
        out_shape=[jax.ShapeDtypeStruct((n, width), BF16),
                   jax.ShapeDtypeStruct((nseq, nblk, 1, LANES), F32)],
        scratch_shapes=[pltpu.VMEM((nblk, 1, LANES), F32),
                        pltpu.VMEM((nblk, 1, LANES), F32),
                        pltpu.VMEM((ngb, MXU_DIM, 2 * sw), BF16),
                        pltpu.VMEM((width // LANES, S5_L * rows, LANES), F32),
                        pltpu.VMEM((nblk, S5_L * rows, LANES), F32),
                        pltpu.VMEM((nblk, rows, LANES), F32),
                        pltpu.VMEM((nblk, rows, LANES), F32),
                        pltpu.VMEM((nblk, 1, LANES), F32),
                        pltpu.VMEM((width // LANES, S5_L * rows, LANES), F32),
                        pltpu.VMEM((S5_L * rows, width), F32)],
        compiler_params=_cparams(1),
        name="s5_mix",
    )(u, h0, lre, lim, dt, bre, bim, cbd, d, wglu, bglu)
    ht = ht.reshape(nseq, ngb, 2, groups // ngb, n_state)
    ht_re = ht[:, :, 0].reshape(nseq, groups, n_state)
    ht_im = ht[:, :, 1].reshape(nseq, groups, n_state)
    return y, ht_re, ht_im


def _diff_lambda(lq1_ref, lk1_ref, lq2_ref, lk2_ref, lam_init):
    s1 = jnp.sum(lq1_ref[...] * lk1_ref[...], axis=-1, keepdims=True)
    s2 = jnp.sum(lq2_ref[...] * lk2_ref[...], axis=-1, keepdims=True)
    return jnp.exp(s1) - jnp.exp(s2) + lam_init


def _split_maps(q):
    lane = lax.broadcasted_iota(jnp.int32, q.shape, 1)
    zero = jnp.zeros_like(q)
    return jnp.where(lane < DIFF_DK, q, zero), jnp.where(lane >= DIFF_DK, q, zero)


def _chunk_mask(q0, k0, tq, tk):
    qc = (q0 + lax.broadcasted_iota(jnp.int32, (tq, tk), 0)) // CHUNK
    kc = (k0 + lax.broadcasted_iota(jnp.int32, (tq, tk), 1)) // CHUNK
    return kc <= qc


def _subln(o, g, lam_init):
    return (_rms(o, g) * (1.0 - lam_init)).astype(BF16)


def _flash_update(blocks):
    work = []
    for streams, koff in blocks:
        chains = []
        for k, q, vxt, m_ref, a_ref in streams:
            tk, tq = k.shape[0], q.shape[0]
            sub = min(tq, FLASH_COLS)
            assert sub % CHUNK == 0
            for c0 in range(0, tq, sub):
                if koff is None or koff + tk <= c0:
                    nk, mask = tk, None
                elif koff >= c0 + sub:
                    continue
                else:
                    nk = min(tk, c0 + sub - koff)
                    kc = (koff + lax.broadcasted_iota(jnp.int32, (nk, sub), 0)) // CHUNK
                    qc = (c0 + lax.broadcasted_iota(jnp.int32, (nk, sub), 1)) // CHUNK
                    mask = kc <= qc
                chains.append((k[0:nk], q, vxt[:, 0:nk], m_ref, a_ref, slice(c0, c0 + sub), mask))
        work.append((chains, [_dot_t(k, q[cs]) for k, q, _, _, _, cs, _ in chains]))
    for chains, scores in work:
        probs = []
        for (_, _, _, m_ref, _, cs, mask), s in zip(chains, scores):
            if mask is not None:
                s = jnp.where(mask, s, NEG_INF)
            m_prev = m_ref[:, cs]
            m_new = jnp.maximum(m_prev, jnp.max(s, axis=0, keepdims=True))
            m_ref[:, cs] = m_new
            probs.append((jnp.exp2(m_prev - m_new), jnp.exp2(s - m_new).astype(BF16)))
        for (_, _, vxt, _, a_ref, cs, _), (alpha, p) in zip(chains, probs):
            a_ref[:, cs] = alpha * a_ref[:, cs] + _dot(vxt, p)


def _flash_sweep(i, streams_at, tq, tk):
    nb = tq // tk

    def full(kt, carry):
        _flash_update([(streams_at(kt), None)])
        return carry

    lax.fori_loop(0, i * nb, full, 0)
    _flash_update([(streams_at(i * nb + d), d * tk) for d in range(nb)])


def _unit_rows(cols):
    return (lax.broadcasted_iota(jnp.int32, (VX_ROWS - LANES, cols), 0) == 0).astype(BF16)


def _diff_prompt_kernel(q_ref, k_ref, vt_ref, lq1_ref, lk1_ref, lq2_ref, lk2_ref, g_ref, o_ref,
                        vxt_ref, m_ref, a_ref, *, tq, tk, hq, lam_init):
    i = pl.program_id(2)

    @pl.when(i == 0)
    def _():
        for j in range(hq):
            for kt in range(vxt_ref.shape[1]):
                vxt_ref[j, kt, 0:LANES, :] = vt_ref[j * LANES:(j + 1) * LANES, kt * tk:(kt + 1) * tk]
                vxt_ref[j, kt, LANES:VX_ROWS, :] = _unit_rows(tk)

    qs = [_split_maps(q_ref[:, j * LANES:(j + 1) * LANES]) for j in range(hq)]
    m_ref[...] = jnp.full(m_ref.shape, NEG_INF, F32)
    a_ref[...] = jnp.zeros(a_ref.shape, F32)

    def streams_at(kt):
        sl = pl.ds(pl.multiple_of(kt * tk, tk), tk)
        streams = []
        for j in range(hq):
            kb = k_ref[sl, j * LANES:(j + 1) * LANES]
            for mp in range(2):
                streams.append((kb, qs[j][mp], vxt_ref[j, kt], m_ref.at[2 * j + mp], a_ref.at[2 * j + mp]))
        return streams

    _flash_sweep(i, streams_at, tq, tk)

    lam = _diff_lambda(lq1_ref, lk1_ref, lq2_ref, lk2_ref, lam_init)
    for j in range(hq):
        a1 = a_ref[2 * j]
        a2 = a_ref[2 * j + 1]
        ot = a1[0:LANES] / a1[LANES:LANES + 1] - lam * (a2[0:LANES] / a2[LANES:LANES + 1])
        ms = jnp.mean(ot * ot, axis=0, keepdims=True)
        ot = ot * lax.rsqrt(ms + EPS) * g_ref[...] * (1.0 - lam_init)
        o_ref[:, j * LANES:(j + 1) * LANES] = ot.T.astype(BF16)


def _diff_prompt(q, kb, vbt, lam_w, g, nbatch, seq_len, lam_init):
    n, dw = q.shape
    gcol = g.reshape(LANES, 1)
    heads = dw // LANES
    tq = _tile(seq_len, FLASH_TQ)
    tk = _tile(tq, FLASH_TK)
    hq = 2
    assert tk % CHUNK == 0 and heads % hq == 0
    nq = seq_len // tq
    kern = functools.partial(_diff_prompt_kernel, tq=tq, tk=tk, hq=hq, lam_init=lam_init)
    vec = lambda a: _resident(a.shape)
    return pl.pallas_call(
        kern,
        grid=(nbatch, heads // hq, nq),
        in_specs=[pl.BlockSpec((tq, hq * LANES), lambda b, h, i: (b * nq + i, h)),
                  pl.BlockSpec((seq_len, hq * LANES), lambda b, h, i: (b, h)),
                  pl.BlockSpec((hq * LANES, seq_len), lambda b, h, i: (h, b)),
                  vec(lam_w[0]), vec(lam_w[1]), vec(lam_w[2]), vec(lam_w[3]), vec(gcol)],
        out_specs=pl.BlockSpec((tq, hq * LANES), lambda b, h, i: (b * nq + i, h)),
        out_shape=jax.ShapeDtypeStruct((n, dw), BF16),
        scratch_shapes=[pltpu.VMEM((hq, seq_len // tk, VX_ROWS, tk), BF16),
                        pltpu.VMEM((2 * hq, 1, tq), F32), pltpu.VMEM((2 * hq, VX_ROWS, tq), F32)],
        compiler_params=_cparams(3),
        name="diff_attn_prompt",
    )(q, kb, vbt, *lam_w, gcol)


def _diff_sample_kernel(q_ref, kc_ref, vc_ref, kn_ref, vn_ref, lq1_ref, lk1_ref, lq2_ref, lk2_ref, g_ref, o_ref,
                        *, past, heads, lam_init):
    tq = q_ref.shape[0]
    mask_n = jnp.concatenate([_chunk_mask(past, past, tq, tq)] * 2, axis=0)
    lam = _diff_lambda(lq1_ref, lk1_ref, lq2_ref, lk2_ref, lam_init)
    for h in range(heads):
        hs = slice(h * LANES, (h + 1) * LANES)
        qm = jnp.concatenate(_split_maps(q_ref[:, hs]), axis=0)
        kc = kc_ref[pl.ds(h, past, stride=heads), :].astype(BF16)
        vc = vc_ref[pl.ds(h, past, stride=heads), :].astype(BF16)
        sc = _dot_t(qm, kc)
        sn = jnp.where(mask_n, _dot_t(qm, kn_ref[:, hs]), NEG_INF)
        m = jnp.maximum(jnp.max(sc, axis=-1, keepdims=True), jnp.max(sn, axis=-1, keepdims=True))
        pc = jnp.exp2(sc - m)
        pn = jnp.exp2(sn - m)
        l = jnp.sum(pc, axis=-1, keepdims=True) + jnp.sum(pn, axis=-1, keepdims=True)
        o = (_dot(pc.astype(BF16), vc) + _dot(pn.astype(BF16), vn_ref[:, hs])) / l
        o_ref[:, hs] = _subln(o[0:tq] - lam * o[tq:2 * tq], g_ref[...], lam_init)


def _diff_sample(q, kb, vb, cache_k, cache_v, lam_w, g, nbatch, seq_len, lam_init):
    n, dw = q.shape
    heads = dw // LANES
    past = cache_k.shape[1]
    assert cache_k.shape[0] == nbatch and (past // CHUNK) * CHUNK == past and heads == SUBLANES
    kc = cache_k.reshape(nbatch * past * heads, LANES)
    vc = cache_v.reshape(nbatch * past * heads, LANES)
    kern = functools.partial(_diff_sample_kernel, past=past, heads=heads, lam_init=lam_init)
    vec = lambda a: _resident(a.shape)
    row = lambda b: (b, 0)
    return pl.pallas_call(
        kern,
        grid=(nbatch,),
        in_specs=[pl.BlockSpec((seq_len, dw), row),
                  pl.BlockSpec((past * heads, LANES), row), pl.BlockSpec((past * heads, LANES), row),
                  pl.BlockSpec((seq_len, dw), row), pl.BlockSpec((seq_len, dw), row),
                  vec(lam_w[0]), vec(lam_w[1]), vec(lam_w[2]), vec(lam_w[3]), vec(g)],
        out_specs=pl.BlockSpec((seq_len, dw), row),
        out_shape=jax.ShapeDtypeStruct((n, dw), BF16),
        compiler_params=_cparams(1),
        name="diff_attn_sample",
    )(q, kc, vc, kb, vb, *lam_w, g)


def _out_proj_kernel(*refs, n_lhs, final):
    lhs = refs[:n_lhs]
    w_ref, x_ref, g_ref = refs[n_lhs:n_lhs + 3]
    outs = refs[n_lhs + 3:]
    acc = x_ref[...]
    off = 0
    for a in lhs:
        kdim = a.shape[1]
        acc = acc + _dot(a[...], w_ref[off:off + kdim, :])
        off += kdim
    if final:
        outs[0][...] = _rms(acc, g_ref[...])
    else:
        outs[0][...] = acc
        outs[1][...] = _rms(acc, g_ref[...]).astype(BF16)


def _out_proj(lhs, w, x, g, final=False, tm_pref=512, layer=None):
    n, d = x.shape
    tm = _tile(n, tm_pref)
    row = lambda i: (i, 0)
    kern = functools.partial(_out_proj_kernel, n_lhs=len(lhs), final=final)
    in_specs = [pl.BlockSpec((tm, a.shape[1]), row) for a in lhs]
    if layer is None:
        wspec = _resident(w.shape)
    else:
        wspec = pl.BlockSpec((None,) + w.shape[1:], lambda *_: (layer, 0, 0), pipeline_mode=pl.Buffered(1))
    in_specs += [wspec, pl.BlockSpec((tm, d), row), _resident(g.shape)]
    if final:
        out_specs = [pl.BlockSpec((tm, d), row)]
        out_shape = [jax.ShapeDtypeStruct((n, d), F32)]
    else:
        out_specs = [pl.BlockSpec((tm, d), row), pl.BlockSpec((tm, d), row)]
        out_shape = [jax.ShapeDtypeStruct((n, d), F32), jax.ShapeDtypeStruct((n, d), BF16)]
    return pl.pallas_call(
        kern, grid=(n // tm,), in_specs=in_specs, out_specs=out_specs, out_shape=out_shape,
        compiler_params=_cparams(1), name="out_proj",
    )(*lhs, w, x, g)


def _ffn_up_kernel(hn_ref, wv_ref, wg_ref, cw_ref, cb_ref, st_ref, act_ref, stout_ref, prev_ref,
                   *, seg, tiles_per_seq):
    i = pl.program_id(1)
    hn = hn_ref[...]
    tm = hn.shape[0]
    tf = act_ref.shape[1]
    if tiles_per_seq > 1:
        @pl.when((i % tiles_per_seq) == 0)
        def _():
            prev_ref[...] = st_ref[0]

    cwid = min(tf, MXU_DIM)
    for c0 in range(0, tf, cwid):
        cs = slice(c0, c0 + cwid)
        gate = _dot(hn, wg_ref[:, cs])
        val = _dot(hn, wv_ref[:, cs])
        cw = cw_ref[:, cs]
        cb = cb_ref[:, cs]
        for s in range(tm // seg):
            g0 = gate[s * seg:(s + 1) * seg]
            prev = st_ref[s, :, cs] if tiles_per_seq == 1 else prev_ref[:, cs]
            ext = jnp.concatenate([prev, g0], axis=0)
            g1 = pltpu.roll(ext, 1, 0)[SUBLANES:]
            g2 = pltpu.roll(ext, 2, 0)[SUBLANES:]
            c = cb + cw[0:1] * g2 + cw[1:2] * g1 + cw[2:3] * g0
            act_ref[s * seg:(s + 1) * seg, cs] = (jax.nn.silu(c) * val[s * seg:(s + 1) * seg]).astype(BF16)
            last = g0[seg - SUBLANES:seg]
            stout_ref[s, :, cs] = last
            if tiles_per_seq > 1:
                prev_ref[:, cs] = last


def _ffn_up(hn, w_in, layer, conv_w, conv_b, conv_state, seq_len):
    n, d = hn.shape
    f = w_in.shape[2] // 2
    nseq = n // seq_len
    tf = max(t for t in range(MXU_DIM, f + 1, MXU_DIM) if f % t == 0 and 4 * d * t <= FFN_WEIGHT_BYTES)
    tm = _tile(n, 1024)
    nf = f // tf
    wspec = lambda col: pl.BlockSpec((None, d, tf), lambda j, i: (layer, 0, col(j)), pipeline_mode=pl.Buffered(1))
    if tm >= seq_len:
        seg, tps, spt = seq_len, 1, tm // seq_len
        st_idx = lambda j, i: (i, 0, j)
    else:
        seg, tps, spt = tm, seq_len // tm, 1
        st_idx = lambda j, i: (i // tps, 0, j)
    assert seg % SUBLANES == 0 and seg >= SUBLANES
    st = jnp.pad(conv_state, ((0, 0), (SUBLANES - (CONV_W - 1), 0), (0, 0)))
    cw = jnp.pad(conv_w, ((0, SUBLANES - CONV_W), (0, 0)))
    cb = conv_b.reshape(1, f)
    kern = functools.partial(_ffn_up_kernel, seg=seg, tiles_per_seq=tps)
    act, st_out = pl.pallas_call(
        kern,
        grid=(nf, n // tm),
        in_specs=[pl.BlockSpec((tm, d), lambda j, i: (i, 0)),
                  wspec(lambda j: j), wspec(lambda j: nf + j),
                  pl.BlockSpec((SUBLANES, tf), lambda j, i: (0, j)),
                  pl.BlockSpec((1, tf), lambda j, i: (0, j)),
                  pl.BlockSpec((spt, SUBLANES, tf), st_idx)],
        out_specs=[pl.BlockSpec((tm, tf), lambda j, i: (i, j)),
                   pl.BlockSpec((spt, SUBLANES, tf), st_idx)],
        out_shape=[jax.ShapeDtypeStruct((n, f), BF16), jax.ShapeDtypeStruct((nseq, SUBLANES, f), F32)],
        scratch_shapes=[pltpu.VMEM((SUBLANES, tf), F32)],
        compiler_params=_cparams(2),
        name="ffn_up",
    )(hn, w_in, w_in, cw, cb, st)
    return act, st_out[:, SUBLANES - (CONV_W - 1):, :]


def _rope_pair(x, cos, sin):
    return x * cos + pltpu.roll(x, MLA_ROPE, 1) * sin


def _odd_in_kernel(hn_ref, win_ref, gq_ref, gkv_ref, wuq_ref, cos_ref, sin_ref, *rest,
                   qr, kvr, heads, qscale, expand):
    if expand:
        wk_ref, wvt_ref, q_ref, ckv_ref, ckvb_ref, kpe_ref, kpeb_ref, kn_ref, vt_ref = rest
    else:
        q_ref, ckv_ref, ckvb_ref, kpe_ref, kpeb_ref = rest
    hn = hn_ref[...]
    cos = cos_ref[...]
    sin = sin_ref[...]
    cq = _rms(_dot(hn, win_ref[:, 0:qr]), gq_ref[...]).astype(BF16)
    ckv = _rms(_dot(hn, win_ref[:, qr:qr + kvr]), gkv_ref[...])
    kpe = _rope_pair(_dot(hn, win_ref[:, qr + kvr:qr + kvr + LANES]), cos, sin)
    ckv_ref[...] = ckv
    ckvb = ckv.astype(BF16)
    ckvb_ref[...] = ckvb
    kpe_ref[...] = kpe[:, 0:MLA_ROPE]
    kpeb_ref[...] = kpe.astype(BF16)
    hw = MXU_DIM
    for h in range(heads):
        qh = _dot(cq, wuq_ref[:, h * hw:(h + 1) * hw]) * qscale
        q_ref[:, h * hw:h * hw + LANES] = qh[:, 0:LANES].astype(BF16)
        q_ref[:, h * hw + LANES:(h + 1) * hw] = _rope_pair(qh[:, LANES:hw], cos, sin).astype(BF16)
    if expand:
        kn_ref[...] = _dot(ckvb, wk_ref[...]).astype(BF16)
        vt_ref[...] = _dot_t(wvt_ref[...], ckvb).astype(BF16)


def _odd_in(hn, win, gq, gkv, wuq, cos, sin, wkv, seq_len, heads):
    n, d = hn.shape
    qr, kvr = gq.shape[1], gkv.shape[1]
    tm = _tile(n, 512)
    row = lambda i: (i, 0)
    if tm <= seq_len:
        tps = seq_len // tm
        pos = lambda i: (i % tps, 0)
    else:
        cos = jnp.tile(cos, (tm // seq_len, 1))
        sin = jnp.tile(sin, (tm // seq_len, 1))
        pos = lambda i: (0, 0)
    expand = wkv is not None
    kern = functools.partial(_odd_in_kernel, qr=qr, kvr=kvr, heads=heads,
                             qscale=(MLA_NOPE + MLA_ROPE) ** -0.5 * LOG2E, expand=expand)
    in_specs = [pl.BlockSpec((tm, d), row), _resident(win.shape), _resident(gq.shape), _resident(gkv.shape),
                _resident(wuq.shape), pl.BlockSpec((tm, LANES), pos), pl.BlockSpec((tm, LANES), pos)]
    args = [hn, win, gq, gkv, wuq, cos, sin]
    out_specs = [pl.BlockSpec((tm, heads * MXU_DIM), row), pl.BlockSpec((tm, kvr), row),
                 pl.BlockSpec((tm, kvr), row), pl.BlockSpec((tm, MLA_ROPE), row), pl.BlockSpec((tm, LANES), row)]
    out_shape = [jax.ShapeDtypeStruct((n, heads * MXU_DIM), BF16), jax.ShapeDtypeStruct((n, kvr), F32),
                 jax.ShapeDtypeStruct((n, kvr), BF16), jax.ShapeDtypeStruct((n, MLA_ROPE), F32),
                 jax.ShapeDtypeStruct((n, LANES), BF16)]
    if expand:
        in_specs += [_resident(wkv[0].shape), _resident(wkv[1].shape)]
        args += list(wkv)
        out_specs += [pl.BlockSpec((tm, heads * MLA_NOPE), row),
                      pl.BlockSpec((heads * MLA_V, tm), lambda i: (0, i))]
        out_shape += [jax.ShapeDtypeStruct((n, heads * MLA_NOPE), BF16),
                      jax.ShapeDtypeStruct((heads * MLA_V, n), BF16)]
    return pl.pallas_call(
        kern, grid=(n // tm,), in_specs=in_specs, out_specs=out_specs, out_shape=out_shape,
        compiler_params=_cparams(1), name="odd_in",
    )(*args)


def _mla_prompt_kernel(q_ref, kn_ref, kpe_ref, vt_ref, o_ref, kx_ref, vxt_ref, m_ref, a_ref, *, tq, tk, hp):
    i = pl.program_id(2)
    hw = MXU_DIM

    @pl.when(i == 0)
    def _():
        for j in range(hp):
            kx_ref[j, :, 0:MLA_NOPE] = kn_ref[:, j * MLA_NOPE:(j + 1) * MLA_NOPE]
            kx_ref[j, :, MLA_NOPE:hw] = kpe_ref[...]
            for kt in range(vxt_ref.shape[1]):
                vxt_ref[j, kt, 0:MLA_V, :] = vt_ref[j * MLA_V:(j + 1) * MLA_V, kt * tk:(kt + 1) * tk]
                vxt_ref[j, kt, MLA_V:VX_ROWS, :] = _unit_rows(tk)

    m_ref[...] = jnp.full(m_ref.shape, NEG_INF, F32)
    a_ref[...] = jnp.zeros(a_ref.shape, F32)

    def streams_at(kt):
        sl = pl.ds(pl.multiple_of(kt * tk, tk), tk)
        return [(kx_ref[j, sl, :], q_ref[:, j * hw:(j + 1) * hw], vxt_ref[j, kt], m_ref.at[j], a_ref.at[j])
                for j in range(hp)]

    _flash_sweep(i, streams_at, tq, tk)
    for j in range(hp):
        a = a_ref[j]
        o_ref[:, j * MLA_V:(j + 1) * MLA_V] = (a[0:MLA_V] / a[MLA_V:MLA_V + 1]).T.astype(BF16)


def _mla_prompt(q, kn, kpeb, vt, nbatch, seq_len, heads):
    n = q.shape[0]
    tq = _tile(seq_len, FLASH_TQ)
    tk = _tile(tq, FLASH_TK)
    hp = 4
    assert tk % CHUNK == 0 and heads % hp == 0 and MLA_V == LANES
    nq = seq_len // tq
    kern = functools.partial(_mla_prompt_kernel, tq=tq, tk=tk, hp=hp)
    return pl.pallas_call(
        kern,
        grid=(nbatch, heads // hp, nq),
        in_specs=[pl.BlockSpec((tq, hp * MXU_DIM), lambda b, h, i: (b * nq + i, h)),
                  pl.BlockSpec((seq_len, hp * MLA_NOPE), lambda b, h, i: (b, h)),
                  pl.BlockSpec((seq_len, LANES), lambda b, h, i: (b, 0)),
                  pl.BlockSpec((hp * MLA_V, seq_len), lambda b, h, i: (h, b))],
        out_specs=pl.BlockSpec((tq, hp * MLA_V), lambda b, h, i: (b * nq + i, h)),
        out_shape=jax.ShapeDtypeStruct((n, heads * MLA_V), BF16),
        scratch_shapes=[pltpu.VMEM((hp, seq_len, MXU_DIM), BF16),
                        pltpu.VMEM((hp, seq_len // tk, VX_ROWS, tk), BF16),
                        pltpu.VMEM((hp, 1, tq), F32), pltpu.VMEM((hp, VX_ROWS, tq), F32)],
        compiler_params=_cparams(3),
        name="mla_attn_prompt",
    )(q, kn, kpeb, vt)


def _mla_sample_kernel(q_ref, cc_ref, pc_ref, cn_ref, pn_ref, wk_ref, wv_ref, o_ref, ql_ref, qp_ref,
                       *, heads, past):
    tq = q_ref.shape[0]
    hw = MXU_DIM
    for h in range(heads):
        qn = q_ref[:, h * hw:h * hw + MLA_NOPE]
        ql_ref[h * tq:(h + 1) * tq, :] = _dot(qn, wk_ref[h]).astype(BF16)
        qp_ref[h * tq:(h + 1) * tq, :] = q_ref[:, h * hw + MLA_NOPE:(h + 1) * hw]
    ql = ql_ref[...]
    qp = qp_ref[...]
    cc = cc_ref[...].astype(BF16)
    pc = pc_ref[...].astype(BF16)
    cn = cn_ref[...]
    mask_n = jnp.concatenate([_chunk_mask(past, past, tq, tq)] * heads, axis=0)
    sc = _dot_t(ql, cc) + _dot_t(qp[:, 0:MLA_ROPE], pc)
    sn = jnp.where(mask_n, _dot_t(ql, cn) + _dot_t(qp, pn_ref[...]), NEG_INF)
    m = jnp.maximum(jnp.max(sc, axis=-1, keepdims=True), jnp.max(sn, axis=-1, keepdims=True))
    ec = jnp.exp2(sc - m)
    en = jnp.exp2(sn - m)
    l = jnp.sum(ec, axis=-1, keepdims=True) + jnp.sum(en, axis=-1, keepdims=True)
    ol = ((_dot(ec.astype(BF16), cc) + _dot(en.astype(BF16), cn)) / l).astype(BF16)
    for h in range(heads):
        o_ref[:, h * MLA_V:(h + 1) * MLA_V] = _dot(ol[h * tq:(h + 1) * tq], wv_ref[h]).astype(BF16)


def _mla_sample(q, ckvb, kpeb, cache_ckv, cache_kpe, wk_t, wv, nbatch, seq_len, heads):
    n = q.shape[0]
    past, kvr = cache_ckv.shape[1], cache_ckv.shape[2]
    assert (past // CHUNK) * CHUNK == past
    cc = cache_ckv.reshape(nbatch * past, kvr)
    pc = cache_kpe.reshape(nbatch * past, MLA_ROPE)
    kern = functools.partial(_mla_sample_kernel, heads=heads, past=past)
    row = lambda b: (b, 0)
    return pl.pallas_call(
        kern,
        grid=(nbatch,),
        in_specs=[pl.BlockSpec((seq_len, heads * MXU_DIM), row),
                  pl.BlockSpec((past, kvr), row), pl.BlockSpec((past, MLA_ROPE), row),
                  pl.BlockSpec((seq_len, kvr), row), pl.BlockSpec((seq_len, LANES), row),
                  _resident(wk_t.shape), _resident(wv.shape)],
        out_specs=pl.BlockSpec((seq_len, heads * MLA_V), row),
        out_shape=jax.ShapeDtypeStruct((n, heads * MLA_V), BF16),
        scratch_shapes=[pltpu.VMEM((heads * seq_len, kvr), BF16), pltpu.VMEM((heads * seq_len, LANES), BF16)],
        compiler_params=_cparams(1),
        name="mla_attn_sample",
    )(q, cc, pc, ckvb, kpeb, wk_t, wv)


def _rope_tables(pos):
    half = MLA_ROPE // 2
    inv = ROPE_BASE ** (-jnp.arange(half, dtype=F32) / half)
    ang = pos.astype(F32)[:, None] * inv[None, :]
    cos, sin = jnp.cos(ang), jnp.sin(ang)
    zero = jnp.zeros_like(cos)
    return (jnp.concatenate([cos, cos, zero, zero], axis=1),
            jnp.concatenate([-sin, sin, zero, zero], axis=1))


def _swap_halves(w):
    half = MLA_ROPE // 2
    return jnp.concatenate([w[..., half:], w[..., :half]], axis=-1)


def _prepare_weights(p):
    w = {}
    heads = p['mla_w_uq'].shape[2]
    w['heads'] = heads
    w['norm_mix'] = p['norm_mix'][:, None, :]
    w['norm_ffn'] = p['norm_ffn'][:, None, :]
    w['norm_final'] = p['norm_final'][None, :]
    w['w_in_even'] = p['w_in_even'][0].astype(BF16)
    w['w_out_even'] = p['w_out_even'][0].astype(BF16)
    w['s5'] = dict(a_re=p['s5_a_re'][0], a_im=p['s5_a_im'][0], b_re=p['s5_b_re'][0], b_im=p['s5_b_im'][0],
                   c_re=p['s5_c_re'][0], c_im=p['s5_c_im'][0], d=p['s5_d'][0], log_dt=p['s5_log_dt'][0],
                   w_glu=p['s5_w_glu'][0], b_glu=p['s5_b_glu'][0])
    w['lam'] = [p[k][0][None, :] for k in ('diff_lambda_q1', 'diff_lambda_k1', 'diff_lambda_q2', 'diff_lambda_k2')]
    w['subln'] = p['diff_subln'][0][None, :]
    wi = p['w_in_odd'][0]
    qr = p['mla_q_norm'].shape[1]
    kvr = p['mla_kv_norm'].shape[1]
    wpe = wi[:, qr + kvr:]
    w['w_in_odd'] = jnp.concatenate([wi[:, :qr + kvr], wpe, _swap_halves(wpe)], axis=1).astype(BF16)
    w['gq'] = p['mla_q_norm'][0][None, :]
    w['gkv'] = p['mla_kv_norm'][0][None, :]
    wuq = p['mla_w_uq'][0]
    wuq = jnp.concatenate([wuq, _swap_halves(wuq[..., MLA_NOPE:])], axis=-1)
    w['w_uq'] = wuq.reshape(qr, heads * MXU_DIM).astype(BF16)
    wukv = p['mla_w_ukv'][0]
    w['w_kv'] = (wukv[..., :MLA_NOPE].reshape(kvr, heads * MLA_NOPE).astype(BF16),
                 wukv[..., MLA_NOPE:].reshape(kvr, heads * MLA_V).T.astype(BF16))
    w['w_uk_t'] = jnp.transpose(wukv[..., :MLA_NOPE], (1, 2, 0)).astype(BF16)
    w['w_uv'] = jnp.transpose(wukv[..., MLA_NOPE:], (1, 0, 2)).astype(BF16)
    w['w_out_odd'] = p['w_out_odd'][0].astype(BF16)
    w['ffn_w_in'] = p['ffn_w_in'].astype(BF16)
    w['ffn_w_down'] = p['ffn_w_down'].astype(BF16)
    w['ffn_conv_w'] = p['ffn_conv_w']
    w['ffn_conv_b'] = p['ffn_conv_b']
    return w


def _trunk(x3, s5_re0, s5_im0, k_past, v_past, ckv_past, kpe_past, conv0, pos0, w):
    nb, t, d = x3.shape
    n = nb * t
    x = x3.reshape(n, d)
    heads = w['heads']
    s5w = w['s5']['d'].size
    dw = (w['w_in_even'].shape[1] - s5w) // 3
    dheads = dw // LANES

    lam_init = 0.8 - 0.6 * math.exp(-0.3 * 0)
    u, q, k, v, kb, vb = _even_in(x, w['norm_mix'][0], w['w_in_even'], s5w, dw, v_transposed=k_past is None)
    y_s5, ht_re, ht_im = _s5(u, s5_re0, s5_im0, t, w['s5'])
    if k_past is None:
        o = _diff_prompt(q, kb, vb, w['lam'], w['subln'], nb, t, lam_init)
    else:
        o = _diff_sample(q, kb, vb, k_past, v_past, w['lam'], w['subln'], nb, t, lam_init)
    x, hn = _out_proj([y_s5, o], w['w_out_even'], x, w['norm_ffn'][0])
    act, conv_a = _ffn_up(hn, w['ffn_w_in'], 0, w['ffn_conv_w'][0], w['ffn_conv_b'][0], conv0[0], t)
    x, hn = _out_proj([act], w['ffn_w_down'], x, w['norm_mix'][1], tm_pref=256, layer=0)

    cos, sin = _rope_tables(pos0 + jnp.arange(t, dtype=jnp.int32))
    if ckv_past is None:
        qm, ckv, ckvb, kpe, kpeb, kn, vm = _odd_in(hn, w['w_in_odd'], w['gq'], w['gkv'], w['w_uq'], cos, sin,
                                                   w['w_kv'], t, heads)
        om = _mla_prompt(qm, kn, kpeb, vm, nb, t, heads)
    else:
        qm, ckv, ckvb, kpe, kpeb = _odd_in(hn, w['w_in_odd'], w['gq'], w['gkv'], w['w_uq'], cos, sin,
                                           None, t, heads)
        om = _mla_sample(qm, ckvb, kpeb, ckv_past, kpe_past, w['w_uk_t'], w['w_uv'], nb, t, heads)
    x, hn = _out_proj([om], w['w_out_odd'], x, w['norm_ffn'][1])
    act, conv_b = _ffn_up(hn, w['ffn_w_in'], 1, w['ffn_conv_w'][1], w['ffn_conv_b'][1], conv0[1], t)
    (y,) = _out_proj([act], w['ffn_w_down'], x, w['norm_final'], final=True, tm_pref=256, layer=1)

    groups, n_state = w['s5']['a_re'].shape
    return (y.reshape(nb, t, d), ht_re[None], ht_im[None],
            k.reshape(1, nb, t, dheads, LANES), v.reshape(1, nb, t, dheads, LANES),
            ckv.reshape(1, nb, t, -1), kpe.reshape(1, nb, t, MLA_ROPE), jnp.stack([conv_a, conv_b]))


def kernel(x_prompt, x_sample, state_s5_re, state_s5_im, cache_diff_k, cache_diff_v, cache_mla_ckv, cache_mla_kpe, state_ffn_conv, norm_mix, norm_ffn, norm_final, w_in_even, w_out_even, s5_a_re, s5_a_im, s5_b_re, s5_b_im, s5_c_re, s5_c_im, s5_d, s5_log_dt, s5_w_glu, s5_b_glu, diff_lambda_q1, diff_lambda_k1, diff_lambda_q2, diff_lambda_k2, diff_subln, w_in_odd, mla_q_norm, mla_kv_norm, mla_w_uq, mla_w_ukv, w_out_odd, ffn_w_in, ffn_conv_w, ffn_conv_b, ffn_w_down):
    w = _prepare_weights(dict(
        norm_mix=norm_mix, norm_ffn=norm_ffn, norm_final=norm_final, w_in_even=w_in_even, w_out_even=w_out_even,
        s5_a_re=s5_a_re, s5_a_im=s5_a_im, s5_b_re=s5_b_re, s5_b_im=s5_b_im, s5_c_re=s5_c_re, s5_c_im=s5_c_im,
        s5_d=s5_d, s5_log_dt=s5_log_dt, s5_w_glu=s5_w_glu, s5_b_glu=s5_b_glu,
        diff_lambda_q1=diff_lambda_q1, diff_lambda_k1=diff_lambda_k1, diff_lambda_q2=diff_lambda_q2,
        diff_lambda_k2=diff_lambda_k2, diff_subln=diff_subln, w_in_odd=w_in_odd, mla_q_norm=mla_q_norm,
        mla_kv_norm=mla_kv_norm, mla_w_uq=mla_w_uq, mla_w_ukv=mla_w_ukv, w_out_odd=w_out_odd,
        ffn_w_in=ffn_w_in, ffn_conv_w=ffn_conv_w, ffn_conv_b=ffn_conv_b, ffn_w_down=ffn_w_down))
    nb_p = x_prompt.shape[0]
    groups, n_state = s5_a_re.shape[1:]
    d_ff = ffn_conv_b.shape[1]
    depth = ffn_conv_b.shape[0]
    s5_zero = jnp.zeros((nb_p, groups, n_state), F32)
    conv_zero = jnp.zeros((depth, nb_p, CONV_W - 1, d_ff), F32)
    (y_p, re_p, im_p, k_p, v_p, ckv_p, kpe_p, conv_p) = _trunk(
        x_prompt, s5_zero, s5_zero, None, None, None, None, conv_zero, 0, w)
    past = cache_diff_k.shape[2]
    (y_s, re_s, im_s, k_s, v_s, ckv_s, kpe_s, conv_s) = _trunk(
        x_sample, state_s5_re[0], state_s5_im[0], cache_diff_k[0], cache_diff_v[0], cache_mla_ckv[0],
        cache_mla_kpe[0], state_ffn_conv, past, w)
    return (y_p, y_s, re_p, im_p, re_s, im_s, k_p, v_p, k_s, v_s, ckv_p, kpe_p, ckv_s, kpe_s, conv_p, conv_s)
```

```python
import functools
import math

import jax
import jax.numpy as jnp
from jax import lax
from jax.experimental import pallas as pl
from jax.experimental.pallas import tpu as pltpu

F32 = jnp.float32
BF16 = jnp.bfloat16

CHUNK = 64
EPS = 1e-6
NEG_INF = -1e30
ROPE_BASE = 10000.0
S5_GROUP = 16
DIFF_DK = 64
MLA_NOPE = 128
MLA_ROPE = 64
MLA_V = 128
CONV_W = 3
LOG2E = 1.4426950408889634

LANES = 128
SUBLANES = 8
MXU_DIM = 256
VMEM_LIMIT_BYTES = 56 * 1024 * 1024

FFN_WEIGHT_BYTES = 24 * 1024 * 1024
FLASH_TQ = 1024
FLASH_TK = 512
FLASH_COLS = 256
VX_ROWS = LANES + 16
S5_L = 16
S5_GB = MXU_DIM // S5_GROUP


def _cparams(n_axes):
    return pltpu.CompilerParams(
        dimension_semantics=("arbitrary",) * n_axes,
        vmem_limit_bytes=VMEM_LIMIT_BYTES)


def _resident(shape):
    nd = len(shape)
    return pl.BlockSpec(shape, lambda *_: (0,) * nd, pipeline_mode=pl.Buffered(1))


def _tile(n, pref):
    t = min(n, pref)
    while n % t:
        t //= 2
    return t


def _dot(a, b):
    return jnp.dot(a, b, preferred_element_type=F32)


def _dot_t(a, b):
    return lax.dot_general(a, b, (((1,), (1,)), ((), ())), preferred_element_type=F32)


def _rms(x, g):
    ms = jnp.mean(x * x, axis=-1, keepdims=True)
    return x * lax.rsqrt(ms + EPS) * g


def _even_in_kernel(x_ref, g_ref, w_ref, u_ref, q_ref, k_ref, v_ref, kb_ref, vb_ref,
                    *, s5w, dw, qscale, v_transposed):
    xn = _rms(x_ref[...], g_ref[...]).astype(BF16)
    u_ref[...] = _dot(xn, w_ref[:, 0:s5w])
    q_ref[...] = (_dot(xn, w_ref[:, s5w:s5w + dw]) * qscale).astype(BF16)
    k = _dot(xn, w_ref[:, s5w + dw:s5w + 2 * dw])
    k_ref[...] = k
    kb_ref[...] = k.astype(BF16)
    v = _dot(xn, w_ref[:, s5w + 2 * dw:s5w + 3 * dw])
    v_ref[...] = v
    vb_ref[...] = (v.T if v_transposed else v).astype(BF16)


def _even_in(x, g, w, s5w, dw, v_transposed):
    n, d = x.shape
    tm = _tile(n, 512)
    row = lambda i: (i, 0)
    kern = functools.partial(_even_in_kernel, s5w=s5w, dw=dw, qscale=DIFF_DK ** -0.5 * LOG2E,
                             v_transposed=v_transposed)
    vb_spec = pl.BlockSpec((dw, tm), lambda i: (0, i)) if v_transposed else pl.BlockSpec((tm, dw), row)
    return pl.pallas_call(
        kern,
        grid=(n // tm,),
        in_specs=[pl.BlockSpec((tm, d), row), _resident(g.shape), _resident(w.shape)],
        out_specs=[pl.BlockSpec((tm, s5w), row), pl.BlockSpec((tm, dw), row), pl.BlockSpec((tm, dw), row),
                   pl.BlockSpec((tm, dw), row), pl.BlockSpec((tm, dw), row), vb_spec],
        out_shape=[jax.ShapeDtypeStruct((n, s5w), F32), jax.ShapeDtypeStruct((n, dw), BF16),
                   jax.ShapeDtypeStruct((n, dw), F32), jax.ShapeDtypeStruct((n, dw), F32),
                   jax.ShapeDtypeStruct((n, dw), BF16),
                   jax.ShapeDtypeStruct((dw, n) if v_transposed else (n, dw), BF16)],
        compiler_params=_cparams(1),
        name="even_in",
    )(x, g, w)


def _s5_kernel(u_ref, h0_ref, lre_ref, lim_ref, dt_ref, bre_ref, bim_ref, cbd_ref, d_ref, wglu_ref, bglu_ref,
               y_ref, ht_ref,
               a_ref, a16_ref, bbd_ref, us_ref, xs_ref, xe_ref, hs_ref, carry_ref, ys_ref, yn_ref,
               *, rows, width, cps, ngb):
    t = pl.program_id(0)
    nblk = ngb * 16
    half = 8
    gw = MXU_DIM

    @pl.when(t == 0)
    def _prepare():
        carry_ref[...] = jnp.zeros_like(carry_ref)
        for gb in range(ngb):
            for k in range(half):
                lre = jnp.minimum(lre_ref[gb * half + k], -1e-4)
                lim = lim_ref[gb * half + k]
                dt = jnp.exp(dt_ref[gb * half + k])
                mag = jnp.exp(lre * dt)
                are = mag * jnp.cos(lim * dt)
                aim = mag * jnp.sin(lim * dt)
                den = lre * lre + lim * lim
                cre = ((are - 1.0) * lre + aim * lim) / den
                cim = (aim * lre - (are - 1.0) * lim) / den
                ire, iim = gb * 16 + k, gb * 16 + half + k
                a_ref[ire] = are
                a_ref[iim] = aim
                pre, pim = are, aim
                for _ in range(4):
                    pre, pim = pre * pre - pim * pim, 2.0 * pre * pim
                a16_ref[ire] = pre
                a16_ref[iim] = pim
                br = bre_ref[gb, :, k * LANES:(k + 1) * LANES]
                bi = bim_ref[gb, :, k * LANES:(k + 1) * LANES]
                bbd_ref[gb, :, k * LANES:(k + 1) * LANES] = (cre * br - cim * bi).astype(BF16)
                bbd_ref[gb, :, (half + k) * LANES:(half + k + 1) * LANES] = (cre * bi + cim * br).astype(BF16)

    ncb = width // LANES
    for cb in range(ncb):
        us_ref[cb] = u_ref[:, cb * LANES:(cb + 1) * LANES]

    def step(l):
        return pl.ds(l * rows, rows)

    def at_step(l):
        return pl.ds(l, rows, stride=S5_L)

    for gb in range(ngb):
        cbs = range(gb * gw // LANES, (gb + 1) * gw // LANES)
        lhs = jnp.concatenate(
            [jnp.concatenate([us_ref[cb, at_step(l), :] for cb in cbs], axis=1) for l in range(S5_L)], axis=0)
        x = _dot(lhs.astype(BF16), bbd_ref[gb])
        for j in range(16):
            xs_ref[gb * 16 + j] = x[:, j * LANES:(j + 1) * LANES]
        for k in range(half):
            ire, iim = gb * 16 + k, gb * 16 + half + k
            are, aim = a_ref[ire], a_ref[iim]
            hre = xs_ref[ire, step(0), :]
            him = xs_ref[iim, step(0), :]
            for l in range(1, S5_L):
                xre = xs_ref[ire, step(l), :]
                xim = xs_ref[iim, step(l), :]
                hre, him = are * hre - aim * him + xre, are * him + aim * hre + xim
            xe_ref[ire] = hre
            xe_ref[iim] = him

    base_row = t * rows
    for grp in range(ngb * half // 8):
        ire0 = (grp // (half // 8)) * 16 + (grp % (half // 8)) * 8
        iim0 = ire0 + half
        a16re = a16_ref[ire0:ire0 + 8]
        a16im = a16_ref[iim0:iim0 + 8]

        cre = carry_ref[ire0:ire0 + 8]
        cim = carry_ref[iim0:iim0 + 8]
        for r in range(rows):
            if cps >= rows:
                if r == 0:
                    is_start = (base_row % cps) == 0
                    cre = jnp.where(is_start, h0_ref[0, ire0:ire0 + 8], cre)
                    cim = jnp.where(is_start, h0_ref[0, iim0:iim0 + 8], cim)
            elif r % cps == 0:
                cre = h0_ref[r // cps, ire0:ire0 + 8]
                cim = h0_ref[r // cps, iim0:iim0 + 8]
            hs_ref[ire0:ire0 + 8, r:r + 1, :] = cre
            hs_ref[iim0:iim0 + 8, r:r + 1, :] = cim
            xre = xe_ref[ire0:ire0 + 8, r:r + 1, :]
            xim = xe_ref[iim0:iim0 + 8, r:r + 1, :]
            cre, cim = a16re * cre - a16im * cim + xre, a16re * cim + a16im * cre + xim
            if cps < rows and r % cps == cps - 1:
                ht_ref[r // cps, ire0:ire0 + 8] = cre
                ht_ref[r // cps, iim0:iim0 + 8] = cim
        if cps >= rows:
            ht_ref[0, ire0:ire0 + 8] = cre
            ht_ref[0, iim0:iim0 + 8] = cim
        carry_ref[ire0:ire0 + 8] = cre
        carry_ref[iim0:iim0 + 8] = cim

    for gb in range(ngb):
        for k in range(half):
            ire, iim = gb * 16 + k, gb * 16 + half + k
            are, aim = a_ref[ire], a_ref[iim]
            hre = hs_ref[ire]
            him = hs_ref[iim]
            for l in range(S5_L):
                xre = xs_ref[ire, step(l), :]
                xim = xs_ref[iim, step(l), :]
                hre, him = are * hre - aim * him + xre, are * him + aim * hre + xim
                xs_ref[ire, step(l), :] = hre
                xs_ref[iim, step(l), :] = him
        h = jnp.concatenate([xs_ref[gb * 16 + j].astype(BF16) for j in range(16)], axis=1)
        yc = _dot(h, cbd_ref[gb])
        for c in range(gw // LANES):
            ys_ref[gb * gw // LANES + c] = yc[:, c * LANES:(c + 1) * LANES]

    for cb in range(ncb):
        for r in range(rows):
            yn_ref[r * S5_L:(r + 1) * S5_L, cb * LANES:(cb + 1) * LANES] = ys_ref[cb, pl.ds(r, S5_L, stride=rows), :]

    y = yn_ref[...] + d_ref[...] * u_ref[...]
    z = jax.nn.gelu(y)
    gate = jax.nn.sigmoid(_dot(z.astype(BF16), wglu_ref[...]) + bglu_ref[...])
    y_ref[...] = (z * gate).astype(BF16)


def _s5_layout(p, ngb, n_state):
    return p.reshape(ngb * (S5_GB * n_state // LANES), 1, LANES)


def _s5_state_to_blocks(s, ngb):
    b = s.shape[0]
    return s.reshape(b, ngb, 8, 1, LANES)


def _s5(u, h0_re, h0_im, seq_len, w):
    n, width = u.shape
    groups, n_state = w['a_re'].shape
    assert n_state * S5_GB == 8 * LANES and width == groups * S5_GROUP
    ngb = groups // S5_GB
    nseq = n // seq_len
    cps = seq_len // S5_L
    nrow = n // S5_L
    rows = _tile(nrow, 32)
    assert rows % SUBLANES == 0 and (cps % rows == 0 or rows % cps == 0)
    spt = max(1, rows // cps)
    tps = max(1, cps // rows)
    nblk = ngb * 16
    sw = S5_GB * n_state

    h0 = jnp.concatenate([_s5_state_to_blocks(h0_re, ngb), _s5_state_to_blocks(h0_im, ngb)], axis=2)
    h0 = h0.reshape(nseq, nblk, 1, LANES)
    lre = _s5_layout(w['a_re'], ngb, n_state)
    lim = _s5_layout(w['a_im'], ngb, n_state)
    dt = _s5_layout(jnp.broadcast_to(w['log_dt'][:, None], (groups, n_state)), ngb, n_state)
    eye = jnp.eye(S5_GB, dtype=F32)

    def bdiag_b(b):
        bb = b.reshape(ngb, S5_GB, n_state, S5_GROUP)
        return jnp.einsum('agnp,gh->agphn', bb, eye).reshape(ngb, S5_GB * S5_GROUP, sw)

    def bdiag_c(c):
        cc = c.reshape(ngb, S5_GB, S5_GROUP, n_state)
        return jnp.einsum('agpn,gh->ahngp', cc, eye).reshape(ngb, sw, S5_GB * S5_GROUP)

    bre = bdiag_b(w['b_re'])
    bim = bdiag_b(w['b_im'])
    cbd = jnp.concatenate([bdiag_c(w['c_re']), -bdiag_c(w['c_im'])], axis=1).astype(BF16)
    d = w['d'].reshape(1, width)
    wglu = w['w_glu'].astype(BF16)
    bglu = w['b_glu'].reshape(1, width)

    seq_idx = (lambda t: (t // tps, 0, 0, 0)) if tps > 1 else (lambda t: (t, 0, 0, 0))
    kern = functools.partial(_s5_kernel, rows=rows, width=width, cps=cps, ngb=ngb)
    y, ht = pl.pallas_call(
        kern,
        grid=(nrow // rows,),
        in_specs=[pl.BlockSpec((S5_L * rows, width), lambda t: (t, 0)),
                  pl.BlockSpec((spt, nblk, 1, LANES), seq_idx),
                  _resident(lre.shape), _resident(lim.shape), _resident(dt.shape),
                  _resident(bre.shape), _resident(bim.shape), _resident(cbd.shape),
                  _resident(d.shape), _resident(wglu.shape), _resident(bglu.shape)],
        out_specs=[pl.BlockSpec((S5_L * rows, width), lambda t: (t, 0)),
                   pl.BlockSpec((spt, nblk, 1, LANES), seq_idx)],
        out_shape=[jax.ShapeDtypeStruct((n, width), BF16),
                   jax.ShapeDtypeStruct((nseq, nblk, 1, LANES), F32)],
        scratch_shapes=[pltpu.VMEM((nblk, 1, LANES), F32),
                        pltpu.VMEM((nblk, 1, LANES), F32),
                        pltpu.VMEM((ngb, MXU_DIM, 2 * sw), BF16),
                        pltpu.VMEM((width // LANES, S5_L * rows, LANES), F32),
                        pltpu.VMEM((nblk, S5_L * rows, LANES), F32),
                        pltpu.VMEM((nblk, rows, LANES), F32),
                        pltpu.VMEM((nblk, rows, LANES), F32),
                        pltpu.VMEM((nblk, 1, LANES), F32),
                        pltpu.VMEM((width // LANES, S5_L * rows, LANES), F32),
                        pltpu.VMEM((S5_L * rows, width), F32)],
        compiler_params=_cparams(1),
        name="s5_mix",
    )(u, h0, lre, lim, dt, bre, bim, cbd, d, wglu, bglu)
    ht = ht.reshape(nseq, ngb, 2, groups // ngb, n_state)
    ht_re = ht[:, :, 0].reshape(nseq, groups, n_state)
    ht_im = ht[:, :, 1].reshape(nseq, groups, n_state)
    return y, ht_re, ht_im


def _diff_lambda(lq1_ref, lk1_ref, lq2_ref, lk2_ref, lam_init):
    s1 = jnp.sum(lq1_ref[...] * lk1_ref[...], axis=-1, keepdims=True)
    s2 = jnp.sum(lq2_ref[...] * lk2_ref[...], axis=-1, keepdims=True)
    return jnp.exp(s1) - jnp.exp(s2) + lam_init


def _split_maps(q):
    lane = lax.broadcasted_iota(jnp.int32, q.shape, 1)
    zero = jnp.zeros_like(q)
    return jnp.where(lane < DIFF_DK, q, zero), jnp.where(lane >= DIFF_DK, q, zero)


def _chunk_mask(q0, k0, tq, tk):
    qc = (q0 + lax.broadcasted_iota(jnp.int32, (tq, tk), 0)) // CHUNK
    kc = (k0 + lax.broadcasted_iota(jnp.int32, (tq, tk), 1)) // CHUNK
    return kc <= qc


def _subln(o, g, lam_init):
    return (_rms(o, g) * (1.0 - lam_init)).astype(BF16)


def _flash_update(blocks):
    work = []
    for streams, koff in blocks:
        chains = []
        for k, q, vxt, m_ref, a_ref in streams:
            tk, tq = k.shape[0], q.shape[0]
            sub = min(tq, FLASH_COLS)
            assert sub % CHUNK == 0
            for c0 in range(0, tq, sub):
                if koff is None or koff + tk <= c0:
                    nk, mask = tk, None
                elif koff >= c0 + sub:
                    continue
                else:
                    nk = min(tk, c0 + sub - koff)
                    kc = (koff + lax.broadcasted_iota(jnp.int32, (nk, sub), 0)) // CHUNK
                    qc = (c0 + lax.broadcasted_iota(jnp.int32, (nk, sub), 1)) // CHUNK
                    mask = kc <= qc
                chains.append((k[0:nk], q, vxt[:, 0:nk], m_ref, a_ref, slice(c0, c0 + sub), mask))
        work.append((chains, [_dot_t(k, q[cs]) for k, q, _, _, _, cs, _ in chains]))
    for chains, scores in work:
        probs = []
        for (_, _, _, m_ref, _, cs, mask), s in zip(chains, scores):
            if mask is not None:
                s = jnp.where(mask, s, NEG_INF)
            m_prev = m_ref[:, cs]
            m_new = jnp.maximum(m_prev, jnp.max(s, axis=0, keepdims=True))
            m_ref[:, cs] = m_new
            probs.append((jnp.exp2(m_prev - m_new), jnp.exp2(s - m_new).astype(BF16)))
        for (_, _, vxt, _, a_ref, cs, _), (alpha, p) in zip(chains, probs):
            a_ref[:, cs] = alpha * a_ref[:, cs] + _dot(vxt, p)


def _flash_sweep(i, streams_at, tq, tk):
    nb = tq // tk

    def full(kt, carry):
        _flash_update([(streams_at(kt), None)])
        return carry

    lax.fori_loop(0, i * nb, full, 0)
    _flash_update([(streams_at(i * nb + d), d * tk) for d in range(nb)])


def _unit_rows(cols):
    return (lax.broadcasted_iota(jnp.int32, (VX_ROWS - LANES, cols), 0) == 0).astype(BF16)


def _diff_prompt_kernel(q_ref, k_ref, vt_ref, lq1_ref, lk1_ref, lq2_ref, lk2_ref, g_ref, o_ref,
                        vxt_ref, m_ref, a_ref, *, tq, tk, hq, lam_init):
    i = pl.program_id(2)

    @pl.when(i == 0)
    def _():
        for j in range(hq):
            for kt in range(vxt_ref.shape[1]):
                vxt_ref[j, kt, 0:LANES, :] = vt_ref[j * LANES:(j + 1) * LANES, kt * tk:(kt + 1) * tk]
                vxt_ref[j, kt, LANES:VX_ROWS, :] = _unit_rows(tk)

    qs = [_split_maps(q_ref[:, j * LANES:(j + 1) * LANES]) for j in range(hq)]
    m_ref[...] = jnp.full(m_ref.shape, NEG_INF, F32)
    a_ref[...] = jnp.zeros(a_ref.shape, F32)

    def streams_at(kt):
        sl = pl.ds(pl.multiple_of(kt * tk, tk), tk)
        streams = []
        for j in range(hq):
            kb = k_ref[sl, j * LANES:(j + 1) * LANES]
            for mp in range(2):
                streams.append((kb, qs[j][mp], vxt_ref[j, kt], m_ref.at[2 * j + mp], a_ref.at[2 * j + mp]))
        return streams

    _flash_sweep(i, streams_at, tq, tk)

    lam = _diff_lambda(lq1_ref, lk1_ref, lq2_ref, lk2_ref, lam_init)
    for j in range(hq):
        a1 = a_ref[2 * j]
        a2 = a_ref[2 * j + 1]
        ot = a1[0:LANES] / a1[LANES:LANES + 1] - lam * (a2[0:LANES] / a2[LANES:LANES + 1])
        ms = jnp.mean(ot * ot, axis=0, keepdims=True)
        ot = ot * lax.rsqrt(ms + EPS) * g_ref[...] * (1.0 - lam_init)
        o_ref[:, j * LANES:(j + 1) * LANES] = ot.T.astype(BF16)


def _diff_prompt(q, kb, vbt, lam_w, g, nbatch, seq_len, lam_init):
    n, dw = q.shape
    gcol = g.reshape(LANES, 1)
    heads = dw // LANES
    tq = _tile(seq_len, FLASH_TQ)
    tk = _tile(tq, FLASH_TK)
    hq = 2
    assert tk % CHUNK == 0 and heads % hq == 0
    nq = seq_len // tq
    kern = functools.partial(_diff_prompt_kernel, tq=tq, tk=tk, hq=hq, lam_init=lam_init)
    vec = lambda a: _resident(a.shape)
    return pl.pallas_call(
        kern,
        grid=(nbatch, heads // hq, nq),
        in_specs=[pl.BlockSpec((tq, hq * LANES), lambda b, h, i: (b * nq + i, h)),
                  pl.BlockSpec((seq_len, hq * LANES), lambda b, h, i: (b, h)),
                  pl.BlockSpec((hq * LANES, seq_len), lambda b, h, i: (h, b)),
                  vec(lam_w[0]), vec(lam_w[1]), vec(lam_w[2]), vec(lam_w[3]), vec(gcol)],
        out_specs=pl.BlockSpec((tq, hq * LANES), lambda b, h, i: (b * nq + i, h)),
        out_shape=jax.ShapeDtypeStruct((n, dw), BF16),
        scratch_shapes=[pltpu.VMEM((hq, seq_len // tk, VX_ROWS, tk), BF16),
                        pltpu.VMEM((2 * hq, 1, tq), F32), pltpu.VMEM((2 * hq, VX_ROWS, tq), F32)],
        compiler_params=_cparams(3),
        name="diff_attn_prompt",
    )(q, kb, vbt, *lam_w, gcol)


def _diff_sample_kernel(q_ref, kc_ref, vc_ref, kn_ref, vn_ref, lq1_ref, lk1_ref, lq2_ref, lk2_ref, g_ref, o_ref,
                        *, past, heads, lam_init):
    tq = q_ref.shape[0]
    mask_n = jnp.concatenate([_chunk_mask(past, past, tq, tq)] * 2, axis=0)
    lam = _diff_lambda(lq1_ref, lk1_ref, lq2_ref, lk2_ref, lam_init)
    for h in range(heads):
        hs = slice(h * LANES, (h + 1) * LANES)
        qm = jnp.concatenate(_split_maps(q_ref[:, hs]), axis=0)
        kc = kc_ref[pl.ds(h, past, stride=heads), :].astype(BF16)
        vc = vc_ref[pl.ds(h, past, stride=heads), :].astype(BF16)
        sc = _dot_t(qm, kc)
        sn = jnp.where(mask_n, _dot_t(qm, kn_ref[:, hs]), NEG_INF)
        m = jnp.maximum(jnp.max(sc, axis=-1, keepdims=True), jnp.max(sn, axis=-1, keepdims=True))
        pc = jnp.exp2(sc - m)
        pn = jnp.exp2(sn - m)
        l = jnp.sum(pc, axis=-1, keepdims=True) + jnp.sum(pn, axis=-1, keepdims=True)
        o = (_dot(pc.astype(BF16), vc) + _dot(pn.astype(BF16), vn_ref[:, hs])) / l
        o_ref[:, hs] = _subln(o[0:tq] - lam * o[tq:2 * tq], g_ref[...], lam_init)


def _diff_sample(q, kb, vb, cache_k, cache_v, lam_w, g, nbatch, seq_len, lam_init):
    n, dw = q.shape
    heads = dw // LANES
    past = cache_k.shape[1]
    assert cache_k.shape[0] == nbatch and (past // CHUNK) * CHUNK == past and heads == SUBLANES
    kc = cache_k.reshape(nbatch * past * heads, LANES)
    vc = cache_v.reshape(nbatch * past * heads, LANES)
    kern = functools.partial(_diff_sample_kernel, past=past, heads=heads, lam_init=lam_init)
    vec = lambda a: _resident(a.shape)
    row = lambda b: (b, 0)
    return pl.pallas_call(
        kern,
        grid=(nbatch,),
        in_specs=[pl.BlockSpec((seq_len, dw), row),
                  pl.BlockSpec((past * heads, LANES), row), pl.BlockSpec((past * heads, LANES), row),
                  pl.BlockSpec((seq_len, dw), row), pl.BlockSpec((seq_len, dw), row),
                  vec(lam_w[0]), vec(lam_w[1]), vec(lam_w[2]), vec(lam_w[3]), vec(g)],
        out_specs=pl.BlockSpec((seq_len, dw), row),
        out_shape=jax.ShapeDtypeStruct((n, dw), BF16),
        compiler_params=_cparams(1),
        name="diff_attn_sample",
    )(q, kc, vc, kb, vb, *lam_w, g)


def _out_proj_kernel(*refs, n_lhs, final):
    lhs = refs[:n_lhs]
    w_ref, x_ref, g_ref = refs[n_lhs:n_lhs + 3]
    outs = refs[n_lhs + 3:]
    acc = x_ref[...]
    off = 0
    for a in lhs:
        kdim = a.shape[1]
        acc = acc + _dot(a[...], w_ref[off:off + kdim, :])
        off += kdim
    if final:
        outs[0][...] = _rms(acc, g_ref[...])
    else:
        outs[0][...] = acc
        outs[1][...] = _rms(acc, g_ref[...]).astype(BF16)


def _out_proj(lhs, w, x, g, final=False, tm_pref=512, layer=None):
    n, d = x.shape
    tm = _tile(n, tm_pref)
    row = lambda i: (i, 0)
    kern = functools.partial(_out_proj_kernel, n_lhs=len(lhs), final=final)
    in_specs = [pl.BlockSpec((tm, a.shape[1]), row) for a in lhs]
    if layer is None:
        wspec = _resident(w.shape)
    else:
        wspec = pl.BlockSpec((None,) + w.shape[1:], lambda *_: (layer, 0, 0), pipeline_mode=pl.Buffered(1))
    in_specs += [wspec, pl.BlockSpec((tm, d), row), _resident(g.shape)]
    if final:
        out_specs = [pl.BlockSpec((tm, d), row)]
        out_shape = [jax.ShapeDtypeStruct((n, d), F32)]
    else:
        out_specs = [pl.BlockSpec((tm, d), row), pl.BlockSpec((tm, d), row)]
        out_shape = [jax.ShapeDtypeStruct((n, d), F32), jax.ShapeDtypeStruct((n, d), BF16)]
    return pl.pallas_call(
        kern, grid=(n // tm,), in_specs=in_specs, out_specs=out_specs, out_shape=out_shape,
        compiler_params=_cparams(1), name="out_proj",
    )(*lhs, w, x, g)


def _ffn_up_kernel(hn_ref, wv_ref, wg_ref, cw_ref, cb_ref, st_ref, act_ref, stout_ref, prev_ref,
                   *, seg, tiles_per_seq):
    i = pl.program_id(1)
    hn = hn_ref[...]
    tm = hn.shape[0]
    tf = act_ref.shape[1]
    if tiles_per_seq > 1:
        @pl.when((i % tiles_per_seq) == 0)
        def _():
            prev_ref[...] = st_ref[0]

    cwid = min(tf, MXU_DIM)
    for c0 in range(0, tf, cwid):
        cs = slice(c0, c0 + cwid)
        gate = _dot(hn, wg_ref[:, cs])
        val = _dot(hn, wv_ref[:, cs])
        cw = 0.5 * cw_ref[:, cs]
        cb = 0.5 * cb_ref[:, cs]
        for s in range(tm // seg):
            g0 = gate[s * seg:(s + 1) * seg]
            prev = st_ref[s, :, cs] if tiles_per_seq == 1 else prev_ref[:, cs]
            ext = jnp.concatenate([prev, g0], axis=0)
            g1 = pltpu.roll(ext, 1, 0)[SUBLANES:]
            g2 = pltpu.roll(ext, 2, 0)[SUBLANES:]
            hc = cb + cw[0:1] * g2 + cw[1:2] * g1 + cw[2:3] * g0
            act_ref[s * seg:(s + 1) * seg, cs] = ((hc + hc * jnp.tanh(hc)) * val[s * seg:(s + 1) * seg]).astype(BF16)
            last = g0[seg - SUBLANES:seg]
            stout_ref[s, :, cs] = last
            if tiles_per_seq > 1:
                prev_ref[:, cs] = last


def _ffn_up(hn, w_in, layer, conv_w, conv_b, conv_state, seq_len):
    n, d = hn.shape
    f = w_in.shape[2] // 2
    nseq = n // seq_len
    tf = max(t for t in range(MXU_DIM, f + 1, MXU_DIM) if f % t == 0 and 4 * d * t <= FFN_WEIGHT_BYTES)
    tm = _tile(n, 1024)
    nf = f // tf
    wspec = lambda col: pl.BlockSpec((None, d, tf), lambda j, i: (layer, 0, col(j)), pipeline_mode=pl.Buffered(1))
    if tm >= seq_len:
        seg, tps, spt = seq_len, 1, tm // seq_len
        st_idx = lambda j, i: (i, 0, j)
    else:
        seg, tps, spt = tm, seq_len // tm, 1
        st_idx = lambda j, i: (i // tps, 0, j)
    assert seg % SUBLANES == 0 and seg >= SUBLANES
    st = jnp.pad(conv_state, ((0, 0), (SUBLANES - (CONV_W - 1), 0), (0, 0)))
    cw = jnp.pad(conv_w, ((0, SUBLANES - CONV_W), (0, 0)))
    cb = conv_b.reshape(1, f)
    kern = functools.partial(_ffn_up_kernel, seg=seg, tiles_per_seq=tps)
    act, st_out = pl.pallas_call(
        kern,
        grid=(nf, n // tm),
        in_specs=[pl.BlockSpec((tm, d), lambda j, i: (i, 0)),
                  wspec(lambda j: j), wspec(lambda j: nf + j),
                  pl.BlockSpec((SUBLANES, tf), lambda j, i: (0, j)),
                  pl.BlockSpec((1, tf), lambda j, i: (0, j)),
                  pl.BlockSpec((spt, SUBLANES, tf), st_idx)],
        out_specs=[pl.BlockSpec((tm, tf), lambda j, i: (i, j)),
                   pl.BlockSpec((spt, SUBLANES, tf), st_idx)],
        out_shape=[jax.ShapeDtypeStruct((n, f), BF16), jax.ShapeDtypeStruct((nseq, SUBLANES, f), F32)],
        scratch_shapes=[pltpu.VMEM((SUBLANES, tf), F32)],
        compiler_params=_cparams(2),
        name="ffn_up",
    )(hn, w_in, w_in, cw, cb, st)
    return act, st_out[:, SUBLANES - (CONV_W - 1):, :]


def _rope_pair(x, cos, sin):
    return x * cos + pltpu.roll(x, MLA_ROPE, 1) * sin


def _odd_in_kernel(hn_ref, win_ref, gq_ref, gkv_ref, wuq_ref, cos_ref, sin_ref, *rest,
                   qr, kvr, heads, qscale, expand):
    if expand:
        wk_ref, wvt_ref, q_ref, ckv_ref, ckvb_ref, kpe_ref, kpeb_ref, kn_ref, vt_ref = rest
    else:
        q_ref, ckv_ref, ckvb_ref, kpe_ref, kpeb_ref = rest
    hn = hn_ref[...]
    cos = cos_ref[...]
    sin = sin_ref[...]
    cq = _rms(_dot(hn, win_ref[:, 0:qr]), gq_ref[...]).astype(BF16)
    ckv = _rms(_dot(hn, win_ref[:, qr:qr + kvr]), gkv_ref[...])
    kpe = _rope_pair(_dot(hn, win_ref[:, qr + kvr:qr + kvr + LANES]), cos, sin)
    ckv_ref[...] = ckv
    ckvb = ckv.astype(BF16)
    ckvb_ref[...] = ckvb
    kpe_ref[...] = kpe[:, 0:MLA_ROPE]
    kpeb_ref[...] = kpe.astype(BF16)
    hw = MXU_DIM
    for h in range(heads):
        qh = _dot(cq, wuq_ref[:, h * hw:(h + 1) * hw]) * qscale
        q_ref[:, h * hw:h * hw + LANES] = qh[:, 0:LANES].astype(BF16)
        q_ref[:, h * hw + LANES:(h + 1) * hw] = _rope_pair(qh[:, LANES:hw], cos, sin).astype(BF16)
    if expand:
        kn_ref[...] = _dot(ckvb, wk_ref[...]).astype(BF16)
        vt_ref[...] = _dot_t(wvt_ref[...], ckvb).astype(BF16)


def _odd_in(hn, win, gq, gkv, wuq, cos, sin, wkv, seq_len, heads):
    n, d = hn.shape
    qr, kvr = gq.shape[1], gkv.shape[1]
    tm = _tile(n, 512)
    row = lambda i: (i, 0)
    if tm <= seq_len:
        tps = seq_len // tm
        pos = lambda i: (i % tps, 0)
    else:
        cos = jnp.tile(cos, (tm // seq_len, 1))
        sin = jnp.tile(sin, (tm // seq_len, 1))
        pos = lambda i: (0, 0)
    expand = wkv is not None
    kern = functools.partial(_odd_in_kernel, qr=qr, kvr=kvr, heads=heads,
                             qscale=(MLA_NOPE + MLA_ROPE) ** -0.5 * LOG2E, expand=expand)
    in_specs = [pl.BlockSpec((tm, d), row), _resident(win.shape), _resident(gq.shape), _resident(gkv.shape),
                _resident(wuq.shape), pl.BlockSpec((tm, LANES), pos), pl.BlockSpec((tm, LANES), pos)]
    args = [hn, win, gq, gkv, wuq, cos, sin]
    out_specs = [pl.BlockSpec((tm, heads * MXU_DIM), row), pl.BlockSpec((tm, kvr), row),
                 pl.BlockSpec((tm, kvr), row), pl.BlockSpec((tm, MLA_ROPE), row), pl.BlockSpec((tm, LANES), row)]
    out_shape = [jax.ShapeDtypeStruct((n, heads * MXU_DIM), BF16), jax.ShapeDtypeStruct((n, kvr), F32),
                 jax.ShapeDtypeStruct((n, kvr), BF16), jax.ShapeDtypeStruct((n, MLA_ROPE), F32),
                 jax.ShapeDtypeStruct((n, LANES), BF16)]
    if expand:
        in_specs += [_resident(wkv[0].shape), _resident(wkv[1].shape)]
        args += list(wkv)
        out_specs += [pl.BlockSpec((tm, heads * MLA_NOPE), row),
                      pl.BlockSpec((heads * MLA_V, tm), lambda i: (0, i))]
        out_shape += [jax.ShapeDtypeStruct((n, heads * MLA_NOPE), BF16),
                      jax.ShapeDtypeStruct((heads * MLA_V, n), BF16)]
    return pl.pallas_call(
        kern, grid=(n // tm,), in_specs=in_specs, out_specs=out_specs, out_shape=out_shape,
        compiler_params=_cparams(1), name="odd_in",
    )(*args)


def _mla_prompt_kernel(q_ref, kn_ref, kpe_ref, vt_ref, o_ref, kx_ref, vxt_ref, m_ref, a_ref, *, tq, tk, hp):
    i = pl.program_id(2)
    hw = MXU_DIM

    @pl.when(i == 0)
    def _():
        for j in range(hp):
            kx_ref[j, :, 0:MLA_NOPE] = kn_ref[:, j * MLA_NOPE:(j + 1) * MLA_NOPE]
            kx_ref[j, :, MLA_NOPE:hw] = kpe_ref[...]
            for kt in range(vxt_ref.shape[1]):
                vxt_ref[j, kt, 0:MLA_V, :] = vt_ref[j * MLA_V:(j + 1) * MLA_V, kt * tk:(kt + 1) * tk]
                vxt_ref[j, kt, MLA_V:VX_ROWS, :] = _unit_rows(tk)

    m_ref[...] = jnp.full(m_ref.shape, NEG_INF, F32)
    a_ref[...] = jnp.zeros(a_ref.shape, F32)

    def streams_at(kt):
        sl = pl.ds(pl.multiple_of(kt * tk, tk), tk)
        return [(kx_ref[j, sl, :], q_ref[:, j * hw:(j + 1) * hw], vxt_ref[j, kt], m_ref.at[j], a_ref.at[j])
                for j in range(hp)]

    _flash_sweep(i, streams_at, tq, tk)
    for j in range(hp):
        a = a_ref[j]
        o_ref[:, j * MLA_V:(j + 1) * MLA_V] = (a[0:MLA_V] / a[MLA_V:MLA_V + 1]).T.astype(BF16)


def _mla_prompt(q, kn, kpeb, vt, nbatch, seq_len, heads):
    n = q.shape[0]
    tq = _tile(seq_len, FLASH_TQ)
    tk = _tile(tq, FLASH_TK)
    hp = 4
    assert tk % CHUNK == 0 and heads % hp == 0 and MLA_V == LANES
    nq = seq_len // tq
    kern = functools.partial(_mla_prompt_kernel, tq=tq, tk=tk, hp=hp)
    return pl.pallas_call(
        kern,
        grid=(nbatch, heads // hp, nq),
        in_specs=[pl.BlockSpec((tq, hp * MXU_DIM), lambda b, h, i: (b * nq + i, h)),
                  pl.BlockSpec((seq_len, hp * MLA_NOPE), lambda b, h, i: (b, h)),
                  pl.BlockSpec((seq_len, LANES), lambda b, h, i: (b, 0)),
                  pl.BlockSpec((hp * MLA_V, seq_len), lambda b, h, i: (h, b))],
        out_specs=pl.BlockSpec((tq, hp * MLA_V), lambda b, h, i: (b * nq + i, h)),
        out_shape=jax.ShapeDtypeStruct((n, heads * MLA_V), BF16),
        scratch_shapes=[pltpu.VMEM((hp, seq_len, MXU_DIM), BF16),
                        pltpu.VMEM((hp, seq_len // tk, VX_ROWS, tk), BF16),
                        pltpu.VMEM((hp, 1, tq), F32), pltpu.VMEM((hp, VX_ROWS, tq), F32)],
        compiler_params=_cparams(3),
        name="mla_attn_prompt",
    )(q, kn, kpeb, vt)


def _mla_sample_kernel(q_ref, cc_ref, pc_ref, cn_ref, pn_ref, wk_ref, wv_ref, o_ref, ql_ref, qp_ref,
                       *, heads, past):
    tq = q_ref.shape[0]
    hw = MXU_DIM
    for h in range(heads):
        qn = q_ref[:, h * hw:h * hw + MLA_NOPE]
        ql_ref[h * tq:(h + 1) * tq, :] = _dot(qn, wk_ref[h]).astype(BF16)
        qp_ref[h * tq:(h + 1) * tq, :] = q_ref[:, h * hw + MLA_NOPE:(h + 1) * hw]
    ql = ql_ref[...]
    qp = qp_ref[...]
    cc = cc_ref[...].astype(BF16)
    pc = pc_ref[...].astype(BF16)
    cn = cn_ref[...]
    mask_n = jnp.concatenate([_chunk_mask(past, past, tq, tq)] * heads, axis=0)
    sc = _dot_t(ql, cc) + _dot_t(qp[:, 0:MLA_ROPE], pc)
    sn = jnp.where(mask_n, _dot_t(ql, cn) + _dot_t(qp, pn_ref[...]), NEG_INF)
    m = jnp.maximum(jnp.max(sc, axis=-1, keepdims=True), jnp.max(sn, axis=-1, keepdims=True))
    ec = jnp.exp2(sc - m)
    en = jnp.exp2(sn - m)
    l = jnp.sum(ec, axis=-1, keepdims=True) + jnp.sum(en, axis=-1, keepdims=True)
    ol = ((_dot(ec.astype(BF16), cc) + _dot(en.astype(BF16), cn)) / l).astype(BF16)
    for h in range(heads):
        o_ref[:, h * MLA_V:(h + 1) * MLA_V] = _dot(ol[h * tq:(h + 1) * tq], wv_ref[h]).astype(BF16)


def _mla_sample(q, ckvb, kpeb, cache_ckv, cache_kpe, wk_t, wv, nbatch, seq_len, heads):
    n = q.shape[0]
    past, kvr = cache_ckv.shape[1], cache_ckv.shape[2]
    assert (past // CHUNK) * CHUNK == past
    cc = cache_ckv.reshape(nbatch * past, kvr)
    pc = cache_kpe.reshape(nbatch * past, MLA_ROPE)
    kern = functools.partial(_mla_sample_kernel, heads=heads, past=past)
    row = lambda b: (b, 0)
    return pl.pallas_call(
        kern,
        grid=(nbatch,),
        in_specs=[pl.BlockSpec((seq_len, heads * MXU_DIM), row),
                  pl.BlockSpec((past, kvr), row), pl.BlockSpec((past, MLA_ROPE), row),
                  pl.BlockSpec((seq_len, kvr), row), pl.BlockSpec((seq_len, LANES), row),
                  _resident(wk_t.shape), _resident(wv.shape)],
        out_specs=pl.BlockSpec((seq_len, heads * MLA_V), row),
        out_shape=jax.ShapeDtypeStruct((n, heads * MLA_V), BF16),
        scratch_shapes=[pltpu.VMEM((heads * seq_len, kvr), BF16), pltpu.VMEM((heads * seq_len, LANES), BF16)],
        compiler_params=_cparams(1),
        name="mla_attn_sample",
    )(q, cc, pc, ckvb, kpeb, wk_t, wv)


def _rope_tables(pos):
    half = MLA_ROPE // 2
    inv = ROPE_BASE ** (-jnp.arange(half, dtype=F32) / half)
    ang = pos.astype(F32)[:, None] * inv[None, :]
    cos, sin = jnp.cos(ang), jnp.sin(ang)
    zero = jnp.zeros_like(cos)
    return (jnp.concatenate([cos, cos, zero, zero], axis=1),
            jnp.concatenate([-sin, sin, zero, zero], axis=1))


def _swap_halves(w):
    half = MLA_ROPE // 2
    return jnp.concatenate([w[..., half:], w[..., :half]], axis=-1)


def _prepare_weights(p):
    w = {}
    heads = p['mla_w_uq'].shape[2]
    w['heads'] = heads
    w['norm_mix'] = p['norm_mix'][:, None, :]
    w['norm_ffn'] = p['norm_ffn'][:, None, :]
    w['norm_final'] = p['norm_final'][None, :]
    w['w_in_even'] = p['w_in_even'][0].astype(BF16)
    w['w_out_even'] = p['w_out_even'][0].astype(BF16)
    w['s5'] = dict(a_re=p['s5_a_re'][0], a_im=p['s5_a_im'][0], b_re=p['s5_b_re'][0], b_im=p['s5_b_im'][0],
                   c_re=p['s5_c_re'][0], c_im=p['s5_c_im'][0], d=p['s5_d'][0], log_dt=p['s5_log_dt'][0],
                   w_glu=p['s5_w_glu'][0], b_glu=p['s5_b_glu'][0])
    w['lam'] = [p[k][0][None, :] for k in ('diff_lambda_q1', 'diff_lambda_k1', 'diff_lambda_q2', 'diff_lambda_k2')]
    w['subln'] = p['diff_subln'][0][None, :]
    wi = p['w_in_odd'][0]
    qr = p['mla_q_norm'].shape[1]
    kvr = p['mla_kv_norm'].shape[1]
    wpe = wi[:, qr + kvr:]
    w['w_in_odd'] = jnp.concatenate([wi[:, :qr + kvr], wpe, _swap_halves(wpe)], axis=1).astype(BF16)
    w['gq'] = p['mla_q_norm'][0][None, :]
    w['gkv'] = p['mla_kv_norm'][0][None, :]
    wuq = p['mla_w_uq'][0]
    wuq = jnp.concatenate([wuq, _swap_halves(wuq[..., MLA_NOPE:])], axis=-1)
    w['w_uq'] = wuq.reshape(qr, heads * MXU_DIM).astype(BF16)
    wukv = p['mla_w_ukv'][0]
    w['w_kv'] = (wukv[..., :MLA_NOPE].reshape(kvr, heads * MLA_NOPE).astype(BF16),
                 wukv[..., MLA_NOPE:].reshape(kvr, heads * MLA_V).T.astype(BF16))
    w['w_uk_t'] = jnp.transpose(wukv[..., :MLA_NOPE], (1, 2, 0)).astype(BF16)
    w['w_uv'] = jnp.transpose(wukv[..., MLA_NOPE:], (1, 0, 2)).astype(BF16)
    w['w_out_odd'] = p['w_out_odd'][0].astype(BF16)
    w['ffn_w_in'] = p['ffn_w_in'].astype(BF16)
    w['ffn_w_down'] = p['ffn_w_down'].astype(BF16)
    w['ffn_conv_w'] = p['ffn_conv_w']
    w['ffn_conv_b'] = p['ffn_conv_b']
    return w


def _trunk(x3, s5_re0, s5_im0, k_past, v_past, ckv_past, kpe_past, conv0, pos0, w):
    nb, t, d = x3.shape
    n = nb * t
    x = x3.reshape(n, d)
    heads = w['heads']
    s5w = w['s5']['d'].size
    dw = (w['w_in_even'].shape[1] - s5w) // 3
    dheads = dw // LANES

    lam_init = 0.8 - 0.6 * math.exp(-0.3 * 0)
    u, q, k, v, kb, vb = _even_in(x, w['norm_mix'][0], w['w_in_even'], s5w, dw, v_transposed=k_past is None)
    y_s5, ht_re, ht_im = _s5(u, s5_re0, s5_im0, t, w['s5'])
    if k_past is None:
        o = _diff_prompt(q, kb, vb, w['lam'], w['subln'], nb, t, lam_init)
    else:
        o = _diff_sample(q, kb, vb, k_past, v_past, w['lam'], w['subln'], nb, t, lam_init)
    x, hn = _out_proj([y_s5, o], w['w_out_even'], x, w['norm_ffn'][0])
    act, conv_a = _ffn_up(hn, w['ffn_w_in'], 0, w['ffn_conv_w'][0], w['ffn_conv_b'][0], conv0[0], t)
    x, hn = _out_proj([act], w['ffn_w_down'], x, w['norm_mix'][1], tm_pref=256, layer=0)

    cos, sin = _rope_tables(pos0 + jnp.arange(t, dtype=jnp.int32))
    if ckv_past is None:
        qm, ckv, ckvb, kpe, kpeb, kn, vm = _odd_in(hn, w['w_in_odd'], w['gq'], w['gkv'], w['w_uq'], cos, sin,
                                                   w['w_kv'], t, heads)
        om = _mla_prompt(qm, kn, kpeb, vm, nb, t, heads)
    else:
        qm, ckv, ckvb, kpe, kpeb = _odd_in(hn, w['w_in_odd'], w['gq'], w['gkv'], w['w_uq'], cos, sin,
                                           None, t, heads)
        om = _mla_sample(qm, ckvb, kpeb, ckv_past, kpe_past, w['w_uk_t'], w['w_uv'], nb, t, heads)
    x, hn = _out_proj([om], w['w_out_odd'], x, w['norm_ffn'][1])
    act, conv_b = _ffn_up(hn, w['ffn_w_in'], 1, w['ffn_conv_w'][1], w['ffn_conv_b'][1], conv0[1], t)
    (y,) = _out_proj([act], w['ffn_w_down'], x, w['norm_final'], final=True, tm_pref=256, layer=1)

    groups, n_state = w['s5']['a_re'].shape
    return (y.reshape(nb, t, d), ht_re[None], ht_im[None],
            k.reshape(1, nb, t, dheads, LANES), v.reshape(1, nb, t, dheads, LANES),
            ckv.reshape(1, nb, t, -1), kpe.reshape(1, nb, t, MLA_ROPE), jnp.stack([conv_a, conv_b]))


def kernel(x_prompt, x_sample, state_s5_re, state_s5_im, cache_diff_k, cache_diff_v, cache_mla_ckv, cache_mla_kpe, state_ffn_conv, norm_mix, norm_ffn, norm_final, w_in_even, w_out_even, s5_a_re, s5_a_im, s5_b_re, s5_b_im, s5_c_re, s5_c_im, s5_d, s5_log_dt, s5_w_glu, s5_b_glu, diff_lambda_q1, diff_lambda_k1, diff_lambda_q2, diff_lambda_k2, diff_subln, w_in_odd, mla_q_norm, mla_kv_norm, mla_w_uq, mla_w_ukv, w_out_odd, ffn_w_in, ffn_conv_w, ffn_conv_b, ffn_w_down):
    w = _prepare_weights(dict(
        norm_mix=norm_mix, norm_ffn=norm_ffn, norm_final=norm_final, w_in_even=w_in_even, w_out_even=w_out_even,
        s5_a_re=s5_a_re, s5_a_im=s5_a_im, s5_b_re=s5_b_re, s5_b_im=s5_b_im, s5_c_re=s5_c_re, s5_c_im=s5_c_im,
        s5_d=s5_d, s5_log_dt=s5_log_dt, s5_w_glu=s5_w_glu, s5_b_glu=s5_b_glu,
        diff_lambda_q1=diff_lambda_q1, diff_lambda_k1=diff_lambda_k1, diff_lambda_q2=diff_lambda_q2,
        diff_lambda_k2=diff_lambda_k2, diff_subln=diff_subln, w_in_odd=w_in_odd, mla_q_norm=mla_q_norm,
        mla_kv_norm=mla_kv_norm, mla_w_uq=mla_w_uq, mla_w_ukv=mla_w_ukv, w_out_odd=w_out_odd,
        ffn_w_in=ffn_w_in, ffn_conv_w=ffn_conv_w, ffn_conv_b=ffn_conv_b, ffn_w_down=ffn_w_down))
    nb_p = x_prompt.shape[0]
    groups, n_state = s5_a_re.shape[1:]
    d_ff = ffn_conv_b.shape[1]
    depth = ffn_conv_b.shape[0]
    s5_zero = jnp.zeros((nb_p, groups, n_state), F32)
    conv_zero = jnp.zeros((depth, nb_p, CONV_W - 1, d_ff), F32)
    (y_p, re_p, im_p, k_p, v_p, ckv_p, kpe_p, conv_p) = _trunk(
        x_prompt, s5_zero, s5_zero, None, None, None, None, conv_zero, 0, w)
    past = cache_diff_k.shape[2]
    (y_s, re_s, im_s, k_s, v_s, ckv_s, kpe_s, conv_s) = _trunk(
        x_sample, state_s5_re[0], state_s5_im[0], cache_diff_k[0], cache_diff_v[0], cache_mla_ckv[0],
        cache_mla_kpe[0], state_ffn_conv, past, w)
    return (y_p, y_s, re_p, im_p, re_s, im_s, k_p, v_p, k_s, v_s, ckv_p, kpe_p, ckv_s, kpe_s, conv_p, conv_s)
```
